```python
import math
import jax, jax.numpy as jnp
from jax import lax
import numpy as np

D_MODEL = 1024
BATCH = 4
SEQ = 8192
DEPTH = 1
DEC_BATCH = 16
DEC_SEQ = 32
PAST_LEN = 4096

CHUNK = 64
Q_BLOCK = 128
MIX_WIDTH = D_MODEL
SB_HEAD_DIM = 64
SB_WIDTH = MIX_WIDTH // 2
SB_HEADS = SB_WIDTH // SB_HEAD_DIM
GDN_HEAD_DIM = 128
GDN_WIDTH = MIX_WIDTH - SB_WIDTH
GDN_HEADS = GDN_WIDTH // GDN_HEAD_DIM
CONV_WIDTH = 4
CONV_CH = 3 * GDN_WIDTH
IN_WIDTH = 3 * SB_WIDTH + 4 * GDN_WIDTH + 2 * GDN_HEADS
N_EXPERTS = 32
TOP_K = 4
D_FF = D_MODEL
SWIGLU_LIMIT = 7.0
SWIGLU_ALPHA = 1.702
NORM_EPS = 1e-6
L2_EPS = 1e-6

kernel_name = "hymba_stickbreak_gdn_moe_stream_step"


def rms_norm(x, g):
    xf = x.astype(jnp.float32)
    y = xf * lax.rsqrt(jnp.mean(xf * xf, axis=-1, keepdims=True) + NORM_EPS)
    return (y * g.astype(jnp.float32)).astype(x.dtype)


def l2_normalize(x):
    xf = x.astype(jnp.float32)
    return xf * lax.rsqrt(jnp.sum(xf * xf, axis=-1, keepdims=True) + L2_EPS)


def stick_breaking_block(q, k, v, q_pos, k_pos):
    z = jnp.einsum('bqhd,bkhd->bhqk', q, k).astype(jnp.float32) * (SB_HEAD_DIM ** -0.5)
    mask = k_pos[None, :] < q_pos[:, None]
    log_keep = jnp.where(mask, -jax.nn.softplus(z), 0.0)
    later = lax.cumsum(log_keep, axis=3, reverse=True) - log_keep
    w = jnp.where(mask, jnp.exp(jax.nn.log_sigmoid(z) + later), 0.0)
    return jnp.einsum('bhqk,bkhd->bqhd', w.astype(v.dtype), v)


def stick_breaking_prompt(q, k, v):
    bsz, L, H, d = q.shape
    n_blk = L // Q_BLOCK
    k_pos = jnp.arange(L)

    def one_block(i):
        start = i * Q_BLOCK
        qb = lax.dynamic_slice_in_dim(q, start, Q_BLOCK, axis=1)
        return stick_breaking_block(qb, k, v, start + jnp.arange(Q_BLOCK), k_pos)

    out = lax.map(one_block, jnp.arange(n_blk))
    return jnp.moveaxis(out, 0, 1).reshape(bsz, L, H, d)


def causal_conv_silu(x, buf, w):
    L = x.shape[1]
    xp = jnp.concatenate([buf.astype(x.dtype), x], axis=1)
    y = sum(xp[:, i:i + L] * w[i] for i in range(CONV_WIDTH))
    return jax.nn.silu(y), xp[:, -(CONV_WIDTH - 1):]


def gated_delta_chunked(q, k, v, g, beta, s0):
    f32 = jnp.float32
    bsz, L, H, dk = q.shape
    dv = v.shape[-1]
    C = min(CHUNK, L)
    N = L // C

    def blocks(t):
        t = t.astype(f32).reshape((bsz, N, C, H) + t.shape[3:])
        return jnp.moveaxis(t, 3, 2).swapaxes(0, 1)

    q = blocks(q) * (dk ** -0.5)
    k, v, g, beta = blocks(k), blocks(v), blocks(g), blocks(beta)
    gc = jnp.cumsum(g, axis=-1)
    idx = jnp.arange(C)
    incl = idx[:, None] >= idx[None, :]
    strict = idx[:, None] > idx[None, :]
    decay = jnp.exp(jnp.where(incl, gc[..., :, None] - gc[..., None, :], -jnp.inf))
    kb = k * beta[..., None]
    lower = jnp.where(strict, jnp.einsum('nbhic,nbhjc->nbhij', kb, k) * decay, 0.0)
    rhs = jnp.concatenate([v * beta[..., None], kb * jnp.exp(gc)[..., None]], axis=-1)
    sol = lax.linalg.triangular_solve(lower, rhs, left_side=True, lower=True, unit_diagonal=True)
    u, w = sol[..., :dv], sol[..., dv:]
    qk = jnp.einsum('nbhic,nbhjc->nbhij', q, k) * decay
    q_dec = q * jnp.exp(gc)[..., None]
    k_dec = k * jnp.exp(gc[..., -1:] - gc)[..., None]
    g_last = jnp.exp(gc[..., -1])

    def step(S, xs):
        u_c, w_c, qk_c, qd_c, kd_c, gl_c = xs
        v_new = u_c - jnp.einsum('bhck,bhkv->bhcv', w_c, S)
        o = jnp.einsum('bhck,bhkv->bhcv', qd_c, S) + jnp.einsum('bhij,bhjv->bhiv', qk_c, v_new)
        S = S * gl_c[..., None, None] + jnp.einsum('bhck,bhcv->bhkv', kd_c, v_new)
        return S, o

    S, o = lax.scan(step, s0.astype(f32), (u, w, qk, q_dec, k_dec, g_last))
    o = jnp.moveaxis(o.swapaxes(0, 1), 2, 3).reshape(bsz, L, H, dv)
    return o, S


def gdn_group(q, k, v, gate, a, b, conv_buf, s0, conv_w, a_log, dt_bias, norm_gdn):
    bsz, L, _ = q.shape
    qkv, new_buf = causal_conv_silu(jnp.concatenate([q, k, v], axis=-1), conv_buf, conv_w)
    q, k, v = jnp.split(qkv, 3, axis=-1)
    heads = lambda t: t.reshape(bsz, L, GDN_HEADS, GDN_HEAD_DIM)
    q, k, v = l2_normalize(heads(q)), l2_normalize(heads(k)), heads(v)
    g = -jnp.exp(a_log.astype(jnp.float32)) * jax.nn.softplus(a.astype(jnp.float32) + dt_bias.astype(jnp.float32))
    beta = jax.nn.sigmoid(b.astype(jnp.float32))
    o, S = gated_delta_chunked(q, k, v, g, beta, s0)
    o = rms_norm(o, norm_gdn) * jax.nn.silu(heads(gate).astype(jnp.float32))
    return o.reshape(bsz, L, GDN_WIDTH).astype(gate.dtype), new_buf, S.astype(gate.dtype)


def token_mixing(xn, k_past, v_past, conv_buf, s0, w_in, conv_w, a_log, dt_bias, norm_sb, norm_gdn, w_out):
    bsz, L, _ = xn.shape
    proj = xn @ w_in
    sizes = [SB_WIDTH] * 3 + [GDN_WIDTH] * 4 + [GDN_HEADS] * 2
    qa, ka, va, qb, kb, vb, gate, a, b = jnp.split(proj, np.cumsum(sizes)[:-1].tolist(), axis=-1)
    sb_heads = lambda t: t.reshape(bsz, L, SB_HEADS, SB_HEAD_DIM)
    qa, ka, va = sb_heads(qa), sb_heads(ka), sb_heads(va)
    if k_past is None:
        o_sb = stick_breaking_prompt(qa, ka, va)
    else:
        past = k_past.shape[1]
        k_all = jnp.concatenate([k_past.astype(ka.dtype), ka], axis=1)
        v_all = jnp.concatenate([v_past.astype(va.dtype), va], axis=1)
        o_sb = stick_breaking_block(qa, k_all, v_all, past + jnp.arange(L), jnp.arange(past + L))
    o_sb = rms_norm(o_sb, norm_sb).reshape(bsz, L, SB_WIDTH)
    o_gdn, new_buf, new_s = gdn_group(qb, kb, vb, gate, a, b, conv_buf, s0, conv_w, a_log, dt_bias, norm_gdn)
    mixed = jnp.concatenate([o_sb, o_gdn], axis=-1)
    return mixed @ w_out, ka, va, new_buf, new_s


def moe_ffn(x, router_w, router_b, w_gate, b_gate, w_up, b_up, w_down, b_down):
    logits = (x @ router_w + router_b).astype(jnp.float32)
    top_val, top_idx = lax.top_k(logits, TOP_K)
    probs = jax.nn.softmax(top_val, axis=-1)
    combine = jnp.sum(jax.nn.one_hot(top_idx, N_EXPERTS, dtype=jnp.float32) * probs[..., None], axis=1)
    y = jnp.zeros(x.shape, jnp.float32)
    for e in range(N_EXPERTS):
        gt = jnp.minimum(x @ w_gate[e] + b_gate[e], SWIGLU_LIMIT)
        up = jnp.clip(x @ w_up[e] + b_up[e], -SWIGLU_LIMIT, SWIGLU_LIMIT)
        h = (up + 1.0) * (gt * jax.nn.sigmoid(SWIGLU_ALPHA * gt))
        y = y + combine[:, e:e + 1] * (h @ w_down[e] + b_down[e])
    return y.astype(x.dtype)


def setup_inputs(seed: int = 0) -> dict:
    key = jax.random.key(seed)
    ks = jax.random.split(key, 32)
    f32 = jnp.float32
    nrm = lambda kk, shape, scale: jax.random.normal(kk, shape, f32) * scale
    dt = jax.random.uniform(ks[9], (DEPTH, GDN_HEADS), f32, minval=0.001, maxval=0.1)
    return {
        "x_prompt": nrm(ks[0], (BATCH, SEQ, D_MODEL), 1.0),
        "x_sample": nrm(ks[1], (DEC_BATCH, DEC_SEQ, D_MODEL), 1.0),
        "cache_sb_k": nrm(ks[2], (DEPTH, DEC_BATCH, PAST_LEN, SB_HEADS, SB_HEAD_DIM), 1.0),
        "cache_sb_v": nrm(ks[3], (DEPTH, DEC_BATCH, PAST_LEN, SB_HEADS, SB_HEAD_DIM), 1.0),
        "cache_gdn_conv": nrm(ks[4], (DEPTH, DEC_BATCH, CONV_WIDTH - 1, CONV_CH), 1.0),
        "state_gdn": nrm(ks[5], (DEPTH, DEC_BATCH, GDN_HEADS, GDN_HEAD_DIM, GDN_HEAD_DIM), 0.1),
        "norm_mix": 1.0 + nrm(ks[6], (DEPTH, D_MODEL), 0.02),
        "w_in": nrm(ks[7], (DEPTH, D_MODEL, IN_WIDTH), D_MODEL ** -0.5),
        "conv_w": nrm(ks[8], (DEPTH, CONV_WIDTH, CONV_CH), CONV_WIDTH ** -0.5),
        "a_log": jnp.log(jax.random.uniform(ks[10], (DEPTH, GDN_HEADS), f32, minval=1.0, maxval=16.0)),
        "dt_bias": dt + jnp.log(-jnp.expm1(-dt)),
        "norm_sb": 1.0 + nrm(ks[11], (DEPTH, SB_HEAD_DIM), 0.02),
        "norm_gdn": 1.0 + nrm(ks[12], (DEPTH, GDN_HEAD_DIM), 0.02),
        "w_out": nrm(ks[13], (DEPTH, MIX_WIDTH, D_MODEL), MIX_WIDTH ** -0.5),
        "norm_ffn": 1.0 + nrm(ks[14], (DEPTH, D_MODEL), 0.02),
        "router_w": nrm(ks[15], (DEPTH, D_MODEL, N_EXPERTS), D_MODEL ** -0.5),
        "router_b": nrm(ks[16], (DEPTH, N_EXPERTS), 0.01),
        "w_gate": nrm(ks[17], (DEPTH, N_EXPERTS, D_MODEL, D_FF), D_MODEL ** -0.5),
        "b_gate": nrm(ks[18], (DEPTH, N_EXPERTS, D_FF), 0.02),
        "w_up": nrm(ks[19], (DEPTH, N_EXPERTS, D_MODEL, D_FF), D_MODEL ** -0.5),
        "b_up": nrm(ks[20], (DEPTH, N_EXPERTS, D_FF), 0.02),
        "w_down": nrm(ks[21], (DEPTH, N_EXPERTS, D_FF, D_MODEL), D_FF ** -0.5),
        "b_down": nrm(ks[22], (DEPTH, N_EXPERTS, D_MODEL), 0.02),
        "norm_final": 1.0 + nrm(ks[23], (D_MODEL,), 0.02),
    }


def reference(x_prompt, x_sample, cache_sb_k, cache_sb_v, cache_gdn_conv, state_gdn,
              norm_mix, w_in, conv_w, a_log, dt_bias, norm_sb, norm_gdn, w_out,
              norm_ffn, router_w, router_b, w_gate, b_gate, w_up, b_up, w_down, b_down, norm_final):
    hp, hs = x_prompt, x_sample
    n_p = hp.shape[0] * hp.shape[1]
    kp_l, vp_l, cp_l, sp_l, ks_l, vs_l, cs_l, ss_l = [], [], [], [], [], [], [], []
    for l in range(DEPTH):
        mix_w = (w_in[l], conv_w[l], a_log[l], dt_bias[l], norm_sb[l], norm_gdn[l], w_out[l])
        zero_buf = jnp.zeros((hp.shape[0], CONV_WIDTH - 1, CONV_CH), hp.dtype)
        zero_s = jnp.zeros((hp.shape[0], GDN_HEADS, GDN_HEAD_DIM, GDN_HEAD_DIM), jnp.float32)
        mp, kp, vp, cp, sp = token_mixing(rms_norm(hp, norm_mix[l]), None, None, zero_buf, zero_s, *mix_w)
        hp = hp + mp
        ms, kss, vss, css, sss = token_mixing(rms_norm(hs, norm_mix[l]), cache_sb_k[l], cache_sb_v[l],
                                              cache_gdn_conv[l], state_gdn[l], *mix_w)
        hs = hs + ms
        flat = jnp.concatenate([hp.reshape(-1, D_MODEL), hs.reshape(-1, D_MODEL)], axis=0)
        f = moe_ffn(rms_norm(flat, norm_ffn[l]), router_w[l], router_b[l], w_gate[l], b_gate[l],
                    w_up[l], b_up[l], w_down[l], b_down[l])
        hp = hp + f[:n_p].reshape(hp.shape)
        hs = hs + f[n_p:].reshape(hs.shape)
        kp_l.append(kp); vp_l.append(vp); cp_l.append(cp); sp_l.append(sp)
        ks_l.append(kss); vs_l.append(vss); cs_l.append(css); ss_l.append(sss)
    y_prompt = rms_norm(hp, norm_final)
    y_sample = rms_norm(hs, norm_final)
    return (y_prompt, y_sample,
            jnp.stack(kp_l), jnp.stack(vp_l), jnp.stack(cp_l), jnp.stack(sp_l),
            jnp.stack(ks_l), jnp.stack(vs_l), jnp.stack(cs_l), jnp.stack(ss_l))
```

```python
import functools
import math

import jax
import jax.numpy as jnp
from jax import lax
from jax.experimental import pallas as pl
from jax.experimental.pallas import tpu as pltpu

F32 = jnp.float32
BF16 = jnp.bfloat16

D_MODEL = 1024
SB_HEAD_DIM = 64
SB_WIDTH = 512
SB_PAIRS = SB_WIDTH // 128
GDN_HEAD_DIM = 128
GDN_HEADS = 4
GDN_WIDTH = 512
CONV_WIDTH = 4
CONV_CH = 3 * GDN_WIDTH
IN_WIDTH = 3 * SB_WIDTH + 4 * GDN_WIDTH + 2 * GDN_HEADS
IN_WIDTH_PAD = 3 * SB_WIDTH + 4 * GDN_WIDTH + 128
N_EXPERTS = 32
TOP_K = 4
SWIGLU_LIMIT = 7.0
SWIGLU_ALPHA = 1.702
NORM_EPS = 1e-6
L2_EPS = 1e-6
CHUNK = 64

LANES = 128
VMEM_LIMIT_BYTES = 48 * 1024 * 1024
SB_LOG_CUTOFF = 104.0

TM_DENSE = 256
TQ_SB = 256
TK_SB = 128
SB_CACHE_WINDOW = 512
TM_ROWS = 256
TM_GROUP = 256


def _cparams(sem):
    return pltpu.CompilerParams(dimension_semantics=sem, vmem_limit_bytes=VMEM_LIMIT_BYTES)


def _softplus(z):
    return jnp.maximum(z, 0.0) + jnp.log(1.0 + jnp.exp(-jnp.abs(z)))


def _sigmoid(z):
    return 1.0 / (1.0 + jnp.exp(-z))


def _in_proj_kernel(x_ref, g_ref, w_ref, qsb_ref, ksb_ref, vsb_ref, kbf_ref, vbf_ref, gdn_ref, gate_ref, ab_ref):
    x = x_ref[...]
    xn = x * lax.rsqrt(jnp.mean(x * x, axis=-1, keepdims=True) + NORM_EPS) * g_ref[...]
    xn = xn.astype(BF16)

    def mm(lo, hi):
        return jnp.dot(xn, w_ref[:, lo:hi], preferred_element_type=F32)

    qsb_ref[...] = (mm(0, 512) * (SB_HEAD_DIM ** -0.5)).astype(BF16)
    k = mm(512, 1024)
    ksb_ref[...] = k
    kbf_ref[...] = k.astype(BF16)
    v = mm(1024, 1536)
    vsb_ref[...] = v
    vbf_ref[...] = v.astype(BF16)
    for j in range(3):
        gdn_ref[:, j * 512:(j + 1) * 512] = mm(1536 + j * 512, 2048 + j * 512)
    gate_ref[...] = mm(3072, 3584)
    ab_ref[...] = mm(3584, 3712)


def _in_proj(x2d, g_row, w_bf):
    n = x2d.shape[0]
    tm = min(TM_DENSE, n)
    row = lambda w: pl.BlockSpec((tm, w), lambda i: (i, 0))
    out_shape = (
        jax.ShapeDtypeStruct((n, SB_WIDTH), BF16),
        jax.ShapeDtypeStruct((n, SB_WIDTH), F32),
        jax.ShapeDtypeStruct((n, SB_WIDTH), F32),
        jax.ShapeDtypeStruct((n, SB_WIDTH), BF16),
        jax.ShapeDtypeStruct((n, SB_WIDTH), BF16),
        jax.ShapeDtypeStruct((n, CONV_CH), F32),
        jax.ShapeDtypeStruct((n, GDN_WIDTH), F32),
        jax.ShapeDtypeStruct((n, LANES), F32),
    )
    return pl.pallas_call(
        _in_proj_kernel,
        grid=(n // tm,),
        in_specs=[row(D_MODEL), pl.BlockSpec((1, D_MODEL), lambda i: (0, 0)),
                  pl.BlockSpec((D_MODEL, IN_WIDTH_PAD), lambda i: (0, 0))],
        out_specs=(row(SB_WIDTH), row(SB_WIDTH), row(SB_WIDTH), row(SB_WIDTH), row(SB_WIDTH),
                   row(CONV_CH), row(GDN_WIDTH), row(LANES)),
        out_shape=out_shape,
        compiler_params=_cparams(("arbitrary",)),
        name="in_proj",
    )(x2d, g_row, w_bf)


def _sb_block(q_h, k_blk, v_blk, qpos, kpos, c):
    tk = k_blk.shape[0]
    z = lax.dot_general(q_h, k_blk, (((1,), (1,)), ((), ())), preferred_element_type=F32)
    visible = kpos < qpos
    sp = jnp.where(visible, _softplus(z), 0.0)
    jj = lax.broadcasted_iota(jnp.int32, (tk, tk), 0)
    ss = lax.broadcasted_iota(jnp.int32, (tk, tk), 1)
    tri = jnp.where(jj >= ss, 1.0, 0.0).astype(BF16)
    sp_hi = sp.astype(BF16)
    sp_lo = (sp - sp_hi.astype(F32)).astype(BF16)
    r = jnp.dot(sp_hi, tri, preferred_element_type=F32) + jnp.dot(sp_lo, tri, preferred_element_type=F32)
    w = jnp.where(visible, jnp.exp(z - r - c), 0.0)
    pv = jnp.dot(w.astype(BF16), v_blk, preferred_element_type=F32)
    return pv, r[:, 0:1]


def _sb_pair_step(q0, q1, k_blk, v_blk, qpos, kpos, acc_ref, c_ref):
    tk = k_blk.shape[0]
    lane = lax.broadcasted_iota(jnp.int32, (1, LANES), 1)
    pv0, m0 = _sb_block(q0, k_blk, v_blk, qpos, kpos, c_ref[0][:, :tk])
    pv1, m1 = _sb_block(q1, k_blk, v_blk, qpos, kpos, c_ref[1][:, :tk])
    acc_ref[...] += jnp.where(lane < SB_HEAD_DIM, pv0, pv1)
    c0 = c_ref[0] + m0
    c1 = c_ref[1] + m1
    c_ref[0] = c0
    c_ref[1] = c1
    return jnp.minimum(jnp.min(c0), jnp.min(c1))


def _sb_finish(acc, nsb_row):
    lane = lax.broadcasted_iota(jnp.int32, (1, LANES), 1)
    first = lane < SB_HEAD_DIM
    sq = acc * acc
    s_all = jnp.sum(sq, axis=-1, keepdims=True)
    s0 = jnp.sum(jnp.where(first, sq, 0.0), axis=-1, keepdims=True)
    ms = jnp.where(first, s0, s_all - s0) * (1.0 / SB_HEAD_DIM)
    return acc * lax.rsqrt(ms + NORM_EPS) * nsb_row


def _split_heads(q):
    lane = lax.broadcasted_iota(jnp.int32, (1, LANES), 1)
    zero = jnp.zeros_like(q)
    return jnp.where(lane < SB_HEAD_DIM, q, zero), jnp.where(lane >= SB_HEAD_DIM, q, zero)


def _sb_prompt_kernel(q_ref, k_ref, v_ref, nsb_ref, o_ref, acc_ref, c_ref, *, tq, tk):
    i = pl.program_id(2)
    q0, q1 = _split_heads(q_ref[0])
    acc_ref[...] = jnp.zeros_like(acc_ref)
    c_ref[...] = jnp.zeros_like(c_ref)
    qpos = i * tq + lax.broadcasted_iota(jnp.int32, (tq, 1), 0)

    def cond(carry):
        j, cmin = carry
        return jnp.logical_and(j >= 0, cmin < SB_LOG_CUTOFF)

    def body(carry):
        j, _ = carry
        ks = pl.multiple_of(j * tk, tk)
        kpos = ks + lax.broadcasted_iota(jnp.int32, (1, tk), 1)
        cmin = _sb_pair_step(q0, q1, k_ref[0, pl.ds(ks, tk), :], v_ref[0, pl.ds(ks, tk), :], qpos, kpos,
                             acc_ref, c_ref)
        return j - 1, cmin

    lax.while_loop(cond, body, ((i + 1) * (tq // tk) - 1, jnp.float32(0.0)))
    o_ref[0] = _sb_finish(acc_ref[...], nsb_ref[...]).astype(o_ref.dtype)


def _sb_prompt(q, k, v, nsb_row):
    b, l, _ = q.shape
    tq = min(TQ_SB, l)
    tk = min(TK_SB, tq)
    qspec = pl.BlockSpec((1, tq, LANES), lambda bb, hp, i: (bb, i, hp))
    kvspec = pl.BlockSpec((1, l, LANES), lambda bb, hp, i: (bb, 0, hp))
    return pl.pallas_call(
        functools.partial(_sb_prompt_kernel, tq=tq, tk=tk),
        grid=(b, SB_PAIRS, l // tq),
        in_specs=[qspec, kvspec, kvspec, pl.BlockSpec((1, LANES), lambda bb, hp, i: (0, 0))],
        out_specs=qspec,
        out_shape=jax.ShapeDtypeStruct((b, l, SB_WIDTH), BF16),
        scratch_shapes=[pltpu.VMEM((tq, LANES), F32), pltpu.VMEM((2, tq, LANES), F32)],
        compiler_params=_cparams(("arbitrary", "arbitrary", "arbitrary")),
        name="sb_prompt",
    )(q, k, v, nsb_row)


def _sb_sample_kernel(q_ref, kn_ref, vn_ref, kw_ref, vw_ref, kc_hbm, vc_hbm, nsb_ref, o_ref,
                      acc_ref, c_ref, kbuf, vbuf, sem, *, past, window, tk):
    b = pl.program_id(0)
    hp = pl.program_id(1)
    t = q_ref.shape[1]
    q0, q1 = _split_heads(q_ref[0])
    acc_ref[...] = jnp.zeros_like(acc_ref)
    c_ref[...] = jnp.zeros_like(c_ref)
    qpos = past + lax.broadcasted_iota(jnp.int32, (t, 1), 0)

    kpos_new = past + lax.broadcasted_iota(jnp.int32, (1, t), 1)
    cmin0 = _sb_pair_step(q0, q1, kn_ref[0], vn_ref[0], qpos, kpos_new, acc_ref, c_ref)

    def wcond(carry):
        j, cmin = carry
        return jnp.logical_and(j >= 0, cmin < SB_LOG_CUTOFF)

    def wbody(carry):
        j, _ = carry
        ws = pl.multiple_of(j * tk, tk)
        kpos = (past - window) + ws + lax.broadcasted_iota(jnp.int32, (1, tk), 1)
        cmin = _sb_pair_step(q0, q1, kw_ref[0, pl.ds(ws, tk), :].astype(BF16),
                             vw_ref[0, pl.ds(ws, tk), :].astype(BF16), qpos, kpos, acc_ref, c_ref)
        return j - 1, cmin

    _, cmin1 = lax.while_loop(wcond, wbody, (window // tk - 1, cmin0))

    n_old = (past - window) // tk
    if n_old > 0:
        def fetch(j):
            rows = pl.ds(pl.multiple_of(j * tk, tk), tk)
            cols = pl.ds(pl.multiple_of(hp * LANES, LANES), LANES)
            return (pltpu.make_async_copy(kc_hbm.at[b, rows, cols], kbuf, sem.at[0]),
                    pltpu.make_async_copy(vc_hbm.at[b, rows, cols], vbuf, sem.at[1]))

        def obody(carry):
            j, _ = carry
            ck, cv = fetch(j)
            ck.start()
            cv.start()
            ck.wait()
            cv.wait()
            kpos = j * tk + lax.broadcasted_iota(jnp.int32, (1, tk), 1)
            cmin = _sb_pair_step(q0, q1, kbuf[...].astype(BF16), vbuf[...].astype(BF16), qpos, kpos,
                                 acc_ref, c_ref)
            return j - 1, cmin

        lax.while_loop(wcond, obody, (n_old - 1, cmin1))

    o_ref[0] = _sb_finish(acc_ref[...], nsb_ref[...]).astype(o_ref.dtype)


def _sb_sample(q, k_new, v_new, k_cache, v_cache, nsb_row):
    b, t, _ = q.shape
    past = k_cache.shape[1]
    tk = min(TK_SB, past)
    window = min(SB_CACHE_WINDOW, past)
    assert past % window == 0 and window % tk == 0 and (past - window) % tk == 0
    new_spec = pl.BlockSpec((1, t, LANES), lambda bb, hp: (bb, 0, hp))
    win_spec = pl.BlockSpec((1, window, LANES), lambda bb, hp: (bb, past // window - 1, hp))
    any_spec = pl.BlockSpec(memory_space=pl.ANY)
    return pl.pallas_call(
        functools.partial(_sb_sample_kernel, past=past, window=window, tk=tk),
        grid=(b, SB_PAIRS),
        in_specs=[new_spec, new_spec, new_spec, win_spec, win_spec, any_spec, any_spec,
                  pl.BlockSpec((1, LANES), lambda bb, hp: (0, 0))],
        out_specs=new_spec,
        out_shape=jax.ShapeDtypeStruct((b, t, SB_WIDTH), BF16),
        scratch_shapes=[pltpu.VMEM((t, LANES), F32), pltpu.VMEM((2, t, LANES), F32),
                        pltpu.VMEM((tk, LANES), F32), pltpu.VMEM((tk, LANES), F32),
                        pltpu.SemaphoreType.DMA((2,))],
        compiler_params=_cparams(("arbitrary", "arbitrary")),
        name="sb_sample",
    )(q, k_new, v_new, k_cache, v_cache, k_cache, v_cache, nsb_row)


def _gdn_kernel(qkv_ref, gate_ref, ab_ref, cinit_ref, sinit_ref, cw_ref, alog_ref, dtb_ref, ng_ref,
                o_ref, cout_ref, sout_ref, xp_ref, s_ref, *, chunk):
    c = pl.program_id(1)
    nc = pl.num_programs(1)

    @pl.when(c == 0)
    def _():
        xp_ref[0:8, :] = cinit_ref[0]
        s_ref[...] = sinit_ref[0]

    xp_ref[8:8 + chunk, :] = qkv_ref[0]
    y = cw_ref[0:1, :] * xp_ref[5:5 + chunk, :]
    for i in range(1, CONV_WIDTH):
        y = y + cw_ref[i:i + 1, :] * xp_ref[5 + i:5 + i + chunk, :]
    y = y * _sigmoid(y)
    tail = xp_ref[chunk:chunk + 8, :]
    xp_ref[0:8, :] = tail

    @pl.when(c == nc - 1)
    def _():
        cout_ref[0] = tail

    ab = ab_ref[0]
    g_tile = -jnp.exp(alog_ref[...]) * _softplus(ab + dtb_ref[...])
    beta_tile = _sigmoid(ab)
    ii = lax.broadcasted_iota(jnp.int32, (chunk, chunk), 0)
    jj = lax.broadcasted_iota(jnp.int32, (chunk, chunk), 1)
    tri = jnp.where(ii >= jj, 1.0, 0.0).astype(F32)
    gc_tile = jnp.dot(tri, g_tile, preferred_element_type=F32, precision=lax.Precision.HIGHEST)
    gc_t = gc_tile.T
    scale = GDN_HEAD_DIM ** -0.5

    for h in range(GDN_HEADS):
        sl = slice(h * GDN_HEAD_DIM, (h + 1) * GDN_HEAD_DIM)
        qh = y[:, sl]
        kh = y[:, GDN_WIDTH + h * GDN_HEAD_DIM:GDN_WIDTH + (h + 1) * GDN_HEAD_DIM]
        vh = y[:, 2 * GDN_WIDTH + h * GDN_HEAD_DIM:2 * GDN_WIDTH + (h + 1) * GDN_HEAD_DIM]
        qn = qh * lax.rsqrt(jnp.sum(qh * qh, axis=-1, keepdims=True) + L2_EPS)
        kn = kh * lax.rsqrt(jnp.sum(kh * kh, axis=-1, keepdims=True) + L2_EPS)
        gcol = gc_tile[:, h:h + 1]
        grow = gc_t[h:h + 1, :]
        glast = gc_tile[chunk - 1:chunk, h:h + 1]
        bcol = beta_tile[:, GDN_HEADS + h:GDN_HEADS + h + 1]
        decay = jnp.exp(jnp.where(ii >= jj, gcol - grow, -1e30))
        egc = jnp.exp(gcol)
        kb = kn * bcol
        qs = qn * scale

        a2 = lax.dot_general(jnp.concatenate([kb, qs], axis=0).astype(BF16), kn.astype(BF16),
                             (((1,), (1,)), ((), ())), preferred_element_type=F32)
        lower = jnp.where(ii > jj, a2[:chunk] * decay, 0.0)
        qk = a2[chunk:] * decay

        rhs = jnp.concatenate([vh * bcol, kb * egc], axis=1)
        lb = lower.astype(BF16)
        sol = rhs - jnp.dot(lb, rhs.astype(BF16), preferred_element_type=F32)
        pw = jnp.dot(lb, lb, preferred_element_type=F32)
        n_fac = int(math.log2(chunk)) - 1
        for f in range(n_fac):
            pb = pw.astype(BF16)
            sol = sol + jnp.dot(pb, sol.astype(BF16), preferred_element_type=F32)
            if f + 1 < n_fac:
                pw = jnp.dot(pb, pb, preferred_element_type=F32)
        u = sol[:, :GDN_HEAD_DIM]
        w = sol[:, GDN_HEAD_DIM:]

        s_h = s_ref[h]
        ws = jnp.dot(jnp.concatenate([w, qs * egc], axis=0).astype(BF16), s_h.astype(BF16),
                     preferred_element_type=F32)
        v_new = u - ws[:chunk]
        o = ws[chunk:] + jnp.dot(qk.astype(BF16), v_new.astype(BF16), preferred_element_type=F32)
        kd = kn * jnp.exp(glast - gcol)
        s_ref[h] = s_h * jnp.exp(glast) + lax.dot_general(
            kd.astype(BF16), v_new.astype(BF16), (((0,), (0,)), ((), ())), preferred_element_type=F32)

        on = o * lax.rsqrt(jnp.mean(o * o, axis=-1, keepdims=True) + NORM_EPS) * ng_ref[...]
        gt = gate_ref[0][:, sl]
        o_ref[0, :, sl] = (on * (gt * _sigmoid(gt))).astype(o_ref.dtype)

    @pl.when(c == nc - 1)
    def _():
        sout_ref[0] = s_ref[...]


def _gdn(qkv, gate, ab, conv_init, s_init, cw8, alog_row, dtb_row, ng_row):
    b, l, _ = qkv.shape
    chunk = min(CHUNK, l)
    nc = l // chunk
    tok = lambda w: pl.BlockSpec((1, chunk, w), lambda bb, c: (bb, c, 0))
    const = lambda shape: pl.BlockSpec(shape, lambda bb, c: (0,) * len(shape))
    per_b3 = pl.BlockSpec((1, 8, CONV_CH), lambda bb, c: (bb, 0, 0))
    per_b4 = pl.BlockSpec((1, GDN_HEADS, GDN_HEAD_DIM, GDN_HEAD_DIM), lambda bb, c: (bb, 0, 0, 0))
    return pl.pallas_call(
        functools.partial(_gdn_kernel, chunk=chunk),
        grid=(b, nc),
        in_specs=[tok(CONV_CH), tok(GDN_WIDTH), tok(LANES), per_b3, per_b4,
                  const((8, CONV_CH)), const((1, LANES)), const((1, LANES)), const((1, LANES))],
        out_specs=(tok(GDN_WIDTH), per_b3, per_b4),
        out_shape=(jax.ShapeDtypeStruct((b, l, GDN_WIDTH), BF16),
                   jax.ShapeDtypeStruct((b, 8, CONV_CH), F32),
                   jax.ShapeDtypeStruct((b, GDN_HEADS, GDN_HEAD_DIM, GDN_HEAD_DIM), F32)),
        scratch_shapes=[pltpu.VMEM((chunk + 8, CONV_CH), F32),
                        pltpu.VMEM((GDN_HEADS, GDN_HEAD_DIM, GDN_HEAD_DIM), F32)],
        compiler_params=_cparams(("arbitrary", "arbitrary")),
        name="gdn",
    )(qkv, gate, ab, conv_init, s_init, cw8, alog_row, dtb_row, ng_row)


def _out_route_kernel(xp_ref, osbp_ref, ogdnp_ref, xs_ref, osbs_ref, ogdns_ref, wo_ref, nf_ref, rw_ref, rb_ref,
                      h_ref, xn_ref, route_ref, cnt_ref, run_ref, *, tm, steps_p):
    i = pl.program_id(0)

    @pl.when(i == 0)
    def _():
        run_ref[...] = jnp.zeros_like(run_ref)

    is_p = i < steps_p
    x = jnp.where(is_p, xp_ref[...], xs_ref[...])
    osb = jnp.where(is_p, osbp_ref[...], osbs_ref[...])
    ogdn = jnp.where(is_p, ogdnp_ref[...], ogdns_ref[...])
    h = (x
         + jnp.dot(osb, wo_ref[0:SB_WIDTH, :], preferred_element_type=F32)
         + jnp.dot(ogdn, wo_ref[SB_WIDTH:, :], preferred_element_type=F32))
    h_ref[...] = h
    xn = h * lax.rsqrt(jnp.mean(h * h, axis=-1, keepdims=True) + NORM_EPS) * nf_ref[...]
    xn_ref[...] = xn

    lane = lax.broadcasted_iota(jnp.int32, (1, LANES), 1)
    lane_f = lane.astype(F32)
    logits = jnp.dot(xn, rw_ref[...], preferred_element_type=F32, precision=lax.Precision.HIGHEST) + rb_ref[...]
    logits = jnp.where(lane < N_EXPERTS, logits, -jnp.inf)
    vals, hots = [], []
    for _ in range(TOP_K):
        m = jnp.max(logits, axis=-1, keepdims=True)
        idx = jnp.min(jnp.where(logits == m, lane_f, float(LANES)), axis=-1, keepdims=True)
        hot = lane_f == idx
        logits = jnp.where(hot, -jnp.inf, logits)
        vals.append(m)
        hots.append((hot, idx))
    exps = [jnp.exp(v - vals[0]) for v in vals]
    denom = exps[0] + exps[1] + exps[2] + exps[3]

    multi = jnp.zeros((tm, LANES), F32)
    for hot, _ in hots:
        multi = jnp.where(hot, 1.0, multi)
    ii = lax.broadcasted_iota(jnp.int32, (tm, tm), 0)
    jj = lax.broadcasted_iota(jnp.int32, (tm, tm), 1)
    earlier = jnp.where(ii > jj, 1.0, 0.0).astype(BF16)
    run = run_ref[0:1, :]
    rank_all = jnp.dot(earlier, multi.astype(BF16), preferred_element_type=F32) + run
    run_new = run + jnp.sum(multi, axis=0, keepdims=True)
    run_ref[...] = jnp.broadcast_to(run_new, run_ref.shape)
    cnt_ref[...] = jnp.broadcast_to(run_new, cnt_ref.shape)

    route = jnp.zeros((tm, LANES), F32)
    for k, (hot, idx) in enumerate(hots):
        rank = jnp.sum(jnp.where(hot, rank_all, 0.0), axis=-1, keepdims=True)
        route = jnp.where(lane == k, idx, route)
        route = jnp.where(lane == TOP_K + k, rank, route)
        route = jnp.where(lane == 2 * TOP_K + k, exps[k] / denom, route)
    route_ref[...] = route


def _out_route(xp, osbp, ogdnp, xs, osbs, ogdns, wo_bf, nf_row, rw_pad, rb_row):
    n_p, n_s = xp.shape[0], xs.shape[0]
    tm = min(TM_DENSE, n_p, n_s)
    steps_p, steps_s = n_p // tm, n_s // tm
    prow = lambda w: pl.BlockSpec((tm, w), lambda i: (jnp.minimum(i, steps_p - 1), 0))
    srow = lambda w: pl.BlockSpec((tm, w), lambda i: (jnp.maximum(i - steps_p, 0), 0))
    orow = lambda w: pl.BlockSpec((tm, w), lambda i: (i, 0))
    const = lambda shape: pl.BlockSpec(shape, lambda i: (0,) * len(shape))
    n_total = n_p + n_s
    out_shape = (jax.ShapeDtypeStruct((n_total, D_MODEL), F32),
                 jax.ShapeDtypeStruct((n_total, D_MODEL), F32),
                 jax.ShapeDtypeStruct((n_total, LANES), F32),
                 jax.ShapeDtypeStruct((8, LANES), F32))
    return pl.pallas_call(
        functools.partial(_out_route_kernel, tm=tm, steps_p=steps_p),
        grid=(steps_p + steps_s,),
        in_specs=[prow(D_MODEL), prow(SB_WIDTH), prow(GDN_WIDTH), srow(D_MODEL), srow(SB_WIDTH), srow(GDN_WIDTH),
                  const((D_MODEL, D_MODEL)), const((1, D_MODEL)), const((D_MODEL, LANES)), const((1, LANES))],
        out_specs=(orow(D_MODEL), orow(D_MODEL), orow(LANES), const((8, LANES))),
        out_shape=out_shape,
        scratch_shapes=[pltpu.VMEM((8, LANES), F32)],
        compiler_params=_cparams(("arbitrary",)),
        name="out_route",
    )(xp, osbp, ogdnp, xs, osbs, ogdns, wo_bf, nf_row, rw_pad, rb_row)


def _scatter_rows_kernel(pos_ref, x_hbm, xs_hbm, sem, *, tm):
    i = pl.program_id(0)

    def issue(t, carry):
        for k in range(TOP_K):
            p = pos_ref[t * TOP_K + k]
            pltpu.make_async_copy(x_hbm.at[pl.ds(i * tm + t, 1)], xs_hbm.at[pl.ds(p, 1)], sem).start()
        return carry

    lax.fori_loop(0, tm, issue, 0)

    def drain(t, carry):
        for k in range(TOP_K):
            pltpu.make_async_copy(x_hbm.at[pl.ds(0, 1)], xs_hbm.at[pl.ds(0, 1)], sem).wait()
        return carry

    lax.fori_loop(0, tm, drain, 0)


def _scatter_rows(pos_flat, xn):
    n = xn.shape[0]
    tm = min(TM_ROWS, n)
    return pl.pallas_call(
        functools.partial(_scatter_rows_kernel, tm=tm),
        grid=(n // tm,),
        in_specs=[pl.BlockSpec((tm * TOP_K,), lambda i: (i,), memory_space=pltpu.SMEM),
                  pl.BlockSpec(memory_space=pl.ANY)],
        out_specs=pl.BlockSpec(memory_space=pl.ANY),
        out_shape=jax.ShapeDtypeStruct((n * TOP_K, D_MODEL), xn.dtype),
        scratch_shapes=[pltpu.SemaphoreType.DMA(())],
        compiler_params=_cparams(("arbitrary",)),
        name="moe_scatter",
    )(pos_flat, xn)


def _experts_kernel(tile_ref, exp_ref, lo_ref, hi_ref, first_ref, nvalid_ref,
                    x_ref, wg_ref, bg_ref, wu_ref, bu_ref, wd_ref, bd_ref, y_ref, *, tmg):
    v = pl.program_id(0)

    @pl.when(v < nvalid_ref[0])
    def _():
        x = x_ref[...].astype(BF16)
        gt = jnp.minimum(jnp.dot(x, wg_ref[0], preferred_element_type=F32) + bg_ref[0], SWIGLU_LIMIT)
        up = jnp.clip(jnp.dot(x, wu_ref[0], preferred_element_type=F32) + bu_ref[0], -SWIGLU_LIMIT, SWIGLU_LIMIT)
        hid = (up + 1.0) * (gt * _sigmoid(SWIGLU_ALPHA * gt))
        y = jnp.dot(hid.astype(BF16), wd_ref[0], preferred_element_type=F32) + bd_ref[0]
        rows = tile_ref[v] * tmg + lax.broadcasted_iota(jnp.int32, (tmg, 1), 0)
        mine = jnp.logical_and(rows >= lo_ref[v], rows < hi_ref[v])

        @pl.when(first_ref[v] == 1)
        def _():
            y_ref[...] = jnp.where(mine, y, 0.0)

        @pl.when(first_ref[v] == 0)
        def _():
            y_ref[...] = jnp.where(mine, y, y_ref[...])


def _experts(meta, xs, wg, bg, wu, bu, wd, bd):
    n4 = xs.shape[0]
    tmg = min(TM_GROUP, n4)
    n_visits = meta[0].shape[0]
    xspec = pl.BlockSpec((tmg, D_MODEL), lambda v, tile, exp, lo, hi, first, nv: (tile[v], 0))
    wspec = pl.BlockSpec((1, D_MODEL, D_MODEL), lambda v, tile, exp, lo, hi, first, nv: (exp[v], 0, 0))
    bspec = pl.BlockSpec((1, 1, D_MODEL), lambda v, tile, exp, lo, hi, first, nv: (exp[v], 0, 0))
    grid_spec = pltpu.PrefetchScalarGridSpec(
        num_scalar_prefetch=6,
        grid=(n_visits,),
        in_specs=[xspec, wspec, bspec, wspec, bspec, wspec, bspec],
        out_specs=xspec,
    )
    return pl.pallas_call(
        functools.partial(_experts_kernel, tmg=tmg),
        grid_spec=grid_spec,
        out_shape=jax.ShapeDtypeStruct((n4, D_MODEL), F32),
        compiler_params=_cparams(("arbitrary",)),
        name="moe_experts",
    )(*meta, xs, wg, bg, wu, bu, wd, bd)


def _combine_kernel(pos_ref, h_ref, route_ref, nfin_ref, ys_hbm, o_ref, buf, sem, *, tm):
    def issue(t, carry):
        for k in range(TOP_K):
            p = pos_ref[t * TOP_K + k]
            pltpu.make_async_copy(ys_hbm.at[pl.ds(p, 1)], buf.at[k, pl.ds(t, 1)], sem).start()
        return carry

    lax.fori_loop(0, tm, issue, 0)

    def drain(t, carry):
        for k in range(TOP_K):
            pltpu.make_async_copy(ys_hbm.at[pl.ds(0, 1)], buf.at[k, pl.ds(0, 1)], sem).wait()
        return carry

    lax.fori_loop(0, tm, drain, 0)

    route = route_ref[...]
    out = h_ref[...]
    for k in range(TOP_K):
        out = out + route[:, 2 * TOP_K + k:2 * TOP_K + k + 1] * buf[k]
    o_ref[...] = out * lax.rsqrt(jnp.mean(out * out, axis=-1, keepdims=True) + NORM_EPS) * nfin_ref[...]


def _combine(pos_flat, h, route, nfin_row, ys, row0, n):
    tm = min(TM_ROWS, n)
    off = row0 // tm
    return pl.pallas_call(
        functools.partial(_combine_kernel, tm=tm),
        grid=(n // tm,),
        in_specs=[pl.BlockSpec((tm * TOP_K,), lambda i: (i + off,), memory_space=pltpu.SMEM),
                  pl.BlockSpec((tm, D_MODEL), lambda i: (i + off, 0)),
                  pl.BlockSpec((tm, LANES), lambda i: (i + off, 0)),
                  pl.BlockSpec((1, D_MODEL), lambda i: (0, 0)),
                  pl.BlockSpec(memory_space=pl.ANY)],
        out_specs=pl.BlockSpec((tm, D_MODEL), lambda i: (i, 0)),
        out_shape=jax.ShapeDtypeStruct((n, D_MODEL), F32),
        scratch_shapes=[pltpu.VMEM((TOP_K, tm, D_MODEL), F32), pltpu.SemaphoreType.DMA(())],
        compiler_params=_cparams(("arbitrary",)),
        name="moe_combine",
    )(pos_flat, h, route, nfin_row, ys)


def _group_plan(counts, n4, tmg):
    n_tiles = n4 // tmg
    n_visits = n_tiles + N_EXPERTS - 1
    ends = jnp.cumsum(counts)
    starts = ends - counts
    t_first = starts // tmg
    t_cnt = jnp.where(counts > 0, (ends - 1) // tmg - t_first + 1, 0)
    v_end = jnp.cumsum(t_cnt)
    v_start = v_end - t_cnt
    total = v_end[-1]
    v = jnp.arange(n_visits, dtype=jnp.int32)
    g = jnp.minimum(jnp.sum((v[:, None] >= v_end[None, :]).astype(jnp.int32), axis=1), N_EXPERTS - 1)
    g_last = jnp.max(jnp.where(counts > 0, jnp.arange(N_EXPERTS), 0)).astype(jnp.int32)
    valid = v < total
    g = jnp.where(valid, g, g_last)
    tile = jnp.where(valid, t_first[g] + (v - v_start[g]), n_tiles - 1).astype(jnp.int32)
    lo = jnp.where(valid, jnp.maximum(starts[g], tile * tmg), 0).astype(jnp.int32)
    hi = jnp.where(valid, jnp.minimum(ends[g], (tile + 1) * tmg), 0).astype(jnp.int32)
    prev_tile = jnp.concatenate([jnp.full((1,), -1, jnp.int32), tile[:-1]])
    first = (tile != prev_tile).astype(jnp.int32)
    return tile, g, lo, hi, first, total.reshape(1).astype(jnp.int32), starts


def _pad_rows(a, rows):
    return jnp.concatenate([a, jnp.zeros((rows - a.shape[0],) + a.shape[1:], a.dtype)], axis=0)


def kernel(x_prompt, x_sample, cache_sb_k, cache_sb_v, cache_gdn_conv, state_gdn, norm_mix, w_in, conv_w, a_log,
           dt_bias, norm_sb, norm_gdn, w_out, norm_ffn, router_w, router_b, w_gate, b_gate, w_up, b_up, w_down,
           b_down, norm_final):
    bp, lp, _ = x_prompt.shape
    bs, ls, _ = x_sample.shape
    past = cache_sb_k.shape[2]
    n_p, n_s = bp * lp, bs * ls
    n_tot = n_p + n_s
    assert norm_mix.shape[0] == 1, "single-layer trunk"
    for n in (n_p, n_s):
        assert n % min(TM_DENSE, n) == 0 and n % min(TM_ROWS, n) == 0
    assert n_p % min(TM_DENSE, n_s) == 0 and n_p % min(TM_ROWS, n_s) == 0
    assert (n_tot * TOP_K) % TM_GROUP == 0 and n_tot % TM_ROWS == 0
    assert lp % min(TQ_SB, lp) == 0 and lp % min(CHUNK, lp) == 0 and ls % min(CHUNK, ls) == 0 and ls >= 8

    w_in_bf = jnp.pad(w_in[0], ((0, 0), (0, IN_WIDTH_PAD - IN_WIDTH))).astype(BF16)
    nmix_row = norm_mix[0].reshape(1, D_MODEL)
    nsb_row = jnp.tile(norm_sb[0], 2).reshape(1, LANES)
    ng_row = norm_gdn[0].reshape(1, LANES)
    cw8 = _pad_rows(conv_w[0], 8)
    alog_row = jnp.pad(a_log[0], (0, LANES - GDN_HEADS)).reshape(1, LANES)
    dtb_row = jnp.pad(dt_bias[0], (0, LANES - GDN_HEADS)).reshape(1, LANES)
    wo_bf = w_out[0].astype(BF16)
    nf_row = norm_ffn[0].reshape(1, D_MODEL)
    rw_pad = jnp.pad(router_w[0], ((0, 0), (0, LANES - N_EXPERTS)))
    rb_row = jnp.pad(router_b[0], (0, LANES - N_EXPERTS)).reshape(1, LANES)
    nfin_row = norm_final.reshape(1, D_MODEL)

    xp2 = x_prompt.reshape(n_p, D_MODEL)
    xs2 = x_sample.reshape(n_s, D_MODEL)

    qsb, ksb, vsb, kbf, vbf, gdn_in, gate, ab = _in_proj(xp2, nmix_row, w_in_bf)
    r3 = lambda a, b, l: a.reshape(b, l, a.shape[-1])
    osb_p = _sb_prompt(r3(qsb, bp, lp), r3(kbf, bp, lp), r3(vbf, bp, lp), nsb_row)
    ogdn_p, conv_p, state_p = _gdn(
        r3(gdn_in, bp, lp), r3(gate, bp, lp), r3(ab, bp, lp),
        jnp.zeros((bp, 8, CONV_CH), F32), jnp.zeros((bp, GDN_HEADS, GDN_HEAD_DIM, GDN_HEAD_DIM), F32),
        cw8, alog_row, dtb_row, ng_row)
    k_prompt, v_prompt = ksb, vsb

    qsb, ksb, vsb, kbf, vbf, gdn_in, gate, ab = _in_proj(xs2, nmix_row, w_in_bf)
    osb_s = _sb_sample(r3(qsb, bs, ls), r3(kbf, bs, ls), r3(vbf, bs, ls),
                       cache_sb_k[0].reshape(bs, past, SB_WIDTH), cache_sb_v[0].reshape(bs, past, SB_WIDTH), nsb_row)
    conv_init = jnp.concatenate([jnp.zeros((bs, 8 - (CONV_WIDTH - 1), CONV_CH), F32), cache_gdn_conv[0]], axis=1)
    ogdn_s, conv_s, state_s = _gdn(r3(gdn_in, bs, ls), r3(gate, bs, ls), r3(ab, bs, ls), conv_init, state_gdn[0],
                                   cw8, alog_row, dtb_row, ng_row)
    k_sample, v_sample = ksb, vsb

    h_buf, xn_buf, route_buf, cnt = _out_route(
        xp2, osb_p.reshape(n_p, SB_WIDTH), ogdn_p.reshape(n_p, GDN_WIDTH),
        xs2, osb_s.reshape(n_s, SB_WIDTH), ogdn_s.reshape(n_s, GDN_WIDTH), wo_bf, nf_row, rw_pad, rb_row)

    counts = cnt[0, :N_EXPERTS].astype(jnp.int32)
    n4 = n_tot * TOP_K
    tmg = min(TM_GROUP, n4)
    tile, grp, lo, hi, first, nvalid, starts = _group_plan(counts, n4, tmg)
    idx = route_buf[:, 0:TOP_K].astype(jnp.int32)
    rank = route_buf[:, TOP_K:2 * TOP_K].astype(jnp.int32)
    pos_flat = (starts[idx] + rank).reshape(n4)

    xs_sorted = _scatter_rows(pos_flat, xn_buf)
    e3 = lambda bias: bias[0].reshape(N_EXPERTS, 1, D_MODEL)
    ys_sorted = _experts((tile, grp, lo, hi, first, nvalid), xs_sorted,
                         w_gate[0].astype(BF16), e3(b_gate), w_up[0].astype(BF16), e3(b_up),
                         w_down[0].astype(BF16), e3(b_down))
    y_prompt = _combine(pos_flat, h_buf, route_buf, nfin_row, ys_sorted, 0, n_p).reshape(bp, lp, D_MODEL)
    y_sample = _combine(pos_flat, h_buf, route_buf, nfin_row, ys_sorted, n_p, n_s).reshape(bs, ls, D_MODEL)

    heads = lambda a, b, l: a.reshape(1, b, l, SB_WIDTH // SB_HEAD_DIM, SB_HEAD_DIM)
    return (y_prompt, y_sample,
            heads(k_prompt, bp, lp), heads(v_prompt, bp, lp),
            conv_p[:, 8 - (CONV_WIDTH - 1):][None], state_p[None],
            heads(k_sample, bs, ls), heads(v_sample, bs, ls),
            conv_s[:, 8 - (CONV_WIDTH - 1):][None], state_s[None])
```

```python
import functools
import math

import jax
import jax.numpy as jnp
from jax import lax
from jax.experimental import pallas as pl
from jax.experimental.pallas import tpu as pltpu

F32 = jnp.float32
BF16 = jnp.bfloat16

D_MODEL = 1024
SB_HEAD_DIM = 64
SB_WIDTH = 512
SB_HEADS = SB_WIDTH // SB_HEAD_DIM
SB_PAIRS = SB_WIDTH // 128
GDN_HEAD_DIM = 128
GDN_HEADS = 4
GDN_WIDTH = 512
CONV_WIDTH = 4
CONV_CH = 3 * GDN_WIDTH
IN_WIDTH = 3 * SB_WIDTH + 4 * GDN_WIDTH + 2 * GDN_HEADS
IN_WIDTH_PAD = 3 * SB_WIDTH + 4 * GDN_WIDTH + 128
N_EXPERTS = 32
TOP_K = 4
SWIGLU_LIMIT = 7.0
SWIGLU_ALPHA = 1.702
NORM_EPS = 1e-6
L2_EPS = 1e-6
CHUNK = 64

LANES = 128
VMEM_LIMIT_BYTES = 48 * 1024 * 1024
VMEM_LIMIT_EXPERTS = 56 * 1024 * 1024
SB_LOG_CUTOFF = 104.0

TM_DENSE = 256
TQ_SB = 256
TK_SB = 128
SB_PAIRS_PER_STEP = 4
SB_CACHE_WINDOW = 512
GDN_BATCH_PER_STEP = 4
TM_ROWS = 256
TM_GROUP = 256


def _cparams(sem, limit=VMEM_LIMIT_BYTES):
    return pltpu.CompilerParams(dimension_semantics=sem, vmem_limit_bytes=limit)


def _softplus(z):
    return jnp.maximum(z, 0.0) + jnp.log(1.0 + jnp.exp(-jnp.abs(z)))


def _sigmoid(z):
    return 1.0 / (1.0 + jnp.exp(-z))


def _store_heads(ref, val, tm):
    for h in range(SB_HEADS):
        ref[pl.ds(h, tm, stride=SB_HEADS), :] = val[:, h * SB_HEAD_DIM:(h + 1) * SB_HEAD_DIM]


def _in_proj_kernel(x_ref, g_ref, w_ref, qsb_ref, ksb_ref, vsb_ref, kbf_ref, vbf_ref, gdn_ref, gate_ref, ab_ref,
                    *, tm):
    x = x_ref[...]
    xn = x * lax.rsqrt(jnp.mean(x * x, axis=-1, keepdims=True) + NORM_EPS) * g_ref[...]
    xn = xn.astype(BF16)

    def mm(lo, hi):
        return jnp.dot(xn, w_ref[:, lo:hi], preferred_element_type=F32)

    qsb_ref[...] = (mm(0, 512) * (SB_HEAD_DIM ** -0.5)).astype(BF16)
    k = mm(512, 1024)
    _store_heads(ksb_ref, k, tm)
    kbf_ref[...] = k.astype(BF16)
    v = mm(1024, 1536)
    _store_heads(vsb_ref, v, tm)
    vbf_ref[...] = v.astype(BF16)
    for j in range(3):
        gdn_ref[:, j * 512:(j + 1) * 512] = mm(1536 + j * 512, 2048 + j * 512)
    gate_ref[...] = mm(3072, 3584)
    ab_ref[...] = mm(3584, 3712)


def _in_proj(x2d, g_row, w_bf):
    n = x2d.shape[0]
    tm = min(TM_DENSE, n)
    row = lambda w: pl.BlockSpec((tm, w), lambda i: (i, 0))
    head_rows = pl.BlockSpec((tm * SB_HEADS, SB_HEAD_DIM), lambda i: (i, 0))
    out_shape = (
        jax.ShapeDtypeStruct((n, SB_WIDTH), BF16),
        jax.ShapeDtypeStruct((n * SB_HEADS, SB_HEAD_DIM), F32),
        jax.ShapeDtypeStruct((n * SB_HEADS, SB_HEAD_DIM), F32),
        jax.ShapeDtypeStruct((n, SB_WIDTH), BF16),
        jax.ShapeDtypeStruct((n, SB_WIDTH), BF16),
        jax.ShapeDtypeStruct((n, CONV_CH), F32),
        jax.ShapeDtypeStruct((n, GDN_WIDTH), F32),
        jax.ShapeDtypeStruct((n, LANES), F32),
    )
    return pl.pallas_call(
        functools.partial(_in_proj_kernel, tm=tm),
        grid=(n // tm,),
        in_specs=[row(D_MODEL), pl.BlockSpec((1, D_MODEL), lambda i: (0, 0)),
                  pl.BlockSpec((D_MODEL, IN_WIDTH_PAD), lambda i: (0, 0))],
        out_specs=(row(SB_WIDTH), head_rows, head_rows, row(SB_WIDTH), row(SB_WIDTH),
                   row(CONV_CH), row(GDN_WIDTH), row(LANES)),
        out_shape=out_shape,
        compiler_params=_cparams(("arbitrary",)),
        name="in_proj",
    )(x2d, g_row, w_bf)


def _sb_step(qs, k_blks, v_blks, qpos, kpos, acc_ref, c_ref):
    pairs = len(qs)
    chains = [(p, h) for p in range(pairs) for h in range(2)]
    tk = k_blks[0].shape[0]
    lane = lax.broadcasted_iota(jnp.int32, (1, LANES), 1)
    visible = kpos < qpos
    jj = lax.broadcasted_iota(jnp.int32, (tk, tk), 0)
    ss = lax.broadcasted_iota(jnp.int32, (tk, tk), 1)
    tri = jnp.where(jj >= ss, 1.0, 0.0).astype(BF16)
    nt = (((1,), (1,)), ((), ()))

    z = {ch: lax.dot_general(qs[ch[0]][ch[1]], k_blks[ch[0]], nt, preferred_element_type=F32) for ch in chains}
    sp = {ch: jnp.where(visible, _softplus(z[ch]), 0.0) for ch in chains}
    sp_hi = {ch: sp[ch].astype(BF16) for ch in chains}
    sp_lo = {ch: (sp[ch] - sp_hi[ch].astype(F32)).astype(BF16) for ch in chains}
    r = {ch: jnp.dot(sp_hi[ch], tri, preferred_element_type=F32) + jnp.dot(sp_lo[ch], tri, preferred_element_type=F32)
         for ch in chains}
    w = {ch: jnp.where(visible, jnp.exp(z[ch] - r[ch] - c_ref[2 * ch[0] + ch[1]][:, :tk]), 0.0).astype(BF16)
         for ch in chains}
    pv = {ch: jnp.dot(w[ch], v_blks[ch[0]], preferred_element_type=F32) for ch in chains}
    cmin = None
    for p in range(pairs):
        acc_ref[p] += jnp.where(lane < SB_HEAD_DIM, pv[(p, 0)], pv[(p, 1)])
    for ch in chains:
        c_new = c_ref[2 * ch[0] + ch[1]] + r[ch][:, 0:1]
        c_ref[2 * ch[0] + ch[1]] = c_new
        m = jnp.min(c_new)
        cmin = m if cmin is None else jnp.minimum(cmin, m)
    return cmin


def _sb_finish(acc, nsb_row):
    lane = lax.broadcasted_iota(jnp.int32, (1, LANES), 1)
    first = lane < SB_HEAD_DIM
    sq = acc * acc
    s_all = jnp.sum(sq, axis=-1, keepdims=True)
    s0 = jnp.sum(jnp.where(first, sq, 0.0), axis=-1, keepdims=True)
    ms = jnp.where(first, s0, s_all - s0) * (1.0 / SB_HEAD_DIM)
    return acc * lax.rsqrt(ms + NORM_EPS) * nsb_row


def _split_heads(q):
    lane = lax.broadcasted_iota(jnp.int32, (1, LANES), 1)
    zero = jnp.zeros_like(q)
    return jnp.where(lane < SB_HEAD_DIM, q, zero), jnp.where(lane >= SB_HEAD_DIM, q, zero)


def _sb_continue(carry):
    j, cmin = carry
    return jnp.logical_and(j >= 0, cmin < SB_LOG_CUTOFF)


def _sb_prompt_kernel(q_ref, k_ref, v_ref, nsb_ref, o_ref, acc_ref, c_ref, *, tq, tk, pairs):
    i = pl.program_id(2)
    lanes = lambda p: slice(p * LANES, (p + 1) * LANES)
    qs = [_split_heads(q_ref[0, :, lanes(p)]) for p in range(pairs)]
    acc_ref[...] = jnp.zeros_like(acc_ref)
    c_ref[...] = jnp.zeros_like(c_ref)
    qpos = i * tq + lax.broadcasted_iota(jnp.int32, (tq, 1), 0)

    def body(carry):
        j, _ = carry
        ks = pl.multiple_of(j * tk, tk)
        kpos = ks + lax.broadcasted_iota(jnp.int32, (1, tk), 1)
        k_blks = [k_ref[0, pl.ds(ks, tk), lanes(p)] for p in range(pairs)]
        v_blks = [v_ref[0, pl.ds(ks, tk), lanes(p)] for p in range(pairs)]
        return j - 1, _sb_step(qs, k_blks, v_blks, qpos, kpos, acc_ref, c_ref)

    lax.while_loop(_sb_continue, body, ((i + 1) * (tq // tk) - 1, jnp.float32(0.0)))
    for p in range(pairs):
        o_ref[0, :, lanes(p)] = _sb_finish(acc_ref[p], nsb_ref[...]).astype(o_ref.dtype)


def _sb_prompt(q, k, v, nsb_row):
    b, l, _ = q.shape
    tq = min(TQ_SB, l)
    tk = min(TK_SB, tq)
    pairs = SB_PAIRS_PER_STEP
    width = pairs * LANES
    qspec = pl.BlockSpec((1, tq, width), lambda bb, hp, i: (bb, i, hp))
    kvspec = pl.BlockSpec((1, l, width), lambda bb, hp, i: (bb, 0, hp))
    return pl.pallas_call(
        functools.partial(_sb_prompt_kernel, tq=tq, tk=tk, pairs=pairs),
        grid=(b, SB_PAIRS // pairs, l // tq),
        in_specs=[qspec, kvspec, kvspec, pl.BlockSpec((1, LANES), lambda bb, hp, i: (0, 0))],
        out_specs=qspec,
        out_shape=jax.ShapeDtypeStruct((b, l, SB_WIDTH), BF16),
        scratch_shapes=[pltpu.VMEM((pairs, tq, LANES), F32), pltpu.VMEM((2 * pairs, tq, LANES), F32)],
        compiler_params=_cparams(("arbitrary", "arbitrary", "arbitrary")),
        name="sb_prompt",
    )(q, k, v, nsb_row)


def _sb_sample_kernel(q_ref, kn_ref, vn_ref, kw_ref, vw_ref, kc_hbm, vc_hbm, nsb_ref, o_ref,
                      acc_ref, c_ref, kbuf, vbuf, sem, *, past, window, tk):
    b = pl.program_id(0)
    hp = pl.program_id(1)
    t = q_ref.shape[1]
    qs = [_split_heads(q_ref[0])]
    acc_ref[...] = jnp.zeros_like(acc_ref)
    c_ref[...] = jnp.zeros_like(c_ref)
    qpos = past + lax.broadcasted_iota(jnp.int32, (t, 1), 0)

    kpos_new = past + lax.broadcasted_iota(jnp.int32, (1, t), 1)
    cmin0 = _sb_step(qs, [kn_ref[0]], [vn_ref[0]], qpos, kpos_new, acc_ref, c_ref)

    def wbody(carry):
        j, _ = carry
        ws = pl.multiple_of(j * tk, tk)
        kpos = (past - window) + ws + lax.broadcasted_iota(jnp.int32, (1, tk), 1)
        cmin = _sb_step(qs, [kw_ref[0, pl.ds(ws, tk), :].astype(BF16)],
                        [vw_ref[0, pl.ds(ws, tk), :].astype(BF16)], qpos, kpos, acc_ref, c_ref)
        return j - 1, cmin

    _, cmin1 = lax.while_loop(_sb_continue, wbody, (window // tk - 1, cmin0))

    n_old = (past - window) // tk
    if n_old > 0:
        rows_per_blk = tk * SB_HEADS

        def pair_of(buf):
            h0 = buf[pl.ds(2 * hp, tk, stride=SB_HEADS), :]
            h1 = buf[pl.ds(2 * hp + 1, tk, stride=SB_HEADS), :]
            return jnp.concatenate([h0, h1], axis=-1).astype(BF16)

        def obody(carry):
            j, _ = carry
            rows = pl.ds(pl.multiple_of(j * rows_per_blk, rows_per_blk), rows_per_blk)
            ck = pltpu.make_async_copy(kc_hbm.at[b, rows, :], kbuf, sem.at[0])
            cv = pltpu.make_async_copy(vc_hbm.at[b, rows, :], vbuf, sem.at[1])
            ck.start()
            cv.start()
            ck.wait()
            cv.wait()
            kpos = j * tk + lax.broadcasted_iota(jnp.int32, (1, tk), 1)
            cmin = _sb_step(qs, [pair_of(kbuf)], [pair_of(vbuf)], qpos, kpos, acc_ref, c_ref)
            return j - 1, cmin

        lax.while_loop(_sb_continue, obody, (n_old - 1, cmin1))

    o_ref[0] = _sb_finish(acc_ref[0], nsb_ref[...]).astype(o_ref.dtype)


def _sb_sample(q, k_new, v_new, k_win, v_win, k_cache, v_cache, nsb_row):
    b, t, _ = q.shape
    past = k_cache.shape[1] // SB_HEADS
    window = k_win.shape[1]
    tk = min(TK_SB, window)
    assert window % tk == 0 and (past - window) % tk == 0
    new_spec = pl.BlockSpec((1, t, LANES), lambda bb, hp: (bb, 0, hp))
    win_spec = pl.BlockSpec((1, window, LANES), lambda bb, hp: (bb, 0, hp))
    any_spec = pl.BlockSpec(memory_space=pl.ANY)
    return pl.pallas_call(
        functools.partial(_sb_sample_kernel, past=past, window=window, tk=tk),
        grid=(b, SB_PAIRS),
        in_specs=[new_spec, new_spec, new_spec, win_spec, win_spec, any_spec, any_spec,
                  pl.BlockSpec((1, LANES), lambda bb, hp: (0, 0))],
        out_specs=new_spec,
        out_shape=jax.ShapeDtypeStruct((b, t, SB_WIDTH), BF16),
        scratch_shapes=[pltpu.VMEM((1, t, LANES), F32), pltpu.VMEM((2, t, LANES), F32),
                        pltpu.VMEM((tk * SB_HEADS, SB_HEAD_DIM), F32), pltpu.VMEM((tk * SB_HEADS, SB_HEAD_DIM), F32),
                        pltpu.SemaphoreType.DMA((2,))],
        compiler_params=_cparams(("arbitrary", "arbitrary")),
        name="sb_sample",
    )(q, k_new, v_new, k_win, v_win, k_cache, v_cache, nsb_row)


def _gdn_chunk(nb, qkv_ref, gate_ref, ab_ref, cw_ref, alog_ref, dtb_ref, ng_ref, o_ref, xp_ref, s_ref, chunk):
    ii = lax.broadcasted_iota(jnp.int32, (chunk, chunk), 0)
    jj = lax.broadcasted_iota(jnp.int32, (chunk, chunk), 1)
    tri = jnp.where(ii >= jj, 1.0, 0.0).astype(F32)
    scale = GDN_HEAD_DIM ** -0.5
    hd = GDN_HEAD_DIM
    n_fac = int(math.log2(chunk)) - 1
    chains = [(bi, h) for bi in range(nb) for h in range(GDN_HEADS)]

    ys, gcs, gcts, betas = [], [], [], []
    for bi in range(nb):
        xp_ref[bi, 8:8 + chunk, :] = qkv_ref[bi]
        y = cw_ref[0:1, :] * xp_ref[bi, 5:5 + chunk, :]
        for i in range(1, CONV_WIDTH):
            y = y + cw_ref[i:i + 1, :] * xp_ref[bi, 5 + i:5 + i + chunk, :]
        ys.append(y * _sigmoid(y))
        xp_ref[bi, 0:8, :] = xp_ref[bi, chunk:chunk + 8, :]
        ab = ab_ref[bi]
        g_tile = -jnp.exp(alog_ref[...]) * _softplus(ab + dtb_ref[...])
        betas.append(_sigmoid(ab))
        gc_tile = jnp.dot(tri, g_tile, preferred_element_type=F32, precision=lax.Precision.HIGHEST)
        gcs.append(gc_tile)
        gcts.append(gc_tile.T)

    kn, qs, kb, vb, decay, egc, glast, gcol = {}, {}, {}, {}, {}, {}, {}, {}
    for ch in chains:
        bi, h = ch
        y = ys[bi]
        qh = y[:, h * hd:(h + 1) * hd]
        kh = y[:, GDN_WIDTH + h * hd:GDN_WIDTH + (h + 1) * hd]
        vh = y[:, 2 * GDN_WIDTH + h * hd:2 * GDN_WIDTH + (h + 1) * hd]
        qs[ch] = qh * lax.rsqrt(jnp.sum(qh * qh, axis=-1, keepdims=True) + L2_EPS) * scale
        kn[ch] = kh * lax.rsqrt(jnp.sum(kh * kh, axis=-1, keepdims=True) + L2_EPS)
        gcol[ch] = gcs[bi][:, h:h + 1]
        grow = gcts[bi][h:h + 1, :]
        glast[ch] = gcs[bi][chunk - 1:chunk, h:h + 1]
        bcol = betas[bi][:, GDN_HEADS + h:GDN_HEADS + h + 1]
        decay[ch] = jnp.exp(jnp.where(ii >= jj, gcol[ch] - grow, -1e30))
        egc[ch] = jnp.exp(gcol[ch])
        kb[ch] = kn[ch] * bcol
        vb[ch] = vh * bcol

    a2 = {ch: lax.dot_general(jnp.concatenate([kb[ch], qs[ch]], axis=0).astype(BF16), kn[ch].astype(BF16),
                              (((1,), (1,)), ((), ())), preferred_element_type=F32) for ch in chains}
    lb = {ch: jnp.where(ii > jj, a2[ch][:chunk] * decay[ch], 0.0).astype(BF16) for ch in chains}
    qk = {ch: (a2[ch][chunk:] * decay[ch]).astype(BF16) for ch in chains}

    rhs = {ch: jnp.concatenate([vb[ch], kb[ch] * egc[ch]], axis=1) for ch in chains}
    sol = {ch: rhs[ch] - jnp.dot(lb[ch], rhs[ch].astype(BF16), preferred_element_type=F32) for ch in chains}
    pw = {ch: jnp.dot(lb[ch], lb[ch], preferred_element_type=F32).astype(BF16) for ch in chains}
    for f in range(n_fac):
        sol = {ch: sol[ch] + jnp.dot(pw[ch], sol[ch].astype(BF16), preferred_element_type=F32) for ch in chains}
        if f + 1 < n_fac:
            pw = {ch: jnp.dot(pw[ch], pw[ch], preferred_element_type=F32).astype(BF16) for ch in chains}

    s_old = {ch: s_ref[ch[0], ch[1]] for ch in chains}
    ws = {ch: jnp.dot(jnp.concatenate([sol[ch][:, hd:], qs[ch] * egc[ch]], axis=0).astype(BF16),
                      s_old[ch].astype(BF16), preferred_element_type=F32) for ch in chains}
    v_new = {ch: (sol[ch][:, :hd] - ws[ch][:chunk]).astype(BF16) for ch in chains}
    o = {ch: ws[ch][chunk:] + jnp.dot(qk[ch], v_new[ch], preferred_element_type=F32) for ch in chains}
    for ch in chains:
        kd = (kn[ch] * jnp.exp(glast[ch] - gcol[ch])).astype(BF16)
        s_ref[ch[0], ch[1]] = s_old[ch] * jnp.exp(glast[ch]) + lax.dot_general(
            kd, v_new[ch], (((0,), (0,)), ((), ())), preferred_element_type=F32)

    for ch in chains:
        bi, h = ch
        sl = slice(h * hd, (h + 1) * hd)
        on = o[ch] * lax.rsqrt(jnp.mean(o[ch] * o[ch], axis=-1, keepdims=True) + NORM_EPS) * ng_ref[...]
        gt = gate_ref[bi][:, sl]
        o_ref[bi, :, sl] = (on * (gt * _sigmoid(gt))).astype(o_ref.dtype)


def _gdn_kernel(qkv_ref, gate_ref, ab_ref, cinit_ref, sinit_ref, cw_ref, alog_ref, dtb_ref, ng_ref,
                o_ref, cout_ref, sout_ref, xp_ref, s_ref, *, chunk, nb):
    c = pl.program_id(1)

    @pl.when(c == 0)
    def _():
        xp_ref[:, 0:8, :] = cinit_ref[...]
        s_ref[...] = sinit_ref[...]

    _gdn_chunk(nb, qkv_ref, gate_ref, ab_ref, cw_ref, alog_ref, dtb_ref, ng_ref, o_ref, xp_ref, s_ref, chunk)

    @pl.when(c == pl.num_programs(1) - 1)
    def _():
        cout_ref[...] = xp_ref[:, 0:8, :]
        sout_ref[...] = s_ref[...]


def _gdn(qkv, gate, ab, conv_init, s_init, cw8, alog_row, dtb_row, ng_row):
    b, l, _ = qkv.shape
    chunk = min(CHUNK, l)
    nc = l // chunk
    nb = min(GDN_BATCH_PER_STEP, b)
    assert b % nb == 0
    tok = lambda w: pl.BlockSpec((nb, chunk, w), lambda bb, c: (bb, c, 0))
    const = lambda shape: pl.BlockSpec(shape, lambda bb, c: (0,) * len(shape))
    per_b3 = pl.BlockSpec((nb, 8, CONV_CH), lambda bb, c: (bb, 0, 0))
    per_b4 = pl.BlockSpec((nb, GDN_HEADS, GDN_HEAD_DIM, GDN_HEAD_DIM), lambda bb, c: (bb, 0, 0, 0))
    return pl.pallas_call(
        functools.partial(_gdn_kernel, chunk=chunk, nb=nb),
        grid=(b // nb, nc),
        in_specs=[tok(CONV_CH), tok(GDN_WIDTH), tok(LANES), per_b3, per_b4,
                  const((8, CONV_CH)), const((1, LANES)), const((1, LANES)), const((1, LANES))],
        out_specs=(tok(GDN_WIDTH), per_b3, per_b4),
        out_shape=(jax.ShapeDtypeStruct((b, l, GDN_WIDTH), BF16),
                   jax.ShapeDtypeStruct((b, 8, CONV_CH), F32),
                   jax.ShapeDtypeStruct((b, GDN_HEADS, GDN_HEAD_DIM, GDN_HEAD_DIM), F32)),
        scratch_shapes=[pltpu.VMEM((nb, chunk + 8, CONV_CH), F32),
                        pltpu.VMEM((nb, GDN_HEADS, GDN_HEAD_DIM, GDN_HEAD_DIM), F32)],
        compiler_params=_cparams(("arbitrary", "arbitrary")),
        name="gdn",
    )(qkv, gate, ab, conv_init, s_init, cw8, alog_row, dtb_row, ng_row)


def _out_route_kernel(xp_ref, osbp_ref, ogdnp_ref, xs_ref, osbs_ref, ogdns_ref, wo_ref, nf_ref, rw_ref, rb_ref,
                      h_ref, xn_ref, route_ref, cnt_ref, run_ref, *, tm, steps_p):
    i = pl.program_id(0)

    @pl.when(i == 0)
    def _():
        run_ref[...] = jnp.zeros_like(run_ref)

    is_p = i < steps_p
    x = jnp.where(is_p, xp_ref[...], xs_ref[...])
    osb = jnp.where(is_p, osbp_ref[...], osbs_ref[...])
    ogdn = jnp.where(is_p, ogdnp_ref[...], ogdns_ref[...])
    h = (x
         + jnp.dot(osb, wo_ref[0:SB_WIDTH, :], preferred_element_type=F32)
         + jnp.dot(ogdn, wo_ref[SB_WIDTH:, :], preferred_element_type=F32))
    h_ref[...] = h
    xn = h * lax.rsqrt(jnp.mean(h * h, axis=-1, keepdims=True) + NORM_EPS) * nf_ref[...]
    xn_ref[...] = xn

    lane = lax.broadcasted_iota(jnp.int32, (1, LANES), 1)
    lane_f = lane.astype(F32)
    logits = jnp.dot(xn, rw_ref[...], preferred_element_type=F32, precision=lax.Precision.HIGHEST) + rb_ref[...]
    logits = jnp.where(lane < N_EXPERTS, logits, -jnp.inf)
    vals, hots = [], []
    for _ in range(TOP_K):
        m = jnp.max(logits, axis=-1, keepdims=True)
        idx = jnp.min(jnp.where(logits == m, lane_f, float(LANES)), axis=-1, keepdims=True)
        hot = lane_f == idx
        logits = jnp.where(hot, -jnp.inf, logits)
        vals.append(m)
        hots.append((hot, idx))
    exps = [jnp.exp(v - vals[0]) for v in vals]
    denom = exps[0] + exps[1] + exps[2] + exps[3]

    multi = jnp.zeros((tm, LANES), F32)
    for hot, _ in hots:
        multi = jnp.where(hot, 1.0, multi)
    ii = lax.broadcasted_iota(jnp.int32, (tm, tm), 0)
    jj = lax.broadcasted_iota(jnp.int32, (tm, tm), 1)
    earlier = jnp.where(ii > jj, 1.0, 0.0).astype(BF16)
    run = run_ref[0:1, :]
    rank_all = jnp.dot(earlier, multi.astype(BF16), preferred_element_type=F32) + run
    run_new = run + jnp.sum(multi, axis=0, keepdims=True)
    run_ref[...] = jnp.broadcast_to(run_new, run_ref.shape)
    cnt_ref[...] = jnp.broadcast_to(run_new, cnt_ref.shape)

    route = jnp.zeros((tm, LANES), F32)
    for k, (hot, idx) in enumerate(hots):
        rank = jnp.sum(jnp.where(hot, rank_all, 0.0), axis=-1, keepdims=True)
        route = jnp.where(lane == k, idx, route)
        route = jnp.where(lane == TOP_K + k, rank, route)
        route = jnp.where(lane == 2 * TOP_K + k, exps[k] / denom, route)
    route_ref[...] = route


def _out_route(xp, osbp, ogdnp, xs, osbs, ogdns, wo_bf, nf_row, rw_pad, rb_row):
    n_p, n_s = xp.shape[0], xs.shape[0]
    tm = min(TM_DENSE, n_p, n_s)
    steps_p, steps_s = n_p // tm, n_s // tm
    prow = lambda w: pl.BlockSpec((tm, w), lambda i: (jnp.minimum(i, steps_p - 1), 0))
    srow = lambda w: pl.BlockSpec((tm, w), lambda i: (jnp.maximum(i - steps_p, 0), 0))
    orow = lambda w: pl.BlockSpec((tm, w), lambda i: (i, 0))
    const = lambda shape: pl.BlockSpec(shape, lambda i: (0,) * len(shape))
    n_total = n_p + n_s
    out_shape = (jax.ShapeDtypeStruct((n_total, D_MODEL), F32),
                 jax.ShapeDtypeStruct((n_total, D_MODEL), F32),
                 jax.ShapeDtypeStruct((n_total, LANES), F32),
                 jax.ShapeDtypeStruct((8, LANES), F32))
    return pl.pallas_call(
        functools.partial(_out_route_kernel, tm=tm, steps_p=steps_p),
        grid=(steps_p + steps_s,),
        in_specs=[prow(D_MODEL), prow(SB_WIDTH), prow(GDN_WIDTH), srow(D_MODEL), srow(SB_WIDTH), srow(GDN_WIDTH),
                  const((D_MODEL, D_MODEL)), const((1, D_MODEL)), const((D_MODEL, LANES)), const((1, LANES))],
        out_specs=(orow(D_MODEL), orow(D_MODEL), orow(LANES), const((8, LANES))),
        out_shape=out_shape,
        scratch_shapes=[pltpu.VMEM((8, LANES), F32)],
        compiler_params=_cparams(("arbitrary",)),
        name="out_route",
    )(xp, osbp, ogdnp, xs, osbs, ogdns, wo_bf, nf_row, rw_pad, rb_row)


def _scatter_rows_kernel(pos_ref, x_ref, xs_hbm, sem, *, tm):
    def row_copy(t, p):
        return pltpu.make_async_copy(x_ref.at[pl.ds(t, 1)], xs_hbm.at[pl.ds(p, 1)], sem)

    def issue(t, carry):
        for k in range(TOP_K):
            row_copy(t, pos_ref[t * TOP_K + k]).start(priority=k % 2)
        return carry

    lax.fori_loop(0, tm, issue, 0)

    def drain(t, carry):
        for k in range(TOP_K):
            row_copy(0, 0).wait()
        return carry

    lax.fori_loop(0, tm, drain, 0)


def _scatter_rows(pos_flat, xn):
    n = xn.shape[0]
    tm = min(TM_ROWS, n)
    return pl.pallas_call(
        functools.partial(_scatter_rows_kernel, tm=tm),
        grid=(n // tm,),
        in_specs=[pl.BlockSpec((tm * TOP_K,), lambda i: (i,), memory_space=pltpu.SMEM),
                  pl.BlockSpec((tm, D_MODEL), lambda i: (i, 0))],
        out_specs=pl.BlockSpec(memory_space=pl.ANY),
        out_shape=jax.ShapeDtypeStruct((n * TOP_K, D_MODEL), xn.dtype),
        scratch_shapes=[pltpu.SemaphoreType.DMA(())],
        compiler_params=_cparams(("arbitrary",)),
        name="moe_scatter",
    )(pos_flat, xn)


def _experts_kernel(tile_ref, exp_ref, lo_ref, hi_ref, first_ref, newexp_ref, nvalid_ref,
                    x_ref, wg_ref, bg_ref, wu_ref, bu_ref, wd_ref, bd_ref, y_ref, wg_bf, wu_bf, wd_bf, *, tmg):
    v = pl.program_id(0)

    @pl.when(jnp.logical_and(v < nvalid_ref[0], newexp_ref[v] == 1))
    def _():
        wg_bf[...] = wg_ref[0].astype(BF16)
        wu_bf[...] = wu_ref[0].astype(BF16)
        wd_bf[...] = wd_ref[0].astype(BF16)

    @pl.when(v < nvalid_ref[0])
    def _():
        x = x_ref[...].astype(BF16)
        gt = jnp.minimum(jnp.dot(x, wg_bf[...], preferred_element_type=F32) + bg_ref[0], SWIGLU_LIMIT)
        up = jnp.clip(jnp.dot(x, wu_bf[...], preferred_element_type=F32) + bu_ref[0], -SWIGLU_LIMIT, SWIGLU_LIMIT)
        hid = (up + 1.0) * (gt * _sigmoid(SWIGLU_ALPHA * gt))
        y = jnp.dot(hid.astype(BF16), wd_bf[...], preferred_element_type=F32) + bd_ref[0]
        rows = tile_ref[v] * tmg + lax.broadcasted_iota(jnp.int32, (tmg, 1), 0)
        mine = jnp.logical_and(rows >= lo_ref[v], rows < hi_ref[v])

        @pl.when(first_ref[v] == 1)
        def _():
            y_ref[...] = jnp.where(mine, y, 0.0)

        @pl.when(first_ref[v] == 0)
        def _():
            y_ref[...] = jnp.where(mine, y, y_ref[...])


def _experts(meta, xs, wg, bg, wu, bu, wd, bd):
    n4 = xs.shape[0]
    tmg = min(TM_GROUP, n4)
    n_visits = meta[0].shape[0]
    xspec = pl.BlockSpec((tmg, D_MODEL), lambda v, tile, exp, *_: (tile[v], 0))
    wspec = pl.BlockSpec((1, D_MODEL, D_MODEL), lambda v, tile, exp, *_: (exp[v], 0, 0))
    bspec = pl.BlockSpec((1, 1, D_MODEL), lambda v, tile, exp, *_: (exp[v], 0, 0))
    grid_spec = pltpu.PrefetchScalarGridSpec(
        num_scalar_prefetch=len(meta),
        grid=(n_visits,),
        in_specs=[xspec, wspec, bspec, wspec, bspec, wspec, bspec],
        out_specs=xspec,
        scratch_shapes=[pltpu.VMEM((D_MODEL, D_MODEL), BF16)] * 3,
    )
    return pl.pallas_call(
        functools.partial(_experts_kernel, tmg=tmg),
        grid_spec=grid_spec,
        out_shape=jax.ShapeDtypeStruct((n4, D_MODEL), F32),
        compiler_params=_cparams(("arbitrary",), VMEM_LIMIT_EXPERTS),
        name="moe_experts",
    )(*meta, xs, wg, bg, wu, bu, wd, bd)


def _combine_kernel(pos_ref, h_ref, route_ref, nfin_ref, ys_hbm, o_ref, buf, sem, *, tm):
    def row_copy(t, k, p):
        return pltpu.make_async_copy(ys_hbm.at[pl.ds(p, 1)], buf.at[k, pl.ds(t, 1)], sem)

    def issue(t, carry):
        for k in range(TOP_K):
            row_copy(t, k, pos_ref[t * TOP_K + k]).start(priority=k % 2)
        return carry

    lax.fori_loop(0, tm, issue, 0)

    def drain(t, carry):
        for k in range(TOP_K):
            row_copy(0, k, 0).wait()
        return carry

    lax.fori_loop(0, tm, drain, 0)

    route = route_ref[...]
    out = h_ref[...]
    for k in range(TOP_K):
        out = out + route[:, 2 * TOP_K + k:2 * TOP_K + k + 1] * buf[k]
    o_ref[...] = out * lax.rsqrt(jnp.mean(out * out, axis=-1, keepdims=True) + NORM_EPS) * nfin_ref[...]


def _combine(pos_flat, h, route, nfin_row, ys, row0, n):
    tm = min(TM_ROWS, n)
    off = row0 // tm
    return pl.pallas_call(
        functools.partial(_combine_kernel, tm=tm),
        grid=(n // tm,),
        in_specs=[pl.BlockSpec((tm * TOP_K,), lambda i: (i + off,), memory_space=pltpu.SMEM),
                  pl.BlockSpec((tm, D_MODEL), lambda i: (i + off, 0)),
                  pl.BlockSpec((tm, LANES), lambda i: (i + off, 0)),
                  pl.BlockSpec((1, D_MODEL), lambda i: (0, 0)),
                  pl.BlockSpec(memory_space=pl.ANY)],
        out_specs=pl.BlockSpec((tm, D_MODEL), lambda i: (i, 0)),
        out_shape=jax.ShapeDtypeStruct((n, D_MODEL), F32),
        scratch_shapes=[pltpu.VMEM((TOP_K, tm, D_MODEL), F32), pltpu.SemaphoreType.DMA(())],
        compiler_params=_cparams(("arbitrary",)),
        name="moe_combine",
    )(pos_flat, h, route, nfin_row, ys)


def _group_plan(counts, n4, tmg):
    n_tiles = n4 // tmg
    n_visits = n_tiles + N_EXPERTS - 1
    ends = jnp.cumsum(counts)
    starts = ends - counts
    t_first = starts // tmg
    t_cnt = jnp.where(counts > 0, (ends - 1) // tmg - t_first + 1, 0)
    v_end = jnp.cumsum(t_cnt)
    v_start = v_end - t_cnt
    total = v_end[-1]
    v = jnp.arange(n_visits, dtype=jnp.int32)
    g = jnp.minimum(jnp.sum((v[:, None] >= v_end[None, :]).astype(jnp.int32), axis=1), N_EXPERTS - 1)
    g_last = jnp.max(jnp.where(counts > 0, jnp.arange(N_EXPERTS), 0)).astype(jnp.int32)
    valid = v < total
    g = jnp.where(valid, g, g_last)
    tile = jnp.where(valid, t_first[g] + (v - v_start[g]), n_tiles - 1).astype(jnp.int32)
    lo = jnp.where(valid, jnp.maximum(starts[g], tile * tmg), 0).astype(jnp.int32)
    hi = jnp.where(valid, jnp.minimum(ends[g], (tile + 1) * tmg), 0).astype(jnp.int32)
    prev_tile = jnp.concatenate([jnp.full((1,), -1, jnp.int32), tile[:-1]])
    first = (tile != prev_tile).astype(jnp.int32)
    prev_g = jnp.concatenate([jnp.full((1,), -1, jnp.int32), g[:-1]])
    newexp = (g != prev_g).astype(jnp.int32)
    meta = (tile, g, lo, hi, first, newexp, total.reshape(1).astype(jnp.int32))
    return meta, starts


def _pad_rows(a, rows):
    return jnp.concatenate([a, jnp.zeros((rows - a.shape[0],) + a.shape[1:], a.dtype)], axis=0)


def kernel(x_prompt, x_sample, cache_sb_k, cache_sb_v, cache_gdn_conv, state_gdn, norm_mix, w_in, conv_w, a_log,
           dt_bias, norm_sb, norm_gdn, w_out, norm_ffn, router_w, router_b, w_gate, b_gate, w_up, b_up, w_down,
           b_down, norm_final):
    bp, lp, _ = x_prompt.shape
    bs, ls, _ = x_sample.shape
    past = cache_sb_k.shape[2]
    n_p, n_s = bp * lp, bs * ls
    n_tot = n_p + n_s
    assert norm_mix.shape[0] == 1, "single-layer trunk"
    tm_min = min(TM_DENSE, TM_ROWS, n_p, n_s)
    assert n_p % tm_min == 0 and n_s % tm_min == 0 and n_p % min(TM_ROWS, n_s) == 0
    assert (n_tot * TOP_K) % TM_GROUP == 0 and n_tot % TM_ROWS == 0
    assert lp % min(TQ_SB, lp) == 0 and lp % min(CHUNK, lp) == 0 and ls % min(CHUNK, ls) == 0 and ls >= 8
    window = min(SB_CACHE_WINDOW, past)

    w_in_bf = jnp.pad(w_in[0], ((0, 0), (0, IN_WIDTH_PAD - IN_WIDTH))).astype(BF16)
    nmix_row = norm_mix[0].reshape(1, D_MODEL)
    nsb_row = jnp.tile(norm_sb[0], 2).reshape(1, LANES)
    ng_row = norm_gdn[0].reshape(1, LANES)
    cw8 = _pad_rows(conv_w[0], 8)
    alog_row = jnp.pad(a_log[0], (0, LANES - GDN_HEADS)).reshape(1, LANES)
    dtb_row = jnp.pad(dt_bias[0], (0, LANES - GDN_HEADS)).reshape(1, LANES)
    wo_bf = w_out[0].astype(BF16)
    nf_row = norm_ffn[0].reshape(1, D_MODEL)
    rw_pad = jnp.pad(router_w[0], ((0, 0), (0, LANES - N_EXPERTS)))
    rb_row = jnp.pad(router_b[0], (0, LANES - N_EXPERTS)).reshape(1, LANES)
    nfin_row = norm_final.reshape(1, D_MODEL)

    xp2 = x_prompt.reshape(n_p, D_MODEL)
    xs2 = x_sample.reshape(n_s, D_MODEL)

    qsb, k_prompt, v_prompt, kbf, vbf, gdn_in, gate, ab = _in_proj(xp2, nmix_row, w_in_bf)
    r3 = lambda a, b, l: a.reshape(b, l, a.shape[-1])
    osb_p = _sb_prompt(r3(qsb, bp, lp), r3(kbf, bp, lp), r3(vbf, bp, lp), nsb_row)
    ogdn_p, conv_p, state_p = _gdn(
        r3(gdn_in, bp, lp), r3(gate, bp, lp), r3(ab, bp, lp),
        jnp.zeros((bp, 8, CONV_CH), F32), jnp.zeros((bp, GDN_HEADS, GDN_HEAD_DIM, GDN_HEAD_DIM), F32),
        cw8, alog_row, dtb_row, ng_row)

    qsb, k_sample, v_sample, kbf, vbf, gdn_in, gate, ab = _in_proj(xs2, nmix_row, w_in_bf)
    newest = lambda cache: cache[0, :, past - window:].reshape(bs, window, SB_WIDTH)
    whole = lambda cache: cache[0].reshape(bs, past * SB_HEADS, SB_HEAD_DIM)
    osb_s = _sb_sample(r3(qsb, bs, ls), r3(kbf, bs, ls), r3(vbf, bs, ls), newest(cache_sb_k), newest(cache_sb_v),
                       whole(cache_sb_k), whole(cache_sb_v), nsb_row)
    conv_init = jnp.concatenate([jnp.zeros((bs, 8 - (CONV_WIDTH - 1), CONV_CH), F32), cache_gdn_conv[0]], axis=1)
    ogdn_s, conv_s, state_s = _gdn(r3(gdn_in, bs, ls), r3(gate, bs, ls), r3(ab, bs, ls), conv_init, state_gdn[0],
                                   cw8, alog_row, dtb_row, ng_row)

    h_buf, xn_buf, route_buf, cnt = _out_route(
        xp2, osb_p.reshape(n_p, SB_WIDTH), ogdn_p.reshape(n_p, GDN_WIDTH),
        xs2, osb_s.reshape(n_s, SB_WIDTH), ogdn_s.reshape(n_s, GDN_WIDTH), wo_bf, nf_row, rw_pad, rb_row)

    counts = cnt[0, :N_EXPERTS].astype(jnp.int32)
    n4 = n_tot * TOP_K
    tmg = min(TM_GROUP, n4)
    meta, starts = _group_plan(counts, n4, tmg)
    idx = route_buf[:, 0:TOP_K].astype(jnp.int32)
    rank = route_buf[:, TOP_K:2 * TOP_K].astype(jnp.int32)
    pos_flat = (starts[idx] + rank).reshape(n4)

    xs_sorted = _scatter_rows(pos_flat, xn_buf)
    e3 = lambda bias: bias[0].reshape(N_EXPERTS, 1, D_MODEL)
    ys_sorted = _experts(meta, xs_sorted, w_gate[0], e3(b_gate), w_up[0], e3(b_up), w_down[0], e3(b_down))
    y_prompt = _combine(pos_flat, h_buf, route_buf, nfin_row, ys_sorted, 0, n_p).reshape(bp, lp, D_MODEL)
    y_sample = _combine(pos_flat, h_buf, route_buf, nfin_row, ys_sorted, n_p, n_s).reshape(bs, ls, D_MODEL)

    heads = lambda a, b, l: a.reshape(1, b, l, SB_HEADS, SB_HEAD_DIM)
    return (y_prompt, y_sample,
            heads(k_prompt, bp, lp), heads(v_prompt, bp, lp),
            conv_p[:, 8 - (CONV_WIDTH - 1):][None], state_p[None],
            heads(k_sample, bs, ls), heads(v_sample, bs, ls),
            conv_s[:, 8 - (CONV_WIDTH - 1):][None], state_s[None])
```

```python
import functools
import math

import jax
import jax.numpy as jnp
from jax import lax
from jax.experimental import pallas as pl
from jax.experimental.pallas import tpu as pltpu

F32 = jnp.float32
BF16 = jnp.bfloat16

D_MODEL = 1024
SB_HEAD_DIM = 64
SB_WIDTH = 512
SB_HEADS = SB_WIDTH // SB_HEAD_DIM
SB_PAIRS = SB_WIDTH // 128
GDN_HEAD_DIM = 128
GDN_HEADS = 4
GDN_WIDTH = 512
CONV_WIDTH = 4
CONV_CH = 3 * GDN_WIDTH
IN_WIDTH = 3 * SB_WIDTH + 4 * GDN_WIDTH + 2 * GDN_HEADS
IN_WIDTH_PAD = 3 * SB_WIDTH + 4 * GDN_WIDTH + 128
N_EXPERTS = 32
TOP_K = 4
SWIGLU_LIMIT = 7.0
SWIGLU_ALPHA = 1.702
NORM_EPS = 1e-6
L2_EPS = 1e-6
CHUNK = 64

LANES = 128
VMEM_LIMIT_BYTES = 48 * 1024 * 1024
VMEM_LIMIT_EXPERTS = 56 * 1024 * 1024
SB_LOG_CUTOFF = 104.0

TM_DENSE = 256
TQ_SB = 256
TK_SB = 128
SB_PAIRS_PER_STEP = 4
SB_CACHE_WINDOW = 512
GDN_BATCH_PER_STEP = 4
TM_ROWS = 256
TM_GROUP = 256


def _cparams(sem, limit=VMEM_LIMIT_BYTES):
    return pltpu.CompilerParams(dimension_semantics=sem, vmem_limit_bytes=limit)


def _softplus(z):
    return jnp.maximum(z, 0.0) + jnp.log(1.0 + jnp.exp(-jnp.abs(z)))


def _sigmoid(z):
    return 1.0 / (1.0 + jnp.exp(-z))


def _store_heads(ref, val, tm):
    for h in range(SB_HEADS):
        ref[pl.ds(h, tm, stride=SB_HEADS), :] = val[:, h * SB_HEAD_DIM:(h + 1) * SB_HEAD_DIM]


ROW_TILE = D_MODEL // LANES


def _store_row_tiles(ref, val, tm):
    for j in range(ROW_TILE):
        ref[pl.ds(j, tm, stride=ROW_TILE), :] = val[:, j * LANES:(j + 1) * LANES]


def _load_row_tiles(ref, tm):
    return jnp.concatenate([ref[pl.ds(j, tm, stride=ROW_TILE), :] for j in range(ROW_TILE)], axis=-1)


def _in_proj_kernel(x_ref, g_ref, w_ref, qsb_ref, ksb_ref, vsb_ref, kbf_ref, vbf_ref, gdn_ref, gate_ref, ab_ref,
                    *, tm):
    x = x_ref[...]
    xn = x * lax.rsqrt(jnp.mean(x * x, axis=-1, keepdims=True) + NORM_EPS) * g_ref[...]
    xn = xn.astype(BF16)

    def mm(lo, hi):
        return jnp.dot(xn, w_ref[:, lo:hi], preferred_element_type=F32)

    qsb_ref[...] = (mm(0, 512) * (SB_HEAD_DIM ** -0.5)).astype(BF16)
    k = mm(512, 1024)
    _store_heads(ksb_ref, k, tm)
    kbf_ref[...] = k.astype(BF16)
    v = mm(1024, 1536)
    _store_heads(vsb_ref, v, tm)
    vbf_ref[...] = v.astype(BF16)
    for j in range(3):
        gdn_ref[:, j * 512:(j + 1) * 512] = mm(1536 + j * 512, 2048 + j * 512)
    gate_ref[...] = mm(3072, 3584)
    ab_ref[...] = mm(3584, 3712)


def _in_proj(x2d, g_row, w_bf):
    n = x2d.shape[0]
    tm = min(TM_DENSE, n)
    row = lambda w: pl.BlockSpec((tm, w), lambda i: (i, 0))
    head_rows = pl.BlockSpec((tm * SB_HEADS, SB_HEAD_DIM), lambda i: (i, 0))
    out_shape = (
        jax.ShapeDtypeStruct((n, SB_WIDTH), BF16),
        jax.ShapeDtypeStruct((n * SB_HEADS, SB_HEAD_DIM), F32),
        jax.ShapeDtypeStruct((n * SB_HEADS, SB_HEAD_DIM), F32),
        jax.ShapeDtypeStruct((n, SB_WIDTH), BF16),
        jax.ShapeDtypeStruct((n, SB_WIDTH), BF16),
        jax.ShapeDtypeStruct((n, CONV_CH), F32),
        jax.ShapeDtypeStruct((n, GDN_WIDTH), F32),
        jax.ShapeDtypeStruct((n, LANES), F32),
    )
    return pl.pallas_call(
        functools.partial(_in_proj_kernel, tm=tm),
        grid=(n // tm,),
        in_specs=[row(D_MODEL), pl.BlockSpec((1, D_MODEL), lambda i: (0, 0)),
                  pl.BlockSpec((D_MODEL, IN_WIDTH_PAD), lambda i: (0, 0))],
        out_specs=(row(SB_WIDTH), head_rows, head_rows, row(SB_WIDTH), row(SB_WIDTH),
                   row(CONV_CH), row(GDN_WIDTH), row(LANES)),
        out_shape=out_shape,
        compiler_params=_cparams(("arbitrary",)),
        name="in_proj",
    )(x2d, g_row, w_bf)


def _sb_step(qs, k_blks, v_blks, qpos, kpos, acc_ref, c_ref):
    pairs = len(qs)
    chains = [(p, h) for p in range(pairs) for h in range(2)]
    tk = k_blks[0].shape[0]
    lane = lax.broadcasted_iota(jnp.int32, (1, LANES), 1)
    visible = kpos < qpos
    jj = lax.broadcasted_iota(jnp.int32, (tk, tk), 0)
    ss = lax.broadcasted_iota(jnp.int32, (tk, tk), 1)
    tri = jnp.where(jj >= ss, 1.0, 0.0).astype(BF16)
    nt = (((1,), (1,)), ((), ()))

    z = {ch: lax.dot_general(qs[ch[0]][ch[1]], k_blks[ch[0]], nt, preferred_element_type=F32) for ch in chains}
    sp = {ch: jnp.where(visible, _softplus(z[ch]), 0.0) for ch in chains}
    sp_hi = {ch: sp[ch].astype(BF16) for ch in chains}
    sp_lo = {ch: (sp[ch] - sp_hi[ch].astype(F32)).astype(BF16) for ch in chains}
    r = {ch: jnp.dot(sp_hi[ch], tri, preferred_element_type=F32) + jnp.dot(sp_lo[ch], tri, preferred_element_type=F32)
         for ch in chains}
    w = {ch: jnp.where(visible, jnp.exp(z[ch] - r[ch] - c_ref[2 * ch[0] + ch[1]][:, :tk]), 0.0).astype(BF16)
         for ch in chains}
    pv = {ch: jnp.dot(w[ch], v_blks[ch[0]], preferred_element_type=F32) for ch in chains}
    cmin = None
    for p in range(pairs):
        acc_ref[p] += jnp.where(lane < SB_HEAD_DIM, pv[(p, 0)], pv[(p, 1)])
    for ch in chains:
        c_new = c_ref[2 * ch[0] + ch[1]] + r[ch][:, 0:1]
        c_ref[2 * ch[0] + ch[1]] = c_new
        m = jnp.min(c_new)
        cmin = m if cmin is None else jnp.minimum(cmin, m)
    return cmin


def _sb_finish(acc, nsb_row):
    lane = lax.broadcasted_iota(jnp.int32, (1, LANES), 1)
    first = lane < SB_HEAD_DIM
    sq = acc * acc
    s_all = jnp.sum(sq, axis=-1, keepdims=True)
    s0 = jnp.sum(jnp.where(first, sq, 0.0), axis=-1, keepdims=True)
    ms = jnp.where(first, s0, s_all - s0) * (1.0 / SB_HEAD_DIM)
    return acc * lax.rsqrt(ms + NORM_EPS) * nsb_row


def _split_heads(q):
    lane = lax.broadcasted_iota(jnp.int32, (1, LANES), 1)
    zero = jnp.zeros_like(q)
    return jnp.where(lane < SB_HEAD_DIM, q, zero), jnp.where(lane >= SB_HEAD_DIM, q, zero)


def _sb_continue(carry):
    j, cmin = carry
    return jnp.logical_and(j >= 0, cmin < SB_LOG_CUTOFF)


def _sb_prompt_kernel(q_ref, k_ref, v_ref, nsb_ref, o_ref, acc_ref, c_ref, *, tq, tk, pairs):
    i = pl.program_id(2)
    lanes = lambda p: slice(p * LANES, (p + 1) * LANES)
    qs = [_split_heads(q_ref[0, :, lanes(p)]) for p in range(pairs)]
    acc_ref[...] = jnp.zeros_like(acc_ref)
    c_ref[...] = jnp.zeros_like(c_ref)
    qpos = i * tq + lax.broadcasted_iota(jnp.int32, (tq, 1), 0)

    def body(carry):
        j, _ = carry
        ks = pl.multiple_of(j * tk, tk)
        kpos = ks + lax.broadcasted_iota(jnp.int32, (1, tk), 1)
        k_blks = [k_ref[0, pl.ds(ks, tk), lanes(p)] for p in range(pairs)]
        v_blks = [v_ref[0, pl.ds(ks, tk), lanes(p)] for p in range(pairs)]
        return j - 1, _sb_step(qs, k_blks, v_blks, qpos, kpos, acc_ref, c_ref)

    lax.while_loop(_sb_continue, body, ((i + 1) * (tq // tk) - 1, jnp.float32(0.0)))
    for p in range(pairs):
        o_ref[0, :, lanes(p)] = _sb_finish(acc_ref[p], nsb_ref[...]).astype(o_ref.dtype)


def _sb_prompt(q, k, v, nsb_row):
    b, l, _ = q.shape
    tq = min(TQ_SB, l)
    tk = min(TK_SB, tq)
    pairs = SB_PAIRS_PER_STEP
    width = pairs * LANES
    qspec = pl.BlockSpec((1, tq, width), lambda bb, hp, i: (bb, i, hp))
    kvspec = pl.BlockSpec((1, l, width), lambda bb, hp, i: (bb, 0, hp))
    return pl.pallas_call(
        functools.partial(_sb_prompt_kernel, tq=tq, tk=tk, pairs=pairs),
        grid=(b, SB_PAIRS // pairs, l // tq),
        in_specs=[qspec, kvspec, kvspec, pl.BlockSpec((1, LANES), lambda bb, hp, i: (0, 0))],
        out_specs=qspec,
        out_shape=jax.ShapeDtypeStruct((b, l, SB_WIDTH), BF16),
        scratch_shapes=[pltpu.VMEM((pairs, tq, LANES), F32), pltpu.VMEM((2 * pairs, tq, LANES), F32)],
        compiler_params=_cparams(("arbitrary", "arbitrary", "arbitrary")),
        name="sb_prompt",
    )(q, k, v, nsb_row)


def _sb_sample_kernel(q_ref, kn_ref, vn_ref, kw_ref, vw_ref, kc_hbm, vc_hbm, nsb_ref, o_ref,
                      acc_ref, c_ref, kbuf, vbuf, sem, *, past, window, tk):
    b = pl.program_id(0)
    hp = pl.program_id(1)
    t = q_ref.shape[1]
    qs = [_split_heads(q_ref[0])]
    acc_ref[...] = jnp.zeros_like(acc_ref)
    c_ref[...] = jnp.zeros_like(c_ref)
    qpos = past + lax.broadcasted_iota(jnp.int32, (t, 1), 0)

    kpos_new = past + lax.broadcasted_iota(jnp.int32, (1, t), 1)
    cmin0 = _sb_step(qs, [kn_ref[0]], [vn_ref[0]], qpos, kpos_new, acc_ref, c_ref)

    def wbody(carry):
        j, _ = carry
        ws = pl.multiple_of(j * tk, tk)
        kpos = (past - window) + ws + lax.broadcasted_iota(jnp.int32, (1, tk), 1)
        cmin = _sb_step(qs, [kw_ref[0, pl.ds(ws, tk), :].astype(BF16)],
                        [vw_ref[0, pl.ds(ws, tk), :].astype(BF16)], qpos, kpos, acc_ref, c_ref)
        return j - 1, cmin

    _, cmin1 = lax.while_loop(_sb_continue, wbody, (window // tk - 1, cmin0))

    n_old = (past - window) // tk
    if n_old > 0:
        def pair_of(buf):
            return jnp.concatenate([buf[0], buf[1]], axis=-1).astype(BF16)

        def obody(carry):
            j, _ = carry
            keys = pl.ds(pl.multiple_of(j * tk, tk), tk)
            copies = []
            for hh in range(2):
                copies.append(pltpu.make_async_copy(kc_hbm.at[0, b, keys, 2 * hp + hh], kbuf.at[hh], sem.at[hh]))
                copies.append(pltpu.make_async_copy(vc_hbm.at[0, b, keys, 2 * hp + hh], vbuf.at[hh], sem.at[2 + hh]))
            for cp in copies:
                cp.start()
            for cp in copies:
                cp.wait()
            kpos = j * tk + lax.broadcasted_iota(jnp.int32, (1, tk), 1)
            cmin = _sb_step(qs, [pair_of(kbuf)], [pair_of(vbuf)], qpos, kpos, acc_ref, c_ref)
            return j - 1, cmin

        lax.while_loop(_sb_continue, obody, (n_old - 1, cmin1))

    o_ref[0] = _sb_finish(acc_ref[0], nsb_ref[...]).astype(o_ref.dtype)


def _sb_sample(q, k_new, v_new, k_win, v_win, k_cache, v_cache, nsb_row):
    b, t, _ = q.shape
    past = k_cache.shape[2]
    window = k_win.shape[1]
    tk = min(TK_SB, window)
    assert window % tk == 0 and (past - window) % tk == 0
    new_spec = pl.BlockSpec((1, t, LANES), lambda bb, hp: (bb, 0, hp))
    win_spec = pl.BlockSpec((1, window, LANES), lambda bb, hp: (bb, 0, hp))
    any_spec = pl.BlockSpec(memory_space=pl.ANY)
    return pl.pallas_call(
        functools.partial(_sb_sample_kernel, past=past, window=window, tk=tk),
        grid=(b, SB_PAIRS),
        in_specs=[new_spec, new_spec, new_spec, win_spec, win_spec, any_spec, any_spec,
                  pl.BlockSpec((1, LANES), lambda bb, hp: (0, 0))],
        out_specs=new_spec,
        out_shape=jax.ShapeDtypeStruct((b, t, SB_WIDTH), BF16),
        scratch_shapes=[pltpu.VMEM((1, t, LANES), F32), pltpu.VMEM((2, t, LANES), F32),
                        pltpu.VMEM((2, tk, SB_HEAD_DIM), F32), pltpu.VMEM((2, tk, SB_HEAD_DIM), F32),
                        pltpu.SemaphoreType.DMA((4,))],
        compiler_params=_cparams(("arbitrary", "arbitrary")),
        name="sb_sample",
    )(q, k_new, v_new, k_win, v_win, k_cache, v_cache, nsb_row)


def _gdn_chunk(nb, qkv_ref, gate_ref, ab_ref, cw_ref, alog_ref, dtb_ref, ng_ref, o_ref, xp_ref, s_ref, chunk):
    ii = lax.broadcasted_iota(jnp.int32, (chunk, chunk), 0)
    jj = lax.broadcasted_iota(jnp.int32, (chunk, chunk), 1)
    tri = jnp.where(ii >= jj, 1.0, 0.0).astype(F32)
    scale = GDN_HEAD_DIM ** -0.5
    hd = GDN_HEAD_DIM
    n_fac = int(math.log2(chunk)) - 1
    chains = [(bi, h) for bi in range(nb) for h in range(GDN_HEADS)]

    ys, gcs, gcts, betas = [], [], [], []
    for bi in range(nb):
        xp_ref[bi, 8:8 + chunk, :] = qkv_ref[bi]
        y = cw_ref[0:1, :] * xp_ref[bi, 5:5 + chunk, :]
        for i in range(1, CONV_WIDTH):
            y = y + cw_ref[i:i + 1, :] * xp_ref[bi, 5 + i:5 + i + chunk, :]
        ys.append(y * _sigmoid(y))
        xp_ref[bi, 0:8, :] = xp_ref[bi, chunk:chunk + 8, :]
        ab = ab_ref[bi]
        g_tile = -jnp.exp(alog_ref[...]) * _softplus(ab + dtb_ref[...])
        betas.append(_sigmoid(ab))
        gc_tile = jnp.dot(tri, g_tile, preferred_element_type=F32, precision=lax.Precision.HIGHEST)
        gcs.append(gc_tile)
        gcts.append(gc_tile.T)

    kn, qs, kb, vb, decay, egc, glast, gcol = {}, {}, {}, {}, {}, {}, {}, {}
    for ch in chains:
        bi, h = ch
        y = ys[bi]
        qh = y[:, h * hd:(h + 1) * hd]
        kh = y[:, GDN_WIDTH + h * hd:GDN_WIDTH + (h + 1) * hd]
        vh = y[:, 2 * GDN_WIDTH + h * hd:2 * GDN_WIDTH + (h + 1) * hd]
        qs[ch] = qh * lax.rsqrt(jnp.sum(qh * qh, axis=-1, keepdims=True) + L2_EPS) * scale
        kn[ch] = kh * lax.rsqrt(jnp.sum(kh * kh, axis=-1, keepdims=True) + L2_EPS)
        gcol[ch] = gcs[bi][:, h:h + 1]
        grow = gcts[bi][h:h + 1, :]
        glast[ch] = gcs[bi][chunk - 1:chunk, h:h + 1]
        bcol = betas[bi][:, GDN_HEADS + h:GDN_HEADS + h + 1]
        decay[ch] = jnp.exp(jnp.where(ii >= jj, gcol[ch] - grow, -1e30))
        egc[ch] = jnp.exp(gcol[ch])
        kb[ch] = kn[ch] * bcol
        vb[ch] = vh * bcol

    a2 = {ch: lax.dot_general(jnp.concatenate([kb[ch], qs[ch]], axis=0).astype(BF16), kn[ch].astype(BF16),
                              (((1,), (1,)), ((), ())), preferred_element_type=F32) for ch in chains}
    lb = {ch: jnp.where(ii > jj, a2[ch][:chunk] * decay[ch], 0.0).astype(BF16) for ch in chains}
    qk = {ch: (a2[ch][chunk:] * decay[ch]).astype(BF16) for ch in chains}

    rhs = {ch: jnp.concatenate([vb[ch], kb[ch] * egc[ch]], axis=1) for ch in chains}
    sol = {ch: rhs[ch] - jnp.dot(lb[ch], rhs[ch].astype(BF16), preferred_element_type=F32) for ch in chains}
    pw = {ch: jnp.dot(lb[ch], lb[ch], preferred_element_type=F32).astype(BF16) for ch in chains}
    for f in range(n_fac):
        sol = {ch: sol[ch] + jnp.dot(pw[ch], sol[ch].astype(BF16), preferred_element_type=F32) for ch in chains}
        if f + 1 < n_fac:
            pw = {ch: jnp.dot(pw[ch], pw[ch], preferred_element_type=F32).astype(BF16) for ch in chains}

    s_old = {ch: s_ref[ch[0], ch[1]] for ch in chains}
    ws = {ch: jnp.dot(jnp.concatenate([sol[ch][:, hd:], qs[ch] * egc[ch]], axis=0).astype(BF16),
                      s_old[ch].astype(BF16), preferred_element_type=F32) for ch in chains}
    v_new = {ch: (sol[ch][:, :hd] - ws[ch][:chunk]).astype(BF16) for ch in chains}
    o = {ch: ws[ch][chunk:] + jnp.dot(qk[ch], v_new[ch], preferred_element_type=F32) for ch in chains}
    for ch in chains:
        kd = (kn[ch] * jnp.exp(glast[ch] - gcol[ch])).astype(BF16)
        s_ref[ch[0], ch[1]] = s_old[ch] * jnp.exp(glast[ch]) + lax.dot_general(
            kd, v_new[ch], (((0,), (0,)), ((), ())), preferred_element_type=F32)

    for ch in chains:
        bi, h = ch
        sl = slice(h * hd, (h + 1) * hd)
        on = o[ch] * lax.rsqrt(jnp.mean(o[ch] * o[ch], axis=-1, keepdims=True) + NORM_EPS) * ng_ref[...]
        gt = gate_ref[bi][:, sl]
        o_ref[bi, :, sl] = (on * (gt * _sigmoid(gt))).astype(o_ref.dtype)


def _gdn_kernel(qkv_ref, gate_ref, ab_ref, cinit_ref, sinit_ref, cw_ref, alog_ref, dtb_ref, ng_ref,
                o_ref, cout_ref, sout_ref, xp_ref, s_ref, *, chunk, nb):
    c = pl.program_id(1)

    @pl.when(c == 0)
    def _():
        xp_ref[:, 0:8, :] = cinit_ref[...]
        s_ref[...] = sinit_ref[...]

    _gdn_chunk(nb, qkv_ref, gate_ref, ab_ref, cw_ref, alog_ref, dtb_ref, ng_ref, o_ref, xp_ref, s_ref, chunk)

    @pl.when(c == pl.num_programs(1) - 1)
    def _():
        cout_ref[...] = xp_ref[:, 0:8, :]
        sout_ref[...] = s_ref[...]


def _gdn(qkv, gate, ab, conv_init, s_init, cw8, alog_row, dtb_row, ng_row):
    b, l, _ = qkv.shape
    chunk = min(CHUNK, l)
    nc = l // chunk
    nb = min(GDN_BATCH_PER_STEP, b)
    assert b % nb == 0
    tok = lambda w: pl.BlockSpec((nb, chunk, w), lambda bb, c: (bb, c, 0))
    const = lambda shape: pl.BlockSpec(shape, lambda bb, c: (0,) * len(shape))
    per_b3 = pl.BlockSpec((nb, 8, CONV_CH), lambda bb, c: (bb, 0, 0))
    per_b4 = pl.BlockSpec((nb, GDN_HEADS, GDN_HEAD_DIM, GDN_HEAD_DIM), lambda bb, c: (bb, 0, 0, 0))
    return pl.pallas_call(
        functools.partial(_gdn_kernel, chunk=chunk, nb=nb),
        grid=(b // nb, nc),
        in_specs=[tok(CONV_CH), tok(GDN_WIDTH), tok(LANES), per_b3, per_b4,
                  const((8, CONV_CH)), const((1, LANES)), const((1, LANES)), const((1, LANES))],
        out_specs=(tok(GDN_WIDTH), per_b3, per_b4),
        out_shape=(jax.ShapeDtypeStruct((b, l, GDN_WIDTH), BF16),
                   jax.ShapeDtypeStruct((b, 8, CONV_CH), F32),
                   jax.ShapeDtypeStruct((b, GDN_HEADS, GDN_HEAD_DIM, GDN_HEAD_DIM), F32)),
        scratch_shapes=[pltpu.VMEM((nb, chunk + 8, CONV_CH), F32),
                        pltpu.VMEM((nb, GDN_HEADS, GDN_HEAD_DIM, GDN_HEAD_DIM), F32)],
        compiler_params=_cparams(("arbitrary", "arbitrary")),
        name="gdn",
    )(qkv, gate, ab, conv_init, s_init, cw8, alog_row, dtb_row, ng_row)


def _out_route_kernel(xp_ref, osbp_ref, ogdnp_ref, xs_ref, osbs_ref, ogdns_ref, wo_ref, nf_ref, rw_ref, rb_ref,
                      h_ref, xn_ref, route_ref, cnt_ref, run_ref, *, tm, steps_p):
    i = pl.program_id(0)

    @pl.when(i == 0)
    def _():
        run_ref[...] = jnp.zeros_like(run_ref)

    is_p = i < steps_p
    x = jnp.where(is_p, xp_ref[...], xs_ref[...])
    osb = jnp.where(is_p, osbp_ref[...], osbs_ref[...])
    ogdn = jnp.where(is_p, ogdnp_ref[...], ogdns_ref[...])
    h = (x
         + jnp.dot(osb, wo_ref[0:SB_WIDTH, :], preferred_element_type=F32)
         + jnp.dot(ogdn, wo_ref[SB_WIDTH:, :], preferred_element_type=F32))
    h_ref[...] = h
    xn = h * lax.rsqrt(jnp.mean(h * h, axis=-1, keepdims=True) + NORM_EPS) * nf_ref[...]
    _store_row_tiles(xn_ref, xn, tm)

    lane = lax.broadcasted_iota(jnp.int32, (1, LANES), 1)
    lane_f = lane.astype(F32)
    rw = rw_ref[...]
    xn_hi = xn.astype(BF16)
    xn_lo = (xn - xn_hi.astype(F32)).astype(BF16)
    rw_hi = rw.astype(BF16)
    rw_lo = (rw - rw_hi.astype(F32)).astype(BF16)
    logits = (jnp.dot(xn_hi, rw_hi, preferred_element_type=F32) + jnp.dot(xn_lo, rw_hi, preferred_element_type=F32)
              + jnp.dot(xn_hi, rw_lo, preferred_element_type=F32)) + rb_ref[...]
    logits = jnp.where(lane < N_EXPERTS, logits, -jnp.inf)
    vals, hots = [], []
    for _ in range(TOP_K):
        m = jnp.max(logits, axis=-1, keepdims=True)
        idx = jnp.min(jnp.where(logits == m, lane_f, float(LANES)), axis=-1, keepdims=True)
        hot = lane_f == idx
        logits = jnp.where(hot, -jnp.inf, logits)
        vals.append(m)
        hots.append((hot, idx))
    exps = [jnp.exp(v - vals[0]) for v in vals]
    denom = exps[0] + exps[1] + exps[2] + exps[3]

    multi = jnp.zeros((tm, LANES), F32)
    for hot, _ in hots:
        multi = jnp.where(hot, 1.0, multi)
    ii = lax.broadcasted_iota(jnp.int32, (tm, tm), 0)
    jj = lax.broadcasted_iota(jnp.int32, (tm, tm), 1)
    earlier = jnp.where(ii > jj, 1.0, 0.0).astype(BF16)
    run = run_ref[0:1, :]
    rank_all = jnp.dot(earlier, multi.astype(BF16), preferred_element_type=F32) + run
    run_new = run + jnp.sum(multi, axis=0, keepdims=True)
    run_ref[...] = jnp.broadcast_to(run_new, run_ref.shape)
    cnt_ref[...] = jnp.broadcast_to(run_new, cnt_ref.shape)

    route = jnp.zeros((tm, LANES), F32)
    for k, (hot, idx) in enumerate(hots):
        rank = jnp.sum(jnp.where(hot, rank_all, 0.0), axis=-1, keepdims=True)
        route = jnp.where(lane == k, idx, route)
        route = jnp.where(lane == TOP_K + k, rank, route)
        route = jnp.where(lane == 2 * TOP_K + k, exps[k] / denom, route)
    route_ref[...] = route


def _out_route(xp, osbp, ogdnp, xs, osbs, ogdns, wo_bf, nf_row, rw_pad, rb_row):
    n_p, n_s = xp.shape[0], xs.shape[0]
    tm = min(TM_DENSE, n_p, n_s)
    steps_p, steps_s = n_p // tm, n_s // tm
    prow = lambda w: pl.BlockSpec((tm, w), lambda i: (jnp.minimum(i, steps_p - 1), 0))
    srow = lambda w: pl.BlockSpec((tm, w), lambda i: (jnp.maximum(i - steps_p, 0), 0))
    orow = lambda w: pl.BlockSpec((tm, w), lambda i: (i, 0))
    const = lambda shape: pl.BlockSpec(shape, lambda i: (0,) * len(shape))
    n_total = n_p + n_s
    out_shape = (jax.ShapeDtypeStruct((n_total, D_MODEL), F32),
                 jax.ShapeDtypeStruct((n_total * ROW_TILE, LANES), F32),
                 jax.ShapeDtypeStruct((n_total, LANES), F32),
                 jax.ShapeDtypeStruct((8, LANES), F32))
    return pl.pallas_call(
        functools.partial(_out_route_kernel, tm=tm, steps_p=steps_p),
        grid=(steps_p + steps_s,),
        in_specs=[prow(D_MODEL), prow(SB_WIDTH), prow(GDN_WIDTH), srow(D_MODEL), srow(SB_WIDTH), srow(GDN_WIDTH),
                  const((D_MODEL, D_MODEL)), const((1, D_MODEL)), const((D_MODEL, LANES)), const((1, LANES))],
        out_specs=(orow(D_MODEL), pl.BlockSpec((tm * ROW_TILE, LANES), lambda i: (i, 0)), orow(LANES),
                   const((8, LANES))),
        out_shape=out_shape,
        scratch_shapes=[pltpu.VMEM((8, LANES), F32)],
        compiler_params=_cparams(("arbitrary",)),
        name="out_route",
    )(xp, osbp, ogdnp, xs, osbs, ogdns, wo_bf, nf_row, rw_pad, rb_row)


def _scatter_rows_kernel(pos_ref, x_ref, xs_hbm, sem, *, tm):
    def row_copy(t, p):
        return pltpu.make_async_copy(x_ref.at[pl.ds(pl.multiple_of(t * ROW_TILE, ROW_TILE), ROW_TILE)],
                                     xs_hbm.at[pl.ds(pl.multiple_of(p * ROW_TILE, ROW_TILE), ROW_TILE)], sem)

    def issue(t, carry):
        for k in range(TOP_K):
            row_copy(t, pos_ref[t * TOP_K + k]).start(priority=k % 2)
        return carry

    lax.fori_loop(0, tm, issue, 0)

    def drain(t, carry):
        for k in range(TOP_K):
            row_copy(0, 0).wait()
        return carry

    lax.fori_loop(0, tm, drain, 0)


def _scatter_rows(pos_flat, xn):
    n = xn.shape[0] // ROW_TILE
    tm = min(TM_ROWS, n)
    return pl.pallas_call(
        functools.partial(_scatter_rows_kernel, tm=tm),
        grid=(n // tm,),
        in_specs=[pl.BlockSpec((tm * TOP_K,), lambda i: (i,), memory_space=pltpu.SMEM),
                  pl.BlockSpec((tm * ROW_TILE, LANES), lambda i: (i, 0))],
        out_specs=pl.BlockSpec(memory_space=pl.ANY),
        out_shape=jax.ShapeDtypeStruct((n * TOP_K * ROW_TILE, LANES), xn.dtype),
        scratch_shapes=[pltpu.SemaphoreType.DMA(())],
        compiler_params=_cparams(("arbitrary",)),
        name="moe_scatter",
    )(pos_flat, xn)


def _experts_kernel(tile_ref, exp_ref, lo_ref, hi_ref, first_ref, newexp_ref, nvalid_ref,
                    x_ref, wg_ref, bg_ref, wu_ref, bu_ref, wd_ref, bd_ref, y_ref, wg_bf, wu_bf, wd_bf, *, tmg):
    v = pl.program_id(0)

    @pl.when(jnp.logical_and(v < nvalid_ref[0], newexp_ref[v] == 1))
    def _():
        wg_bf[...] = wg_ref[0].astype(BF16)
        wu_bf[...] = wu_ref[0].astype(BF16)
        wd_bf[...] = wd_ref[0].astype(BF16)

    @pl.when(v < nvalid_ref[0])
    def _():
        x = _load_row_tiles(x_ref, tmg).astype(BF16)
        gt = jnp.minimum(jnp.dot(x, wg_bf[...], preferred_element_type=F32) + bg_ref[0], SWIGLU_LIMIT)
        up = jnp.clip(jnp.dot(x, wu_bf[...], preferred_element_type=F32) + bu_ref[0], -SWIGLU_LIMIT, SWIGLU_LIMIT)
        hid = (up + 1.0) * (gt * _sigmoid(SWIGLU_ALPHA * gt))
        y = jnp.dot(hid.astype(BF16), wd_bf[...], preferred_element_type=F32) + bd_ref[0]
        rows = tile_ref[v] * tmg + lax.broadcasted_iota(jnp.int32, (tmg, 1), 0)
        mine = jnp.logical_and(rows >= lo_ref[v], rows < hi_ref[v])

        @pl.when(first_ref[v] == 1)
        def _():
            _store_row_tiles(y_ref, jnp.where(mine, y, 0.0), tmg)

        @pl.when(first_ref[v] == 0)
        def _():
            _store_row_tiles(y_ref, jnp.where(mine, y, _load_row_tiles(y_ref, tmg)), tmg)


def _experts(meta, xs, wg, bg, wu, bu, wd, bd):
    n4 = xs.shape[0] // ROW_TILE
    tmg = min(TM_GROUP, n4)
    n_visits = meta[0].shape[0]
    xspec = pl.BlockSpec((tmg * ROW_TILE, LANES), lambda v, tile, exp, *_: (tile[v], 0))
    wspec = pl.BlockSpec((1, D_MODEL, D_MODEL), lambda v, tile, exp, *_: (exp[v], 0, 0))
    bspec = pl.BlockSpec((1, 1, D_MODEL), lambda v, tile, exp, *_: (exp[v], 0, 0))
    grid_spec = pltpu.PrefetchScalarGridSpec(
        num_scalar_prefetch=len(meta),
        grid=(n_visits,),
        in_specs=[xspec, wspec, bspec, wspec, bspec, wspec, bspec],
        out_specs=xspec,
        scratch_shapes=[pltpu.VMEM((D_MODEL, D_MODEL), BF16)] * 3,
    )
    return pl.pallas_call(
        functools.partial(_experts_kernel, tmg=tmg),
        grid_spec=grid_spec,
        out_shape=jax.ShapeDtypeStruct((n4 * ROW_TILE, LANES), F32),
        compiler_params=_cparams(("arbitrary",), VMEM_LIMIT_EXPERTS),
        name="moe_experts",
    )(*meta, xs, wg, bg, wu, bu, wd, bd)


def _combine_kernel(pos_ref, h_ref, route_ref, nfin_ref, ys_hbm, o_ref, buf, sem, *, tm):
    def row_copy(t, k, p):
        return pltpu.make_async_copy(ys_hbm.at[pl.ds(pl.multiple_of(p * ROW_TILE, ROW_TILE), ROW_TILE)],
                                     buf.at[k, pl.ds(pl.multiple_of(t * ROW_TILE, ROW_TILE), ROW_TILE)], sem)

    def issue(t, carry):
        for k in range(TOP_K):
            row_copy(t, k, pos_ref[t * TOP_K + k]).start(priority=k % 2)
        return carry

    lax.fori_loop(0, tm, issue, 0)

    def drain(t, carry):
        for k in range(TOP_K):
            row_copy(0, k, 0).wait()
        return carry

    lax.fori_loop(0, tm, drain, 0)

    route = route_ref[...]
    h = h_ref[...]
    pieces = []
    for j in range(ROW_TILE):
        piece = h[:, j * LANES:(j + 1) * LANES]
        for k in range(TOP_K):
            piece = piece + route[:, 2 * TOP_K + k:2 * TOP_K + k + 1] * buf[k, pl.ds(j, tm, stride=ROW_TILE), :]
        pieces.append(piece)
    out = jnp.concatenate(pieces, axis=-1)
    o_ref[...] = out * lax.rsqrt(jnp.mean(out * out, axis=-1, keepdims=True) + NORM_EPS) * nfin_ref[...]


def _combine(pos_flat, h, route, nfin_row, ys, row0, n):
    tm = min(TM_ROWS, n)
    off = row0 // tm
    return pl.pallas_call(
        functools.partial(_combine_kernel, tm=tm),
        grid=(n // tm,),
        in_specs=[pl.BlockSpec((tm * TOP_K,), lambda i: (i + off,), memory_space=pltpu.SMEM),
                  pl.BlockSpec((tm, D_MODEL), lambda i: (i + off, 0)),
                  pl.BlockSpec((tm, LANES), lambda i: (i + off, 0)),
                  pl.BlockSpec((1, D_MODEL), lambda i: (0, 0)),
                  pl.BlockSpec(memory_space=pl.ANY)],
        out_specs=pl.BlockSpec((tm, D_MODEL), lambda i: (i, 0)),
        out_shape=jax.ShapeDtypeStruct((n, D_MODEL), F32),
        scratch_shapes=[pltpu.VMEM((TOP_K, tm * ROW_TILE, LANES), F32), pltpu.SemaphoreType.DMA(())],
        compiler_params=_cparams(("arbitrary",)),
        name="moe_combine",
    )(pos_flat, h, route, nfin_row, ys)


def _group_plan(counts, n4, tmg):
    n_tiles = n4 // tmg
    n_visits = n_tiles + N_EXPERTS - 1
    ends = jnp.cumsum(counts)
    starts = ends - counts
    t_first = starts // tmg
    t_cnt = jnp.where(counts > 0, (ends - 1) // tmg - t_first + 1, 0)
    v_end = jnp.cumsum(t_cnt)
    v_start = v_end - t_cnt
    total = v_end[-1]
    v = jnp.arange(n_visits, dtype=jnp.int32)
    g = jnp.minimum(jnp.sum((v[:, None] >= v_end[None, :]).astype(jnp.int32), axis=1), N_EXPERTS - 1)
    g_last = jnp.max(jnp.where(counts > 0, jnp.arange(N_EXPERTS), 0)).astype(jnp.int32)
    valid = v < total
    g = jnp.where(valid, g, g_last)
    tile = jnp.where(valid, t_first[g] + (v - v_start[g]), n_tiles - 1).astype(jnp.int32)
    lo = jnp.where(valid, jnp.maximum(starts[g], tile * tmg), 0).astype(jnp.int32)
    hi = jnp.where(valid, jnp.minimum(ends[g], (tile + 1) * tmg), 0).astype(jnp.int32)
    prev_tile = jnp.concatenate([jnp.full((1,), -1, jnp.int32), tile[:-1]])
    first = (tile != prev_tile).astype(jnp.int32)
    prev_g = jnp.concatenate([jnp.full((1,), -1, jnp.int32), g[:-1]])
    newexp = (g != prev_g).astype(jnp.int32)
    meta = (tile, g, lo, hi, first, newexp, total.reshape(1).astype(jnp.int32))
    return meta, starts


def _pad_rows(a, rows):
    return jnp.concatenate([a, jnp.zeros((rows - a.shape[0],) + a.shape[1:], a.dtype)], axis=0)


def kernel(x_prompt, x_sample, cache_sb_k, cache_sb_v, cache_gdn_conv, state_gdn, norm_mix, w_in, conv_w, a_log,
           dt_bias, norm_sb, norm_gdn, w_out, norm_ffn, router_w, router_b, w_gate, b_gate, w_up, b_up, w_down,
           b_down, norm_final):
    bp, lp, _ = x_prompt.shape
    bs, ls, _ = x_sample.shape
    past = cache_sb_k.shape[2]
    n_p, n_s = bp * lp, bs * ls
    n_tot = n_p + n_s
    assert norm_mix.shape[0] == 1, "single-layer trunk"
    tm_min = min(TM_DENSE, TM_ROWS, n_p, n_s)
    assert n_p % tm_min == 0 and n_s % tm_min == 0 and n_p % min(TM_ROWS, n_s) == 0
    assert (n_tot * TOP_K) % TM_GROUP == 0 and n_tot % TM_ROWS == 0
    assert lp % min(TQ_SB, lp) == 0 and lp % min(CHUNK, lp) == 0 and ls % min(CHUNK, ls) == 0 and ls >= 8
    window = min(SB_CACHE_WINDOW, past)

    w_in_bf = jnp.pad(w_in[0], ((0, 0), (0, IN_WIDTH_PAD - IN_WIDTH))).astype(BF16)
    nmix_row = norm_mix[0].reshape(1, D_MODEL)
    nsb_row = jnp.tile(norm_sb[0], 2).reshape(1, LANES)
    ng_row = norm_gdn[0].reshape(1, LANES)
    cw8 = _pad_rows(conv_w[0], 8)
    alog_row = jnp.pad(a_log[0], (0, LANES - GDN_HEADS)).reshape(1, LANES)
    dtb_row = jnp.pad(dt_bias[0], (0, LANES - GDN_HEADS)).reshape(1, LANES)
    wo_bf = w_out[0].astype(BF16)
    nf_row = norm_ffn[0].reshape(1, D_MODEL)
    rw_pad = jnp.pad(router_w[0], ((0, 0), (0, LANES - N_EXPERTS)))
    rb_row = jnp.pad(router_b[0], (0, LANES - N_EXPERTS)).reshape(1, LANES)
    nfin_row = norm_final.reshape(1, D_MODEL)

    xp2 = x_prompt.reshape(n_p, D_MODEL)
    xs2 = x_sample.reshape(n_s, D_MODEL)

    qsb, k_prompt, v_prompt, kbf, vbf, gdn_in, gate, ab = _in_proj(xp2, nmix_row, w_in_bf)
    r3 = lambda a, b, l: a.reshape(b, l, a.shape[-1])
    osb_p = _sb_prompt(r3(qsb, bp, lp), r3(kbf, bp, lp), r3(vbf, bp, lp), nsb_row)
    ogdn_p, conv_p, state_p = _gdn(
        r3(gdn_in, bp, lp), r3(gate, bp, lp), r3(ab, bp, lp),
        jnp.zeros((bp, 8, CONV_CH), F32), jnp.zeros((bp, GDN_HEADS, GDN_HEAD_DIM, GDN_HEAD_DIM), F32),
        cw8, alog_row, dtb_row, ng_row)

    qsb, k_sample, v_sample, kbf, vbf, gdn_in, gate, ab = _in_proj(xs2, nmix_row, w_in_bf)
    newest = lambda cache: cache[0, :, past - window:].reshape(bs, window, SB_WIDTH)
    osb_s = _sb_sample(r3(qsb, bs, ls), r3(kbf, bs, ls), r3(vbf, bs, ls), newest(cache_sb_k), newest(cache_sb_v),
                       cache_sb_k, cache_sb_v, nsb_row)
    conv_init = jnp.concatenate([jnp.zeros((bs, 8 - (CONV_WIDTH - 1), CONV_CH), F32), cache_gdn_conv[0]], axis=1)
    ogdn_s, conv_s, state_s = _gdn(r3(gdn_in, bs, ls), r3(gate, bs, ls), r3(ab, bs, ls), conv_init, state_gdn[0],
                                   cw8, alog_row, dtb_row, ng_row)

    h_buf, xn_buf, route_buf, cnt = _out_route(
        xp2, osb_p.reshape(n_p, SB_WIDTH), ogdn_p.reshape(n_p, GDN_WIDTH),
        xs2, osb_s.reshape(n_s, SB_WIDTH), ogdn_s.reshape(n_s, GDN_WIDTH), wo_bf, nf_row, rw_pad, rb_row)

    counts = cnt[0, :N_EXPERTS].astype(jnp.int32)
    n4 = n_tot * TOP_K
    tmg = min(TM_GROUP, n4)
    meta, starts = _group_plan(counts, n4, tmg)
    idx = route_buf[:, 0:TOP_K].astype(jnp.int32)
    rank = route_buf[:, TOP_K:2 * TOP_K].astype(jnp.int32)
    pos_flat = (starts[idx] + rank).reshape(n4)

    xs_sorted = _scatter_rows(pos_flat, xn_buf)
    e3 = lambda bias: bias[0].reshape(N_EXPERTS, 1, D_MODEL)
    ys_sorted = _experts(meta, xs_sorted, w_gate[0], e3(b_gate), w_up[0], e3(b_up), w_down[0], e3(b_down))
    y_prompt = _combine(pos_flat, h_buf, route_buf, nfin_row, ys_sorted, 0, n_p).reshape(bp, lp, D_MODEL)
    y_sample = _combine(pos_flat, h_buf, route_buf, nfin_row, ys_sorted, n_p, n_s).reshape(bs, ls, D_MODEL)

    heads = lambda a, b, l: a.reshape(1, b, l, SB_HEADS, SB_HEAD_DIM)
    return (y_prompt, y_sample,
            heads(k_prompt, bp, lp), heads(v_prompt, bp, lp),
            conv_p[:, 8 - (CONV_WIDTH - 1):][None], state_p[None],
            heads(k_sample, bs, ls), heads(v_sample, bs, ls),
            conv_s[:, 8 - (CONV_WIDTH - 1):][None], state_s[None])
```

```python
import functools
import math

import jax
import jax.numpy as jnp
from jax import lax
from jax.experimental import pallas as pl
from jax.experimental.pallas import tpu as pltpu

F32 = jnp.float32
BF16 = jnp.bfloat16

D_MODEL = 1024
SB_HEAD_DIM = 64
SB_WIDTH = 512
SB_HEADS = SB_WIDTH // SB_HEAD_DIM
SB_PAIRS = SB_WIDTH // 128
GDN_HEAD_DIM = 128
GDN_HEADS = 4
GDN_WIDTH = 512
CONV_WIDTH = 4
CONV_CH = 3 * GDN_WIDTH
IN_WIDTH = 3 * SB_WIDTH + 4 * GDN_WIDTH + 2 * GDN_HEADS
IN_WIDTH_PAD = 3 * SB_WIDTH + 4 * GDN_WIDTH + 128
N_EXPERTS = 32
TOP_K = 4
SWIGLU_LIMIT = 7.0
SWIGLU_ALPHA = 1.702
NORM_EPS = 1e-6
L2_EPS = 1e-6
CHUNK = 64

LANES = 128
VMEM_LIMIT_BYTES = 48 * 1024 * 1024
VMEM_LIMIT_EXPERTS = 56 * 1024 * 1024
SB_LOG_CUTOFF = 104.0

TM_DENSE = 256
TQ_SB = 256
TK_SB = 256
SB_PAIRS_PER_STEP = 4
SB_CACHE_WINDOW = 512
GDN_BATCH_PER_STEP = 4
TM_ROWS = 256
TM_GROUP = 256


def _cparams(sem, limit=VMEM_LIMIT_BYTES):
    return pltpu.CompilerParams(dimension_semantics=sem, vmem_limit_bytes=limit)


def _softplus(z):
    return jnp.maximum(z, 0.0) + jnp.log(1.0 + jnp.exp(-jnp.abs(z)))


def _sigmoid(z):
    return 1.0 / (1.0 + jnp.exp(-z))


def _store_heads(ref, val, tm):
    for h in range(SB_HEADS):
        ref[pl.ds(h, tm, stride=SB_HEADS), :] = val[:, h * SB_HEAD_DIM:(h + 1) * SB_HEAD_DIM]


ROW_TILE = D_MODEL // LANES


def _store_row_tiles(ref, val, tm):
    for j in range(ROW_TILE):
        ref[pl.ds(j, tm, stride=ROW_TILE), :] = val[:, j * LANES:(j + 1) * LANES]


def _load_row_tiles(ref, tm):
    return jnp.concatenate([ref[pl.ds(j, tm, stride=ROW_TILE), :] for j in range(ROW_TILE)], axis=-1)


def _in_proj_kernel(x_ref, g_ref, w_ref, qsb_ref, ksb_ref, vsb_ref, kbf_ref, vbf_ref, gdn_ref, gate_ref, ab_ref,
                    *, tm):
    x = x_ref[...]
    xn = x * lax.rsqrt(jnp.mean(x * x, axis=-1, keepdims=True) + NORM_EPS) * g_ref[...]
    xn = xn.astype(BF16)

    def mm(lo, hi):
        return jnp.dot(xn, w_ref[:, lo:hi], preferred_element_type=F32)

    qsb_ref[...] = (mm(0, 512) * (SB_HEAD_DIM ** -0.5)).astype(BF16)
    k = mm(512, 1024)
    _store_heads(ksb_ref, k, tm)
    kbf_ref[...] = k.astype(BF16)
    v = mm(1024, 1536)
    _store_heads(vsb_ref, v, tm)
    vbf_ref[...] = v.astype(BF16)
    for j in range(3):
        gdn_ref[:, j * 512:(j + 1) * 512] = mm(1536 + j * 512, 2048 + j * 512)
    gate_ref[...] = mm(3072, 3584)
    ab_ref[...] = mm(3584, 3712)


def _in_proj(x2d, g_row, w_bf):
    n = x2d.shape[0]
    tm = min(TM_DENSE, n)
    row = lambda w: pl.BlockSpec((tm, w), lambda i: (i, 0))
    head_rows = pl.BlockSpec((tm * SB_HEADS, SB_HEAD_DIM), lambda i: (i, 0))
    out_shape = (
        jax.ShapeDtypeStruct((n, SB_WIDTH), BF16),
        jax.ShapeDtypeStruct((n * SB_HEADS, SB_HEAD_DIM), F32),
        jax.ShapeDtypeStruct((n * SB_HEADS, SB_HEAD_DIM), F32),
        jax.ShapeDtypeStruct((n, SB_WIDTH), BF16),
        jax.ShapeDtypeStruct((n, SB_WIDTH), BF16),
        jax.ShapeDtypeStruct((n, CONV_CH), F32),
        jax.ShapeDtypeStruct((n, GDN_WIDTH), F32),
        jax.ShapeDtypeStruct((n, LANES), F32),
    )
    return pl.pallas_call(
        functools.partial(_in_proj_kernel, tm=tm),
        grid=(n // tm,),
        in_specs=[row(D_MODEL), pl.BlockSpec((1, D_MODEL), lambda i: (0, 0)),
                  pl.BlockSpec((D_MODEL, IN_WIDTH_PAD), lambda i: (0, 0))],
        out_specs=(row(SB_WIDTH), head_rows, head_rows, row(SB_WIDTH), row(SB_WIDTH),
                   row(CONV_CH), row(GDN_WIDTH), row(LANES)),
        out_shape=out_shape,
        compiler_params=_cparams(("arbitrary",)),
        name="in_proj",
    )(x2d, g_row, w_bf)


def _sb_step(qs, k_blks, v_blks, qpos, kpos, acc_ref, c_ref, kv_t=False):
    pairs = len(qs)
    chains = [(p, h) for p in range(pairs) for h in range(2)]
    tk = k_blks[0].shape[1 if kv_t else 0]
    lane = lax.broadcasted_iota(jnp.int32, (1, LANES), 1)
    visible = kpos < qpos
    jj = lax.broadcasted_iota(jnp.int32, (tk, tk), 0)
    ss = lax.broadcasted_iota(jnp.int32, (tk, tk), 1)
    tri = jnp.where(jj >= ss, 1.0, 0.0).astype(BF16)
    nt = (((1,), (1,)), ((), ()))

    if kv_t:
        z = {ch: jnp.dot(qs[ch[0]][ch[1]], k_blks[ch[0]], preferred_element_type=F32) for ch in chains}
    else:
        z = {ch: lax.dot_general(qs[ch[0]][ch[1]], k_blks[ch[0]], nt, preferred_element_type=F32) for ch in chains}
    sp = {ch: jnp.where(visible, _softplus(z[ch]), 0.0) for ch in chains}
    sp_hi = {ch: sp[ch].astype(BF16) for ch in chains}
    sp_lo = {ch: (sp[ch] - sp_hi[ch].astype(F32)).astype(BF16) for ch in chains}
    r = {ch: jnp.dot(sp_hi[ch], tri, preferred_element_type=F32) + jnp.dot(sp_lo[ch], tri, preferred_element_type=F32)
         for ch in chains}
    def mass(ch):
        c = c_ref[2 * ch[0] + ch[1]]
        return c[:, :tk] if tk <= LANES else jnp.concatenate([c] * (tk // LANES), axis=-1)

    w = {ch: jnp.where(visible, jnp.exp(z[ch] - r[ch] - mass(ch)), 0.0).astype(BF16) for ch in chains}
    if kv_t:
        pv = {ch: lax.dot_general(w[ch], v_blks[ch[0]], nt, preferred_element_type=F32) for ch in chains}
    else:
        pv = {ch: jnp.dot(w[ch], v_blks[ch[0]], preferred_element_type=F32) for ch in chains}
    cmin = None
    for p in range(pairs):
        acc_ref[p] += jnp.where(lane < SB_HEAD_DIM, pv[(p, 0)], pv[(p, 1)])
    for ch in chains:
        c_new = c_ref[2 * ch[0] + ch[1]] + r[ch][:, 0:1]
        c_ref[2 * ch[0] + ch[1]] = c_new
        m = jnp.min(c_new)
        cmin = m if cmin is None else jnp.minimum(cmin, m)
    return cmin


def _sb_finish(acc, nsb_row):
    lane = lax.broadcasted_iota(jnp.int32, (1, LANES), 1)
    first = lane < SB_HEAD_DIM
    sq = acc * acc
    s_all = jnp.sum(sq, axis=-1, keepdims=True)
    s0 = jnp.sum(jnp.where(first, sq, 0.0), axis=-1, keepdims=True)
    ms = jnp.where(first, s0, s_all - s0) * (1.0 / SB_HEAD_DIM)
    return acc * lax.rsqrt(ms + NORM_EPS) * nsb_row


def _split_heads(q):
    lane = lax.broadcasted_iota(jnp.int32, (1, LANES), 1)
    zero = jnp.zeros_like(q)
    return jnp.where(lane < SB_HEAD_DIM, q, zero), jnp.where(lane >= SB_HEAD_DIM, q, zero)


def _sb_continue(carry):
    j, cmin = carry
    return jnp.logical_and(j >= 0, cmin < SB_LOG_CUTOFF)


def _sb_prompt_kernel(q_ref, k_ref, v_ref, nsb_ref, o_ref, acc_ref, c_ref, *, tq, tk, pairs):
    i = pl.program_id(2)
    lanes = lambda p: slice(p * LANES, (p + 1) * LANES)
    qs = [_split_heads(q_ref[0, :, lanes(p)]) for p in range(pairs)]
    acc_ref[...] = jnp.zeros_like(acc_ref)
    c_ref[...] = jnp.zeros_like(c_ref)
    qpos = i * tq + lax.broadcasted_iota(jnp.int32, (tq, 1), 0)

    def body(carry):
        j, _ = carry
        ks = pl.multiple_of(j * tk, tk)
        kpos = ks + lax.broadcasted_iota(jnp.int32, (1, tk), 1)
        k_blks = [k_ref[0, pl.ds(ks, tk), lanes(p)] for p in range(pairs)]
        v_blks = [v_ref[0, pl.ds(ks, tk), lanes(p)] for p in range(pairs)]
        return j - 1, _sb_step(qs, k_blks, v_blks, qpos, kpos, acc_ref, c_ref)

    lax.while_loop(_sb_continue, body, ((i + 1) * (tq // tk) - 1, jnp.float32(0.0)))
    for p in range(pairs):
        o_ref[0, :, lanes(p)] = _sb_finish(acc_ref[p], nsb_ref[...]).astype(o_ref.dtype)


def _sb_prompt(q, k, v, nsb_row):
    b, l, _ = q.shape
    tq = min(TQ_SB, l)
    tk = min(TK_SB, tq)
    pairs = SB_PAIRS_PER_STEP
    width = pairs * LANES
    qspec = pl.BlockSpec((1, tq, width), lambda bb, hp, i: (bb, i, hp))
    kvspec = pl.BlockSpec((1, l, width), lambda bb, hp, i: (bb, 0, hp))
    return pl.pallas_call(
        functools.partial(_sb_prompt_kernel, tq=tq, tk=tk, pairs=pairs),
        grid=(b, SB_PAIRS // pairs, l // tq),
        in_specs=[qspec, kvspec, kvspec, pl.BlockSpec((1, LANES), lambda bb, hp, i: (0, 0))],
        out_specs=qspec,
        out_shape=jax.ShapeDtypeStruct((b, l, SB_WIDTH), BF16),
        scratch_shapes=[pltpu.VMEM((pairs, tq, LANES), F32), pltpu.VMEM((2 * pairs, tq, LANES), F32)],
        compiler_params=_cparams(("arbitrary", "arbitrary", "arbitrary")),
        name="sb_prompt",
    )(q, k, v, nsb_row)


def _sb_sample_kernel(q_ref, kn_ref, vn_ref, kw_ref, vw_ref, kc_hbm, vc_hbm, nsb_ref, o_ref,
                      acc_ref, c_ref, kbuf, vbuf, sem, *, past, window, tk):
    b = pl.program_id(0)
    hp = pl.program_id(1)
    t = q_ref.shape[1]
    qs = [_split_heads(q_ref[0])]
    acc_ref[...] = jnp.zeros_like(acc_ref)
    c_ref[...] = jnp.zeros_like(c_ref)
    qpos = past + lax.broadcasted_iota(jnp.int32, (t, 1), 0)

    kpos_new = past + lax.broadcasted_iota(jnp.int32, (1, t), 1)
    cmin0 = _sb_step(qs, [kn_ref[0]], [vn_ref[0]], qpos, kpos_new, acc_ref, c_ref)

    def pair_t(blk):
        return blk.reshape(LANES, tk).astype(BF16)

    def wbody(carry):
        j, _ = carry
        ws = pl.multiple_of(j * tk, tk)
        kpos = (past - window) + ws + lax.broadcasted_iota(jnp.int32, (1, tk), 1)
        cmin = _sb_step(qs, [pair_t(kw_ref[:, :, pl.ds(ws, tk)])], [pair_t(vw_ref[:, :, pl.ds(ws, tk)])],
                        qpos, kpos, acc_ref, c_ref, kv_t=True)
        return j - 1, cmin

    _, cmin1 = lax.while_loop(_sb_continue, wbody, (window // tk - 1, cmin0))

    n_old = (past - window) // tk
    if n_old > 0:
        def obody(carry):
            j, _ = carry
            keys = pl.ds(pl.multiple_of(j * tk, tk), tk)
            ck = pltpu.make_async_copy(kc_hbm.at[0, b, pl.ds(2 * hp, 2), :, keys], kbuf, sem.at[0])
            cv = pltpu.make_async_copy(vc_hbm.at[0, b, pl.ds(2 * hp, 2), :, keys], vbuf, sem.at[1])
            ck.start()
            cv.start()
            ck.wait()
            cv.wait()
            kpos = j * tk + lax.broadcasted_iota(jnp.int32, (1, tk), 1)
            cmin = _sb_step(qs, [pair_t(kbuf[...])], [pair_t(vbuf[...])], qpos, kpos, acc_ref, c_ref, kv_t=True)
            return j - 1, cmin

        lax.while_loop(_sb_continue, obody, (n_old - 1, cmin1))

    o_ref[0] = _sb_finish(acc_ref[0], nsb_ref[...]).astype(o_ref.dtype)


def _sb_sample(q, k_new, v_new, k_cache_t, v_cache_t, nsb_row):
    b, t, _ = q.shape
    past = k_cache_t.shape[4]
    window = min(SB_CACHE_WINDOW, past)
    tk = min(TK_SB, window)
    assert past % window == 0 and window % tk == 0 and (past - window) % tk == 0
    new_spec = pl.BlockSpec((1, t, LANES), lambda bb, hp: (bb, 0, hp))
    win_spec = pl.BlockSpec((None, None, 2, SB_HEAD_DIM, window), lambda bb, hp: (0, bb, hp, 0, past // window - 1))
    any_spec = pl.BlockSpec(memory_space=pl.ANY)
    return pl.pallas_call(
        functools.partial(_sb_sample_kernel, past=past, window=window, tk=tk),
        grid=(b, SB_PAIRS),
        in_specs=[new_spec, new_spec, new_spec, win_spec, win_spec, any_spec, any_spec,
                  pl.BlockSpec((1, LANES), lambda bb, hp: (0, 0))],
        out_specs=new_spec,
        out_shape=jax.ShapeDtypeStruct((b, t, SB_WIDTH), BF16),
        scratch_shapes=[pltpu.VMEM((1, t, LANES), F32), pltpu.VMEM((2, t, LANES), F32),
                        pltpu.VMEM((2, SB_HEAD_DIM, tk), F32), pltpu.VMEM((2, SB_HEAD_DIM, tk), F32),
                        pltpu.SemaphoreType.DMA((2,))],
        compiler_params=_cparams(("arbitrary", "arbitrary")),
        name="sb_sample",
    )(q, k_new, v_new, k_cache_t, v_cache_t, k_cache_t, v_cache_t, nsb_row)


def _gdn_chunk(nb, qkv_ref, gate_ref, ab_ref, cw_ref, alog_ref, dtb_ref, ng_ref, o_ref, xp_ref, s_ref, chunk):
    ii = lax.broadcasted_iota(jnp.int32, (chunk, chunk), 0)
    jj = lax.broadcasted_iota(jnp.int32, (chunk, chunk), 1)
    tri = jnp.where(ii >= jj, 1.0, 0.0).astype(F32)
    scale = GDN_HEAD_DIM ** -0.5
    hd = GDN_HEAD_DIM
    n_fac = int(math.log2(chunk)) - 1
    chains = [(bi, h) for bi in range(nb) for h in range(GDN_HEADS)]

    ys, gcs, gcts, betas = [], [], [], []
    for bi in range(nb):
        xp_ref[bi, 8:8 + chunk, :] = qkv_ref[bi]
        y = cw_ref[0:1, :] * xp_ref[bi, 5:5 + chunk, :]
        for i in range(1, CONV_WIDTH):
            y = y + cw_ref[i:i + 1, :] * xp_ref[bi, 5 + i:5 + i + chunk, :]
        ys.append(y * _sigmoid(y))
        xp_ref[bi, 0:8, :] = xp_ref[bi, chunk:chunk + 8, :]
        ab = ab_ref[bi]
        g_tile = -jnp.exp(alog_ref[...]) * _softplus(ab + dtb_ref[...])
        betas.append(_sigmoid(ab))
        gc_tile = jnp.dot(tri, g_tile, preferred_element_type=F32, precision=lax.Precision.HIGHEST)
        gcs.append(gc_tile)
        gcts.append(gc_tile.T)

    kn, qs, kb, vb, decay, egc, glast, gcol = {}, {}, {}, {}, {}, {}, {}, {}
    for ch in chains:
        bi, h = ch
        y = ys[bi]
        qh = y[:, h * hd:(h + 1) * hd]
        kh = y[:, GDN_WIDTH + h * hd:GDN_WIDTH + (h + 1) * hd]
        vh = y[:, 2 * GDN_WIDTH + h * hd:2 * GDN_WIDTH + (h + 1) * hd]
        qs[ch] = qh * lax.rsqrt(jnp.sum(qh * qh, axis=-1, keepdims=True) + L2_EPS) * scale
        kn[ch] = kh * lax.rsqrt(jnp.sum(kh * kh, axis=-1, keepdims=True) + L2_EPS)
        gcol[ch] = gcs[bi][:, h:h + 1]
        grow = gcts[bi][h:h + 1, :]
        glast[ch] = gcs[bi][chunk - 1:chunk, h:h + 1]
        bcol = betas[bi][:, GDN_HEADS + h:GDN_HEADS + h + 1]
        decay[ch] = jnp.exp(jnp.where(ii >= jj, gcol[ch] - grow, -1e30))
        egc[ch] = jnp.exp(gcol[ch])
        kb[ch] = kn[ch] * bcol
        vb[ch] = vh * bcol

    a2 = {ch: lax.dot_general(jnp.concatenate([kb[ch], qs[ch]], axis=0).astype(BF16), kn[ch].astype(BF16),
                              (((1,), (1,)), ((), ())), preferred_element_type=F32) for ch in chains}
    lb = {ch: jnp.where(ii > jj, a2[ch][:chunk] * decay[ch], 0.0).astype(BF16) for ch in chains}
    qk = {ch: (a2[ch][chunk:] * decay[ch]).astype(BF16) for ch in chains}

    rhs = {ch: jnp.concatenate([vb[ch], kb[ch] * egc[ch]], axis=1) for ch in chains}
    sol = {ch: rhs[ch] - jnp.dot(lb[ch], rhs[ch].astype(BF16), preferred_element_type=F32) for ch in chains}
    pw = {ch: jnp.dot(lb[ch], lb[ch], preferred_element_type=F32).astype(BF16) for ch in chains}
    for f in range(n_fac):
        sol = {ch: sol[ch] + jnp.dot(pw[ch], sol[ch].astype(BF16), preferred_element_type=F32) for ch in chains}
        if f + 1 < n_fac:
            pw = {ch: jnp.dot(pw[ch], pw[ch], preferred_element_type=F32).astype(BF16) for ch in chains}

    s_old = {ch: s_ref[ch[0], ch[1]] for ch in chains}
    ws = {ch: jnp.dot(jnp.concatenate([sol[ch][:, hd:], qs[ch] * egc[ch]], axis=0).astype(BF16),
                      s_old[ch].astype(BF16), preferred_element_type=F32) for ch in chains}
    v_new = {ch: (sol[ch][:, :hd] - ws[ch][:chunk]).astype(BF16) for ch in chains}
    o = {ch: ws[ch][chunk:] + jnp.dot(qk[ch], v_new[ch], preferred_element_type=F32) for ch in chains}
    for ch in chains:
        kd = (kn[ch] * jnp.exp(glast[ch] - gcol[ch])).astype(BF16)
        s_ref[ch[0], ch[1]] = s_old[ch] * jnp.exp(glast[ch]) + lax.dot_general(
            kd, v_new[ch], (((0,), (0,)), ((), ())), preferred_element_type=F32)

    for ch in chains:
        bi, h = ch
        sl = slice(h * hd, (h + 1) * hd)
        on = o[ch] * lax.rsqrt(jnp.mean(o[ch] * o[ch], axis=-1, keepdims=True) + NORM_EPS) * ng_ref[...]
        gt = gate_ref[bi][:, sl]
        o_ref[bi, :, sl] = (on * (gt * _sigmoid(gt))).astype(o_ref.dtype)


def _gdn_kernel(qkv_ref, gate_ref, ab_ref, cinit_ref, sinit_ref, cw_ref, alog_ref, dtb_ref, ng_ref,
                o_ref, cout_ref, sout_ref, xp_ref, s_ref, *, chunk, nb):
    c = pl.program_id(1)

    @pl.when(c == 0)
    def _():
        xp_ref[:, 0:8, :] = cinit_ref[...]
        s_ref[...] = sinit_ref[...]

    _gdn_chunk(nb, qkv_ref, gate_ref, ab_ref, cw_ref, alog_ref, dtb_ref, ng_ref, o_ref, xp_ref, s_ref, chunk)

    @pl.when(c == pl.num_programs(1) - 1)
    def _():
        cout_ref[...] = xp_ref[:, 0:8, :]
        sout_ref[...] = s_ref[...]


def _gdn(qkv, gate, ab, conv_init, s_init, cw8, alog_row, dtb_row, ng_row):
    b, l, _ = qkv.shape
    chunk = min(CHUNK, l)
    nc = l // chunk
    nb = min(GDN_BATCH_PER_STEP, b)
    assert b % nb == 0
    tok = lambda w: pl.BlockSpec((nb, chunk, w), lambda bb, c: (bb, c, 0))
    const = lambda shape: pl.BlockSpec(shape, lambda bb, c: (0,) * len(shape))
    per_b3 = pl.BlockSpec((nb, 8, CONV_CH), lambda bb, c: (bb, 0, 0))
    per_b4 = pl.BlockSpec((nb, GDN_HEADS, GDN_HEAD_DIM, GDN_HEAD_DIM), lambda bb, c: (bb, 0, 0, 0))
    return pl.pallas_call(
        functools.partial(_gdn_kernel, chunk=chunk, nb=nb),
        grid=(b // nb, nc),
        in_specs=[tok(CONV_CH), tok(GDN_WIDTH), tok(LANES), per_b3, per_b4,
                  const((8, CONV_CH)), const((1, LANES)), const((1, LANES)), const((1, LANES))],
        out_specs=(tok(GDN_WIDTH), per_b3, per_b4),
        out_shape=(jax.ShapeDtypeStruct((b, l, GDN_WIDTH), BF16),
                   jax.ShapeDtypeStruct((b, 8, CONV_CH), F32),
                   jax.ShapeDtypeStruct((b, GDN_HEADS, GDN_HEAD_DIM, GDN_HEAD_DIM), F32)),
        scratch_shapes=[pltpu.VMEM((nb, chunk + 8, CONV_CH), F32),
                        pltpu.VMEM((nb, GDN_HEADS, GDN_HEAD_DIM, GDN_HEAD_DIM), F32)],
        compiler_params=_cparams(("arbitrary", "arbitrary")),
        name="gdn",
    )(qkv, gate, ab, conv_init, s_init, cw8, alog_row, dtb_row, ng_row)


def _out_route_kernel(xp_ref, osbp_ref, ogdnp_ref, xs_ref, osbs_ref, ogdns_ref, wo_ref, nf_ref, rw_ref, rb_ref,
                      h_ref, xn_ref, route_ref, cnt_ref, run_ref, *, tm, steps_p):
    i = pl.program_id(0)

    @pl.when(i == 0)
    def _():
        run_ref[...] = jnp.zeros_like(run_ref)

    is_p = i < steps_p
    x = jnp.where(is_p, xp_ref[...], xs_ref[...])
    osb = jnp.where(is_p, osbp_ref[...], osbs_ref[...])
    ogdn = jnp.where(is_p, ogdnp_ref[...], ogdns_ref[...])
    h = (x
         + jnp.dot(osb, wo_ref[0:SB_WIDTH, :], preferred_element_type=F32)
         + jnp.dot(ogdn, wo_ref[SB_WIDTH:, :], preferred_element_type=F32))
    h_ref[...] = h
    xn = h * lax.rsqrt(jnp.mean(h * h, axis=-1, keepdims=True) + NORM_EPS) * nf_ref[...]
    _store_row_tiles(xn_ref, xn, tm)

    lane = lax.broadcasted_iota(jnp.int32, (1, LANES), 1)
    lane_f = lane.astype(F32)
    rw = rw_ref[...]
    xn_hi = xn.astype(BF16)
    xn_lo = (xn - xn_hi.astype(F32)).astype(BF16)
    rw_hi = rw.astype(BF16)
    rw_lo = (rw - rw_hi.astype(F32)).astype(BF16)
    logits = (jnp.dot(xn_hi, rw_hi, preferred_element_type=F32) + jnp.dot(xn_lo, rw_hi, preferred_element_type=F32)
              + jnp.dot(xn_hi, rw_lo, preferred_element_type=F32)) + rb_ref[...]
    logits = jnp.where(lane < N_EXPERTS, logits, -jnp.inf)
    vals, hots = [], []
    for _ in range(TOP_K):
        m = jnp.max(logits, axis=-1, keepdims=True)
        idx = jnp.min(jnp.where(logits == m, lane_f, float(LANES)), axis=-1, keepdims=True)
        hot = lane_f == idx
        logits = jnp.where(hot, -jnp.inf, logits)
        vals.append(m)
        hots.append((hot, idx))
    exps = [jnp.exp(v - vals[0]) for v in vals]
    denom = exps[0] + exps[1] + exps[2] + exps[3]

    multi = jnp.zeros((tm, LANES), F32)
    for hot, _ in hots:
        multi = jnp.where(hot, 1.0, multi)
    ii = lax.broadcasted_iota(jnp.int32, (tm, tm), 0)
    jj = lax.broadcasted_iota(jnp.int32, (tm, tm), 1)
    earlier = jnp.where(ii > jj, 1.0, 0.0).astype(BF16)
    run = run_ref[0:1, :]
    rank_all = jnp.dot(earlier, multi.astype(BF16), preferred_element_type=F32) + run
    run_new = run + jnp.sum(multi, axis=0, keepdims=True)
    run_ref[...] = jnp.broadcast_to(run_new, run_ref.shape)
    cnt_ref[...] = jnp.broadcast_to(run_new, cnt_ref.shape)

    route = jnp.zeros((tm, LANES), F32)
    for k, (hot, idx) in enumerate(hots):
        rank = jnp.sum(jnp.where(hot, rank_all, 0.0), axis=-1, keepdims=True)
        route = jnp.where(lane == k, idx, route)
        route = jnp.where(lane == TOP_K + k, rank, route)
        route = jnp.where(lane == 2 * TOP_K + k, exps[k] / denom, route)
    route_ref[...] = route


def _out_route(xp, osbp, ogdnp, xs, osbs, ogdns, wo_bf, nf_row, rw_pad, rb_row):
    n_p, n_s = xp.shape[0], xs.shape[0]
    tm = min(TM_DENSE, n_p, n_s)
    steps_p, steps_s = n_p // tm, n_s // tm
    prow = lambda w: pl.BlockSpec((tm, w), lambda i: (jnp.minimum(i, steps_p - 1), 0))
    srow = lambda w: pl.BlockSpec((tm, w), lambda i: (jnp.maximum(i - steps_p, 0), 0))
    orow = lambda w: pl.BlockSpec((tm, w), lambda i: (i, 0))
    const = lambda shape: pl.BlockSpec(shape, lambda i: (0,) * len(shape))
    n_total = n_p + n_s
    out_shape = (jax.ShapeDtypeStruct((n_total, D_MODEL), F32),
                 jax.ShapeDtypeStruct((n_total * ROW_TILE, LANES), F32),
                 jax.ShapeDtypeStruct((n_total, LANES), F32),
                 jax.ShapeDtypeStruct((8, LANES), F32))
    return pl.pallas_call(
        functools.partial(_out_route_kernel, tm=tm, steps_p=steps_p),
        grid=(steps_p + steps_s,),
        in_specs=[prow(D_MODEL), prow(SB_WIDTH), prow(GDN_WIDTH), srow(D_MODEL), srow(SB_WIDTH), srow(GDN_WIDTH),
                  const((D_MODEL, D_MODEL)), const((1, D_MODEL)), const((D_MODEL, LANES)), const((1, LANES))],
        out_specs=(orow(D_MODEL), pl.BlockSpec((tm * ROW_TILE, LANES), lambda i: (i, 0)), orow(LANES),
                   const((8, LANES))),
        out_shape=out_shape,
        scratch_shapes=[pltpu.VMEM((8, LANES), F32)],
        compiler_params=_cparams(("arbitrary",)),
        name="out_route",
    )(xp, osbp, ogdnp, xs, osbs, ogdns, wo_bf, nf_row, rw_pad, rb_row)


def _scatter_rows_kernel(pos_ref, x_ref, xs_hbm, sem, *, tm):
    def row_copy(t, p):
        return pltpu.make_async_copy(x_ref.at[pl.ds(pl.multiple_of(t * ROW_TILE, ROW_TILE), ROW_TILE)],
                                     xs_hbm.at[pl.ds(pl.multiple_of(p * ROW_TILE, ROW_TILE), ROW_TILE)], sem)

    def issue(t, carry):
        for k in range(TOP_K):
            row_copy(t, pos_ref[t * TOP_K + k]).start(priority=k % 2)
        return carry

    lax.fori_loop(0, tm, issue, 0)

    def drain(t, carry):
        for k in range(TOP_K):
            row_copy(0, 0).wait()
        return carry

    lax.fori_loop(0, tm, drain, 0)


def _scatter_rows(pos_flat, xn):
    n = xn.shape[0] // ROW_TILE
    tm = min(TM_ROWS, n)
    return pl.pallas_call(
        functools.partial(_scatter_rows_kernel, tm=tm),
        grid=(n // tm,),
        in_specs=[pl.BlockSpec((tm * TOP_K,), lambda i: (i,), memory_space=pltpu.SMEM),
                  pl.BlockSpec((tm * ROW_TILE, LANES), lambda i: (i, 0))],
        out_specs=pl.BlockSpec(memory_space=pl.ANY),
        out_shape=jax.ShapeDtypeStruct((n * TOP_K * ROW_TILE, LANES), xn.dtype),
        scratch_shapes=[pltpu.SemaphoreType.DMA(())],
        compiler_params=_cparams(("arbitrary",)),
        name="moe_scatter",
    )(pos_flat, xn)


def _experts_kernel(tile_ref, exp_ref, lo_ref, hi_ref, first_ref, newexp_ref, nvalid_ref,
                    x_ref, wg_ref, bg_ref, wu_ref, bu_ref, wd_ref, bd_ref, y_ref, wg_bf, wu_bf, wd_bf, *, tmg):
    v = pl.program_id(0)

    @pl.when(jnp.logical_and(v < nvalid_ref[0], newexp_ref[v] == 1))
    def _():
        wg_bf[...] = wg_ref[0].astype(BF16)
        wu_bf[...] = wu_ref[0].astype(BF16)
        wd_bf[...] = wd_ref[0].astype(BF16)

    @pl.when(v < nvalid_ref[0])
    def _():
        x = _load_row_tiles(x_ref, tmg).astype(BF16)
        gt = jnp.minimum(jnp.dot(x, wg_bf[...], preferred_element_type=F32) + bg_ref[0], SWIGLU_LIMIT)
        up = jnp.clip(jnp.dot(x, wu_bf[...], preferred_element_type=F32) + bu_ref[0], -SWIGLU_LIMIT, SWIGLU_LIMIT)
        hid = (up + 1.0) * (gt * _sigmoid(SWIGLU_ALPHA * gt))
        y = jnp.dot(hid.astype(BF16), wd_bf[...], preferred_element_type=F32) + bd_ref[0]
        rows = tile_ref[v] * tmg + lax.broadcasted_iota(jnp.int32, (tmg, 1), 0)
        mine = jnp.logical_and(rows >= lo_ref[v], rows < hi_ref[v])

        @pl.when(first_ref[v] == 1)
        def _():
            _store_row_tiles(y_ref, jnp.where(mine, y, 0.0), tmg)

        @pl.when(first_ref[v] == 0)
        def _():
            _store_row_tiles(y_ref, jnp.where(mine, y, _load_row_tiles(y_ref, tmg)), tmg)


def _experts(meta, xs, wg, bg, wu, bu, wd, bd):
    n4 = xs.shape[0] // ROW_TILE
    tmg = min(TM_GROUP, n4)
    n_visits = meta[0].shape[0]
    xspec = pl.BlockSpec((tmg * ROW_TILE, LANES), lambda v, tile, exp, *_: (tile[v], 0))
    wspec = pl.BlockSpec((1, D_MODEL, D_MODEL), lambda v, tile, exp, *_: (exp[v], 0, 0))
    bspec = pl.BlockSpec((1, 1, D_MODEL), lambda v, tile, exp, *_: (exp[v], 0, 0))
    grid_spec = pltpu.PrefetchScalarGridSpec(
        num_scalar_prefetch=len(meta),
        grid=(n_visits,),
        in_specs=[xspec, wspec, bspec, wspec, bspec, wspec, bspec],
        out_specs=xspec,
        scratch_shapes=[pltpu.VMEM((D_MODEL, D_MODEL), BF16)] * 3,
    )
    return pl.pallas_call(
        functools.partial(_experts_kernel, tmg=tmg),
        grid_spec=grid_spec,
        out_shape=jax.ShapeDtypeStruct((n4 * ROW_TILE, LANES), F32),
        compiler_params=_cparams(("arbitrary",), VMEM_LIMIT_EXPERTS),
        name="moe_experts",
    )(*meta, xs, wg, bg, wu, bu, wd, bd)


def _combine_kernel(pos_ref, h_ref, route_ref, nfin_ref, ys_hbm, o_ref, buf, sem, *, tm):
    def row_copy(t, k, p):
        return pltpu.make_async_copy(ys_hbm.at[pl.ds(pl.multiple_of(p * ROW_TILE, ROW_TILE), ROW_TILE)],
                                     buf.at[k, pl.ds(pl.multiple_of(t * ROW_TILE, ROW_TILE), ROW_TILE)], sem)

    def issue(t, carry):
        for k in range(TOP_K):
            row_copy(t, k, pos_ref[t * TOP_K + k]).start(priority=k % 2)
        return carry

    lax.fori_loop(0, tm, issue, 0)

    def drain(t, carry):
        for k in range(TOP_K):
            row_copy(0, k, 0).wait()
        return carry

    lax.fori_loop(0, tm, drain, 0)

    route = route_ref[...]
    h = h_ref[...]
    pieces = []
    for j in range(ROW_TILE):
        piece = h[:, j * LANES:(j + 1) * LANES]
        for k in range(TOP_K):
            piece = piece + route[:, 2 * TOP_K + k:2 * TOP_K + k + 1] * buf[k, pl.ds(j, tm, stride=ROW_TILE), :]
        pieces.append(piece)
    out = jnp.concatenate(pieces, axis=-1)
    o_ref[...] = out * lax.rsqrt(jnp.mean(out * out, axis=-1, keepdims=True) + NORM_EPS) * nfin_ref[...]


def _combine(pos_flat, h, route, nfin_row, ys, row0, n):
    tm = min(TM_ROWS, n)
    off = row0 // tm
    return pl.pallas_call(
        functools.partial(_combine_kernel, tm=tm),
        grid=(n // tm,),
        in_specs=[pl.BlockSpec((tm * TOP_K,), lambda i: (i + off,), memory_space=pltpu.SMEM),
                  pl.BlockSpec((tm, D_MODEL), lambda i: (i + off, 0)),
                  pl.BlockSpec((tm, LANES), lambda i: (i + off, 0)),
                  pl.BlockSpec((1, D_MODEL), lambda i: (0, 0)),
                  pl.BlockSpec(memory_space=pl.ANY)],
        out_specs=pl.BlockSpec((tm, D_MODEL), lambda i: (i, 0)),
        out_shape=jax.ShapeDtypeStruct((n, D_MODEL), F32),
        scratch_shapes=[pltpu.VMEM((TOP_K, tm * ROW_TILE, LANES), F32), pltpu.SemaphoreType.DMA(())],
        compiler_params=_cparams(("arbitrary",)),
        name="moe_combine",
    )(pos_flat, h, route, nfin_row, ys)


def _group_plan(counts, n4, tmg):
    n_tiles = n4 // tmg
    n_visits = n_tiles + N_EXPERTS - 1
    ends = jnp.cumsum(counts)
    starts = ends - counts
    t_first = starts // tmg
    t_cnt = jnp.where(counts > 0, (ends - 1) // tmg - t_first + 1, 0)
    v_end = jnp.cumsum(t_cnt)
    v_start = v_end - t_cnt
    total = v_end[-1]
    v = jnp.arange(n_visits, dtype=jnp.int32)
    g = jnp.minimum(jnp.sum((v[:, None] >= v_end[None, :]).astype(jnp.int32), axis=1), N_EXPERTS - 1)
    g_last = jnp.max(jnp.where(counts > 0, jnp.arange(N_EXPERTS), 0)).astype(jnp.int32)
    valid = v < total
    g = jnp.where(valid, g, g_last)
    tile = jnp.where(valid, t_first[g] + (v - v_start[g]), n_tiles - 1).astype(jnp.int32)
    lo = jnp.where(valid, jnp.maximum(starts[g], tile * tmg), 0).astype(jnp.int32)
    hi = jnp.where(valid, jnp.minimum(ends[g], (tile + 1) * tmg), 0).astype(jnp.int32)
    prev_tile = jnp.concatenate([jnp.full((1,), -1, jnp.int32), tile[:-1]])
    first = (tile != prev_tile).astype(jnp.int32)
    prev_g = jnp.concatenate([jnp.full((1,), -1, jnp.int32), g[:-1]])
    newexp = (g != prev_g).astype(jnp.int32)
    meta = (tile, g, lo, hi, first, newexp, total.reshape(1).astype(jnp.int32))
    return meta, starts


def _pad_rows(a, rows):
    return jnp.concatenate([a, jnp.zeros((rows - a.shape[0],) + a.shape[1:], a.dtype)], axis=0)


def kernel(x_prompt, x_sample, cache_sb_k, cache_sb_v, cache_gdn_conv, state_gdn, norm_mix, w_in, conv_w, a_log,
           dt_bias, norm_sb, norm_gdn, w_out, norm_ffn, router_w, router_b, w_gate, b_gate, w_up, b_up, w_down,
           b_down, norm_final):
    bp, lp, _ = x_prompt.shape
    bs, ls, _ = x_sample.shape
    past = cache_sb_k.shape[2]
    n_p, n_s = bp * lp, bs * ls
    n_tot = n_p + n_s
    assert norm_mix.shape[0] == 1, "single-layer trunk"
    tm_min = min(TM_DENSE, TM_ROWS, n_p, n_s)
    assert n_p % tm_min == 0 and n_s % tm_min == 0 and n_p % min(TM_ROWS, n_s) == 0
    assert (n_tot * TOP_K) % TM_GROUP == 0 and n_tot % TM_ROWS == 0
    assert lp % min(TQ_SB, lp) == 0 and lp % min(CHUNK, lp) == 0 and ls % min(CHUNK, ls) == 0 and ls >= 8

    w_in_bf = jnp.pad(w_in[0], ((0, 0), (0, IN_WIDTH_PAD - IN_WIDTH))).astype(BF16)
    nmix_row = norm_mix[0].reshape(1, D_MODEL)
    nsb_row = jnp.tile(norm_sb[0], 2).reshape(1, LANES)
    ng_row = norm_gdn[0].reshape(1, LANES)
    cw8 = _pad_rows(conv_w[0], 8)
    alog_row = jnp.pad(a_log[0], (0, LANES - GDN_HEADS)).reshape(1, LANES)
    dtb_row = jnp.pad(dt_bias[0], (0, LANES - GDN_HEADS)).reshape(1, LANES)
    wo_bf = w_out[0].astype(BF16)
    nf_row = norm_ffn[0].reshape(1, D_MODEL)
    rw_pad = jnp.pad(router_w[0], ((0, 0), (0, LANES - N_EXPERTS)))
    rb_row = jnp.pad(router_b[0], (0, LANES - N_EXPERTS)).reshape(1, LANES)
    nfin_row = norm_final.reshape(1, D_MODEL)

    xp2 = x_prompt.reshape(n_p, D_MODEL)
    xs2 = x_sample.reshape(n_s, D_MODEL)

    qsb, k_prompt, v_prompt, kbf, vbf, gdn_in, gate, ab = _in_proj(xp2, nmix_row, w_in_bf)
    r3 = lambda a, b, l: a.reshape(b, l, a.shape[-1])
    osb_p = _sb_prompt(r3(qsb, bp, lp), r3(kbf, bp, lp), r3(vbf, bp, lp), nsb_row)
    ogdn_p, conv_p, state_p = _gdn(
        r3(gdn_in, bp, lp), r3(gate, bp, lp), r3(ab, bp, lp),
        jnp.zeros((bp, 8, CONV_CH), F32), jnp.zeros((bp, GDN_HEADS, GDN_HEAD_DIM, GDN_HEAD_DIM), F32),
        cw8, alog_row, dtb_row, ng_row)

    qsb, k_sample, v_sample, kbf, vbf, gdn_in, gate, ab = _in_proj(xs2, nmix_row, w_in_bf)
    keys_minor = lambda cache: jnp.transpose(cache, (0, 1, 3, 4, 2))
    osb_s = _sb_sample(r3(qsb, bs, ls), r3(kbf, bs, ls), r3(vbf, bs, ls), keys_minor(cache_sb_k),
                       keys_minor(cache_sb_v), nsb_row)
    conv_init = jnp.concatenate([jnp.zeros((bs, 8 - (CONV_WIDTH - 1), CONV_CH), F32), cache_gdn_conv[0]], axis=1)
    ogdn_s, conv_s, state_s = _gdn(r3(gdn_in, bs, ls), r3(gate, bs, ls), r3(ab, bs, ls), conv_init, state_gdn[0],
                                   cw8, alog_row, dtb_row, ng_row)

    h_buf, xn_buf, route_buf, cnt = _out_route(
        xp2, osb_p.reshape(n_p, SB_WIDTH), ogdn_p.reshape(n_p, GDN_WIDTH),
        xs2, osb_s.reshape(n_s, SB_WIDTH), ogdn_s.reshape(n_s, GDN_WIDTH), wo_bf, nf_row, rw_pad, rb_row)

    counts = cnt[0, :N_EXPERTS].astype(jnp.int32)
    n4 = n_tot * TOP_K
    tmg = min(TM_GROUP, n4)
    meta, starts = _group_plan(counts, n4, tmg)
    idx = route_buf[:, 0:TOP_K].astype(jnp.int32)
    rank = route_buf[:, TOP_K:2 * TOP_K].astype(jnp.int32)
    pos_flat = (starts[idx] + rank).reshape(n4)

    xs_sorted = _scatter_rows(pos_flat, xn_buf)
    e3 = lambda bias: bias[0].reshape(N_EXPERTS, 1, D_MODEL)
    ys_sorted = _experts(meta, xs_sorted, w_gate[0], e3(b_gate), w_up[0], e3(b_up), w_down[0], e3(b_down))
    y_prompt = _combine(pos_flat, h_buf, route_buf, nfin_row, ys_sorted, 0, n_p).reshape(bp, lp, D_MODEL)
    y_sample = _combine(pos_flat, h_buf, route_buf, nfin_row, ys_sorted, n_p, n_s).reshape(bs, ls, D_MODEL)

    heads = lambda a, b, l: a.reshape(1, b, l, SB_HEADS, SB_HEAD_DIM)
    return (y_prompt, y_sample,
            heads(k_prompt, bp, lp), heads(v_prompt, bp, lp),
            conv_p[:, 8 - (CONV_WIDTH - 1):][None], state_p[None],
            heads(k_sample, bs, ls), heads(v_sample, bs, ls),
            conv_s[:, 8 - (CONV_WIDTH - 1):][None], state_s[None])
```

```python
import functools
import math

import jax
import jax.numpy as jnp
from jax import lax
from jax.experimental import pallas as pl
from jax.experimental.pallas import tpu as pltpu

F32 = jnp.float32
BF16 = jnp.bfloat16

D_MODEL = 1024
SB_HEAD_DIM = 64
SB_WIDTH = 512
SB_HEADS = SB_WIDTH // SB_HEAD_DIM
SB_PAIRS = SB_WIDTH // 128
GDN_HEAD_DIM = 128
GDN_HEADS = 4
GDN_WIDTH = 512
CONV_WIDTH = 4
CONV_CH = 3 * GDN_WIDTH
IN_WIDTH = 3 * SB_WIDTH + 4 * GDN_WIDTH + 2 * GDN_HEADS
IN_WIDTH_PAD = 3 * SB_WIDTH + 4 * GDN_WIDTH + 128
N_EXPERTS = 32
TOP_K = 4
SWIGLU_LIMIT = 7.0
SWIGLU_ALPHA = 1.702
NORM_EPS = 1e-6
L2_EPS = 1e-6
CHUNK = 64

LANES = 128
VMEM_LIMIT_BYTES = 48 * 1024 * 1024
VMEM_LIMIT_EXPERTS = 56 * 1024 * 1024
SB_LOG_CUTOFF = 104.0

TM_DENSE = 256
TQ_SB = 256
TK_SB = 256
SB_PAIRS_PER_STEP = 4
SB_CACHE_WINDOW = 512
GDN_BATCH_PER_STEP = 4
TM_ROWS = 256
TM_GROUP = 512
EXPERT_SUBTILES = 2


def _cparams(sem, limit=VMEM_LIMIT_BYTES):
    return pltpu.CompilerParams(dimension_semantics=sem, vmem_limit_bytes=limit)


def _softplus(z):
    return jnp.maximum(z, 0.0) + jnp.log(1.0 + jnp.exp(-jnp.abs(z)))


def _sigmoid(z):
    return 1.0 / (1.0 + jnp.exp(-z))


def _store_heads(ref, val, tm):
    for h in range(SB_HEADS):
        ref[pl.ds(h, tm, stride=SB_HEADS), :] = val[:, h * SB_HEAD_DIM:(h + 1) * SB_HEAD_DIM]


ROW_TILE = D_MODEL // LANES


def _store_row_tiles(ref, val, tm, t0=0):
    for j in range(ROW_TILE):
        ref[pl.ds(t0 * ROW_TILE + j, tm, stride=ROW_TILE), :] = val[:, j * LANES:(j + 1) * LANES]


def _load_row_tiles(ref, tm, t0=0):
    return jnp.concatenate([ref[pl.ds(t0 * ROW_TILE + j, tm, stride=ROW_TILE), :] for j in range(ROW_TILE)],
                           axis=-1)


def _in_proj_kernel(x_ref, g_ref, w_ref, qsb_ref, ksb_ref, vsb_ref, kbf_ref, vbf_ref, gdn_ref, gate_ref, ab_ref,
                    *, tm):
    x = x_ref[...]
    xn = x * lax.rsqrt(jnp.mean(x * x, axis=-1, keepdims=True) + NORM_EPS) * g_ref[...]
    xn = xn.astype(BF16)

    def mm(lo, hi):
        return jnp.dot(xn, w_ref[:, lo:hi], preferred_element_type=F32)

    qsb_ref[...] = (mm(0, 512) * (SB_HEAD_DIM ** -0.5)).astype(BF16)
    k = mm(512, 1024)
    _store_heads(ksb_ref, k, tm)
    kbf_ref[...] = k.astype(BF16)
    v = mm(1024, 1536)
    _store_heads(vsb_ref, v, tm)
    vbf_ref[...] = v.astype(BF16)
    for j in range(3):
        gdn_ref[:, j * 512:(j + 1) * 512] = mm(1536 + j * 512, 2048 + j * 512)
    gate_ref[...] = mm(3072, 3584)
    ab_ref[...] = mm(3584, 3712)


def _in_proj(x2d, g_row, w_bf):
    n = x2d.shape[0]
    tm = min(TM_DENSE, n)
    row = lambda w: pl.BlockSpec((tm, w), lambda i: (i, 0))
    head_rows = pl.BlockSpec((tm * SB_HEADS, SB_HEAD_DIM), lambda i: (i, 0))
    out_shape = (
        jax.ShapeDtypeStruct((n, SB_WIDTH), BF16),
        jax.ShapeDtypeStruct((n * SB_HEADS, SB_HEAD_DIM), F32),
        jax.ShapeDtypeStruct((n * SB_HEADS, SB_HEAD_DIM), F32),
        jax.ShapeDtypeStruct((n, SB_WIDTH), BF16),
        jax.ShapeDtypeStruct((n, SB_WIDTH), BF16),
        jax.ShapeDtypeStruct((n, CONV_CH), F32),
        jax.ShapeDtypeStruct((n, GDN_WIDTH), F32),
        jax.ShapeDtypeStruct((n, LANES), F32),
    )
    return pl.pallas_call(
        functools.partial(_in_proj_kernel, tm=tm),
        grid=(n // tm,),
        in_specs=[row(D_MODEL), pl.BlockSpec((1, D_MODEL), lambda i: (0, 0)),
                  pl.BlockSpec((D_MODEL, IN_WIDTH_PAD), lambda i: (0, 0))],
        out_specs=(row(SB_WIDTH), head_rows, head_rows, row(SB_WIDTH), row(SB_WIDTH),
                   row(CONV_CH), row(GDN_WIDTH), row(LANES)),
        out_shape=out_shape,
        compiler_params=_cparams(("arbitrary",)),
        name="in_proj",
    )(x2d, g_row, w_bf)


def _sb_step(qs, k_blks, v_blks, qpos, kpos, acc_ref, c_ref, kv_t=False):
    pairs = len(qs)
    chains = [(p, h) for p in range(pairs) for h in range(2)]
    tk = k_blks[0].shape[1 if kv_t else 0]
    lane = lax.broadcasted_iota(jnp.int32, (1, LANES), 1)
    visible = kpos < qpos
    jj = lax.broadcasted_iota(jnp.int32, (tk, tk), 0)
    ss = lax.broadcasted_iota(jnp.int32, (tk, tk), 1)
    tri = jnp.where(jj >= ss, 1.0, 0.0).astype(BF16)
    nt = (((1,), (1,)), ((), ()))

    if kv_t:
        z = {ch: jnp.dot(qs[ch[0]][ch[1]], k_blks[ch[0]], preferred_element_type=F32) for ch in chains}
    else:
        z = {ch: lax.dot_general(qs[ch[0]][ch[1]], k_blks[ch[0]], nt, preferred_element_type=F32) for ch in chains}
    sp = {ch: jnp.where(visible, _softplus(z[ch]), 0.0) for ch in chains}
    sp_hi = {ch: sp[ch].astype(BF16) for ch in chains}
    sp_lo = {ch: (sp[ch] - sp_hi[ch].astype(F32)).astype(BF16) for ch in chains}
    r = {ch: jnp.dot(sp_hi[ch], tri, preferred_element_type=F32) + jnp.dot(sp_lo[ch], tri, preferred_element_type=F32)
         for ch in chains}
    def mass(ch):
        c = c_ref[2 * ch[0] + ch[1]]
        return c[:, :tk] if tk <= LANES else jnp.concatenate([c] * (tk // LANES), axis=-1)

    w = {ch: jnp.where(visible, jnp.exp(z[ch] - r[ch] - mass(ch)), 0.0).astype(BF16) for ch in chains}
    if kv_t:
        pv = {ch: lax.dot_general(w[ch], v_blks[ch[0]], nt, preferred_element_type=F32) for ch in chains}
    else:
        pv = {ch: jnp.dot(w[ch], v_blks[ch[0]], preferred_element_type=F32) for ch in chains}
    cmin = None
    for p in range(pairs):
        acc_ref[p] += jnp.where(lane < SB_HEAD_DIM, pv[(p, 0)], pv[(p, 1)])
    for ch in chains:
        c_new = c_ref[2 * ch[0] + ch[1]] + r[ch][:, 0:1]
        c_ref[2 * ch[0] + ch[1]] = c_new
        m = jnp.min(c_new)
        cmin = m if cmin is None else jnp.minimum(cmin, m)
    return cmin


def _sb_finish(acc, nsb_row):
    lane = lax.broadcasted_iota(jnp.int32, (1, LANES), 1)
    first = lane < SB_HEAD_DIM
    sq = acc * acc
    s_all = jnp.sum(sq, axis=-1, keepdims=True)
    s0 = jnp.sum(jnp.where(first, sq, 0.0), axis=-1, keepdims=True)
    ms = jnp.where(first, s0, s_all - s0) * (1.0 / SB_HEAD_DIM)
    return acc * lax.rsqrt(ms + NORM_EPS) * nsb_row


def _split_heads(q):
    lane = lax.broadcasted_iota(jnp.int32, (1, LANES), 1)
    zero = jnp.zeros_like(q)
    return jnp.where(lane < SB_HEAD_DIM, q, zero), jnp.where(lane >= SB_HEAD_DIM, q, zero)


def _sb_continue(carry):
    j, cmin = carry
    return jnp.logical_and(j >= 0, cmin < SB_LOG_CUTOFF)


def _sb_prompt_kernel(q_ref, k_ref, v_ref, nsb_ref, o_ref, acc_ref, c_ref, *, tq, tk, pairs):
    i = pl.program_id(2)
    lanes = lambda p: slice(p * LANES, (p + 1) * LANES)
    qs = [_split_heads(q_ref[0, :, lanes(p)]) for p in range(pairs)]
    acc_ref[...] = jnp.zeros_like(acc_ref)
    c_ref[...] = jnp.zeros_like(c_ref)
    qpos = i * tq + lax.broadcasted_iota(jnp.int32, (tq, 1), 0)

    def body(carry):
        j, _ = carry
        ks = pl.multiple_of(j * tk, tk)
        kpos = ks + lax.broadcasted_iota(jnp.int32, (1, tk), 1)
        k_blks = [k_ref[0, pl.ds(ks, tk), lanes(p)] for p in range(pairs)]
        v_blks = [v_ref[0, pl.ds(ks, tk), lanes(p)] for p in range(pairs)]
        return j - 1, _sb_step(qs, k_blks, v_blks, qpos, kpos, acc_ref, c_ref)

    lax.while_loop(_sb_continue, body, ((i + 1) * (tq // tk) - 1, jnp.float32(0.0)))
    for p in range(pairs):
        o_ref[0, :, lanes(p)] = _sb_finish(acc_ref[p], nsb_ref[...]).astype(o_ref.dtype)


def _sb_prompt(q, k, v, nsb_row):
    b, l, _ = q.shape
    tq = min(TQ_SB, l)
    tk = min(TK_SB, tq)
    pairs = SB_PAIRS_PER_STEP
    width = pairs * LANES
    qspec = pl.BlockSpec((1, tq, width), lambda bb, hp, i: (bb, i, hp))
    kvspec = pl.BlockSpec((1, l, width), lambda bb, hp, i: (bb, 0, hp))
    return pl.pallas_call(
        functools.partial(_sb_prompt_kernel, tq=tq, tk=tk, pairs=pairs),
        grid=(b, SB_PAIRS // pairs, l // tq),
        in_specs=[qspec, kvspec, kvspec, pl.BlockSpec((1, LANES), lambda bb, hp, i: (0, 0))],
        out_specs=qspec,
        out_shape=jax.ShapeDtypeStruct((b, l, SB_WIDTH), BF16),
        scratch_shapes=[pltpu.VMEM((pairs, tq, LANES), F32), pltpu.VMEM((2 * pairs, tq, LANES), F32)],
        compiler_params=_cparams(("arbitrary", "arbitrary", "arbitrary")),
        name="sb_prompt",
    )(q, k, v, nsb_row)


def _sb_sample_kernel(q_ref, kn_ref, vn_ref, kw_ref, vw_ref, kc_hbm, vc_hbm, nsb_ref, o_ref,
                      acc_ref, c_ref, kbuf, vbuf, sem, *, past, window, tk):
    b = pl.program_id(0)
    hp = pl.program_id(1)
    t = q_ref.shape[1]
    qs = [_split_heads(q_ref[0])]
    acc_ref[...] = jnp.zeros_like(acc_ref)
    c_ref[...] = jnp.zeros_like(c_ref)
    qpos = past + lax.broadcasted_iota(jnp.int32, (t, 1), 0)

    kpos_new = past + lax.broadcasted_iota(jnp.int32, (1, t), 1)
    cmin0 = _sb_step(qs, [kn_ref[0]], [vn_ref[0]], qpos, kpos_new, acc_ref, c_ref)

    def pair_t(blk):
        return blk.reshape(LANES, tk).astype(BF16)

    def wbody(carry):
        j, _ = carry
        ws = pl.multiple_of(j * tk, tk)
        kpos = (past - window) + ws + lax.broadcasted_iota(jnp.int32, (1, tk), 1)
        cmin = _sb_step(qs, [pair_t(kw_ref[:, :, pl.ds(ws, tk)])], [pair_t(vw_ref[:, :, pl.ds(ws, tk)])],
                        qpos, kpos, acc_ref, c_ref, kv_t=True)
        return j - 1, cmin

    _, cmin1 = lax.while_loop(_sb_continue, wbody, (window // tk - 1, cmin0))

    n_old = (past - window) // tk
    if n_old > 0:
        def obody(carry):
            j, _ = carry
            keys = pl.ds(pl.multiple_of(j * tk, tk), tk)
            ck = pltpu.make_async_copy(kc_hbm.at[0, b, pl.ds(2 * hp, 2), :, keys], kbuf, sem.at[0])
            cv = pltpu.make_async_copy(vc_hbm.at[0, b, pl.ds(2 * hp, 2), :, keys], vbuf, sem.at[1])
            ck.start()
            cv.start()
            ck.wait()
            cv.wait()
            kpos = j * tk + lax.broadcasted_iota(jnp.int32, (1, tk), 1)
            cmin = _sb_step(qs, [pair_t(kbuf[...])], [pair_t(vbuf[...])], qpos, kpos, acc_ref, c_ref, kv_t=True)
            return j - 1, cmin

        lax.while_loop(_sb_continue, obody, (n_old - 1, cmin1))

    o_ref[0] = _sb_finish(acc_ref[0], nsb_ref[...]).astype(o_ref.dtype)


def _sb_sample(q, k_new, v_new, k_cache_t, v_cache_t, nsb_row):
    b, t, _ = q.shape
    past = k_cache_t.shape[4]
    window = min(SB_CACHE_WINDOW, past)
    tk = min(TK_SB, window)
    assert past % window == 0 and window % tk == 0 and (past - window) % tk == 0
    new_spec = pl.BlockSpec((1, t, LANES), lambda bb, hp: (bb, 0, hp))
    win_spec = pl.BlockSpec((None, None, 2, SB_HEAD_DIM, window), lambda bb, hp: (0, bb, hp, 0, past // window - 1))
    any_spec = pl.BlockSpec(memory_space=pl.ANY)
    return pl.pallas_call(
        functools.partial(_sb_sample_kernel, past=past, window=window, tk=tk),
        grid=(b, SB_PAIRS),
        in_specs=[new_spec, new_spec, new_spec, win_spec, win_spec, any_spec, any_spec,
                  pl.BlockSpec((1, LANES), lambda bb, hp: (0, 0))],
        out_specs=new_spec,
        out_shape=jax.ShapeDtypeStruct((b, t, SB_WIDTH), BF16),
        scratch_shapes=[pltpu.VMEM((1, t, LANES), F32), pltpu.VMEM((2, t, LANES), F32),
                        pltpu.VMEM((2, SB_HEAD_DIM, tk), F32), pltpu.VMEM((2, SB_HEAD_DIM, tk), F32),
                        pltpu.SemaphoreType.DMA((2,))],
        compiler_params=_cparams(("arbitrary", "arbitrary")),
        name="sb_sample",
    )(q, k_new, v_new, k_cache_t, v_cache_t, k_cache_t, v_cache_t, nsb_row)


def _gdn_chunk(nb, qkv_ref, gate_ref, ab_ref, cw_ref, alog_ref, dtb_ref, ng_ref, o_ref, xp_ref, s_ref, chunk):
    ii = lax.broadcasted_iota(jnp.int32, (chunk, chunk), 0)
    jj = lax.broadcasted_iota(jnp.int32, (chunk, chunk), 1)
    tri = jnp.where(ii >= jj, 1.0, 0.0).astype(F32)
    scale = GDN_HEAD_DIM ** -0.5
    hd = GDN_HEAD_DIM
    n_fac = int(math.log2(chunk)) - 1
    chains = [(bi, h) for bi in range(nb) for h in range(GDN_HEADS)]

    ys, gcs, gcts, betas = [], [], [], []
    for bi in range(nb):
        xp_ref[bi, 8:8 + chunk, :] = qkv_ref[bi]
        y = cw_ref[0:1, :] * xp_ref[bi, 5:5 + chunk, :]
        for i in range(1, CONV_WIDTH):
            y = y + cw_ref[i:i + 1, :] * xp_ref[bi, 5 + i:5 + i + chunk, :]
        ys.append(y * _sigmoid(y))
        xp_ref[bi, 0:8, :] = xp_ref[bi, chunk:chunk + 8, :]
        ab = ab_ref[bi]
        g_tile = -jnp.exp(alog_ref[...]) * _softplus(ab + dtb_ref[...])
        betas.append(_sigmoid(ab))
        gc_tile = jnp.dot(tri, g_tile, preferred_element_type=F32, precision=lax.Precision.HIGHEST)
        gcs.append(gc_tile)
        gcts.append(gc_tile.T)

    kn, qs, kb, vb, decay, egc, glast, gcol = {}, {}, {}, {}, {}, {}, {}, {}
    for ch in chains:
        bi, h = ch
        y = ys[bi]
        qh = y[:, h * hd:(h + 1) * hd]
        kh = y[:, GDN_WIDTH + h * hd:GDN_WIDTH + (h + 1) * hd]
        vh = y[:, 2 * GDN_WIDTH + h * hd:2 * GDN_WIDTH + (h + 1) * hd]
        qs[ch] = qh * lax.rsqrt(jnp.sum(qh * qh, axis=-1, keepdims=True) + L2_EPS) * scale
        kn[ch] = kh * lax.rsqrt(jnp.sum(kh * kh, axis=-1, keepdims=True) + L2_EPS)
        gcol[ch] = gcs[bi][:, h:h + 1]
        grow = gcts[bi][h:h + 1, :]
        glast[ch] = gcs[bi][chunk - 1:chunk, h:h + 1]
        bcol = betas[bi][:, GDN_HEADS + h:GDN_HEADS + h + 1]
        decay[ch] = jnp.exp(jnp.where(ii >= jj, gcol[ch] - grow, -1e30))
        egc[ch] = jnp.exp(gcol[ch])
        kb[ch] = kn[ch] * bcol
        vb[ch] = vh * bcol

    a2 = {ch: lax.dot_general(jnp.concatenate([kb[ch], qs[ch]], axis=0).astype(BF16), kn[ch].astype(BF16),
                              (((1,), (1,)), ((), ())), preferred_element_type=F32) for ch in chains}
    lb = {ch: jnp.where(ii > jj, a2[ch][:chunk] * decay[ch], 0.0).astype(BF16) for ch in chains}
    qk = {ch: (a2[ch][chunk:] * decay[ch]).astype(BF16) for ch in chains}

    rhs = {ch: jnp.concatenate([vb[ch], kb[ch] * egc[ch]], axis=1) for ch in chains}
    sol = {ch: rhs[ch] - jnp.dot(lb[ch], rhs[ch].astype(BF16), preferred_element_type=F32) for ch in chains}
    pw = {ch: jnp.dot(lb[ch], lb[ch], preferred_element_type=F32).astype(BF16) for ch in chains}
    for f in range(n_fac):
        sol = {ch: sol[ch] + jnp.dot(pw[ch], sol[ch].astype(BF16), preferred_element_type=F32) for ch in chains}
        if f + 1 < n_fac:
            pw = {ch: jnp.dot(pw[ch], pw[ch], preferred_element_type=F32).astype(BF16) for ch in chains}

    s_old = {ch: s_ref[ch[0], ch[1]] for ch in chains}
    ws = {ch: jnp.dot(jnp.concatenate([sol[ch][:, hd:], qs[ch] * egc[ch]], axis=0).astype(BF16),
                      s_old[ch].astype(BF16), preferred_element_type=F32) for ch in chains}
    v_new = {ch: (sol[ch][:, :hd] - ws[ch][:chunk]).astype(BF16) for ch in chains}
    o = {ch: ws[ch][chunk:] + jnp.dot(qk[ch], v_new[ch], preferred_element_type=F32) for ch in chains}
    for ch in chains:
        kd = (kn[ch] * jnp.exp(glast[ch] - gcol[ch])).astype(BF16)
        s_ref[ch[0], ch[1]] = s_old[ch] * jnp.exp(glast[ch]) + lax.dot_general(
            kd, v_new[ch], (((0,), (0,)), ((), ())), preferred_element_type=F32)

    for ch in chains:
        bi, h = ch
        sl = slice(h * hd, (h + 1) * hd)
        on = o[ch] * lax.rsqrt(jnp.mean(o[ch] * o[ch], axis=-1, keepdims=True) + NORM_EPS) * ng_ref[...]
        gt = gate_ref[bi][:, sl]
        o_ref[bi, :, sl] = (on * (gt * _sigmoid(gt))).astype(o_ref.dtype)


def _gdn_kernel(qkv_ref, gate_ref, ab_ref, cinit_ref, sinit_ref, cw_ref, alog_ref, dtb_ref, ng_ref,
                o_ref, cout_ref, sout_ref, xp_ref, s_ref, *, chunk, nb):
    c = pl.program_id(1)

    @pl.when(c == 0)
    def _():
        xp_ref[:, 0:8, :] = cinit_ref[...]
        s_ref[...] = sinit_ref[...]

    _gdn_chunk(nb, qkv_ref, gate_ref, ab_ref, cw_ref, alog_ref, dtb_ref, ng_ref, o_ref, xp_ref, s_ref, chunk)

    @pl.when(c == pl.num_programs(1) - 1)
    def _():
        cout_ref[...] = xp_ref[:, 0:8, :]
        sout_ref[...] = s_ref[...]


def _gdn(qkv, gate, ab, conv_init, s_init, cw8, alog_row, dtb_row, ng_row):
    b, l, _ = qkv.shape
    chunk = min(CHUNK, l)
    nc = l // chunk
    nb = min(GDN_BATCH_PER_STEP, b)
    assert b % nb == 0
    tok = lambda w: pl.BlockSpec((nb, chunk, w), lambda bb, c: (bb, c, 0))
    const = lambda shape: pl.BlockSpec(shape, lambda bb, c: (0,) * len(shape))
    per_b3 = pl.BlockSpec((nb, 8, CONV_CH), lambda bb, c: (bb, 0, 0))
    per_b4 = pl.BlockSpec((nb, GDN_HEADS, GDN_HEAD_DIM, GDN_HEAD_DIM), lambda bb, c: (bb, 0, 0, 0))
    return pl.pallas_call(
        functools.partial(_gdn_kernel, chunk=chunk, nb=nb),
        grid=(b // nb, nc),
        in_specs=[tok(CONV_CH), tok(GDN_WIDTH), tok(LANES), per_b3, per_b4,
                  const((8, CONV_CH)), const((1, LANES)), const((1, LANES)), const((1, LANES))],
        out_specs=(tok(GDN_WIDTH), per_b3, per_b4),
        out_shape=(jax.ShapeDtypeStruct((b, l, GDN_WIDTH), BF16),
                   jax.ShapeDtypeStruct((b, 8, CONV_CH), F32),
                   jax.ShapeDtypeStruct((b, GDN_HEADS, GDN_HEAD_DIM, GDN_HEAD_DIM), F32)),
        scratch_shapes=[pltpu.VMEM((nb, chunk + 8, CONV_CH), F32),
                        pltpu.VMEM((nb, GDN_HEADS, GDN_HEAD_DIM, GDN_HEAD_DIM), F32)],
        compiler_params=_cparams(("arbitrary", "arbitrary")),
        name="gdn",
    )(qkv, gate, ab, conv_init, s_init, cw8, alog_row, dtb_row, ng_row)


def _out_route_kernel(xp_ref, osbp_ref, ogdnp_ref, xs_ref, osbs_ref, ogdns_ref, wo_ref, nf_ref, rw_ref, rb_ref,
                      h_ref, xn_ref, route_ref, cnt_ref, run_ref, *, tm, steps_p):
    i = pl.program_id(0)

    @pl.when(i == 0)
    def _():
        run_ref[...] = jnp.zeros_like(run_ref)

    is_p = i < steps_p
    x = jnp.where(is_p, xp_ref[...], xs_ref[...])
    osb = jnp.where(is_p, osbp_ref[...], osbs_ref[...])
    ogdn = jnp.where(is_p, ogdnp_ref[...], ogdns_ref[...])
    h = (x
         + jnp.dot(osb, wo_ref[0:SB_WIDTH, :], preferred_element_type=F32)
         + jnp.dot(ogdn, wo_ref[SB_WIDTH:, :], preferred_element_type=F32))
    h_ref[...] = h
    xn = h * lax.rsqrt(jnp.mean(h * h, axis=-1, keepdims=True) + NORM_EPS) * nf_ref[...]
    _store_row_tiles(xn_ref, xn, tm)

    lane = lax.broadcasted_iota(jnp.int32, (1, LANES), 1)
    lane_f = lane.astype(F32)
    rw = rw_ref[...]
    xn_hi = xn.astype(BF16)
    xn_lo = (xn - xn_hi.astype(F32)).astype(BF16)
    rw_hi = rw.astype(BF16)
    rw_lo = (rw - rw_hi.astype(F32)).astype(BF16)
    logits = (jnp.dot(xn_hi, rw_hi, preferred_element_type=F32) + jnp.dot(xn_lo, rw_hi, preferred_element_type=F32)
              + jnp.dot(xn_hi, rw_lo, preferred_element_type=F32)) + rb_ref[...]
    logits = jnp.where(lane < N_EXPERTS, logits, -jnp.inf)
    vals, hots = [], []
    for _ in range(TOP_K):
        m = jnp.max(logits, axis=-1, keepdims=True)
        idx = jnp.min(jnp.where(logits == m, lane_f, float(LANES)), axis=-1, keepdims=True)
        hot = lane_f == idx
        logits = jnp.where(hot, -jnp.inf, logits)
        vals.append(m)
        hots.append((hot, idx))
    exps = [jnp.exp(v - vals[0]) for v in vals]
    denom = exps[0] + exps[1] + exps[2] + exps[3]

    multi = jnp.zeros((tm, LANES), F32)
    for hot, _ in hots:
        multi = jnp.where(hot, 1.0, multi)
    ii = lax.broadcasted_iota(jnp.int32, (tm, tm), 0)
    jj = lax.broadcasted_iota(jnp.int32, (tm, tm), 1)
    earlier = jnp.where(ii > jj, 1.0, 0.0).astype(BF16)
    run = run_ref[0:1, :]
    rank_all = jnp.dot(earlier, multi.astype(BF16), preferred_element_type=F32) + run
    run_new = run + jnp.sum(multi, axis=0, keepdims=True)
    run_ref[...] = jnp.broadcast_to(run_new, run_ref.shape)
    cnt_ref[...] = jnp.broadcast_to(run_new, cnt_ref.shape)

    route = jnp.zeros((tm, LANES), F32)
    for k, (hot, idx) in enumerate(hots):
        rank = jnp.sum(jnp.where(hot, rank_all, 0.0), axis=-1, keepdims=True)
        route = jnp.where(lane == k, idx, route)
        route = jnp.where(lane == TOP_K + k, rank, route)
        route = jnp.where(lane == 2 * TOP_K + k, exps[k] / denom, route)
    route_ref[...] = route


def _out_route(xp, osbp, ogdnp, xs, osbs, ogdns, wo_bf, nf_row, rw_pad, rb_row):
    n_p, n_s = xp.shape[0], xs.shape[0]
    tm = min(TM_DENSE, n_p, n_s)
    steps_p, steps_s = n_p // tm, n_s // tm
    prow = lambda w: pl.BlockSpec((tm, w), lambda i: (jnp.minimum(i, steps_p - 1), 0))
    srow = lambda w: pl.BlockSpec((tm, w), lambda i: (jnp.maximum(i - steps_p, 0), 0))
    orow = lambda w: pl.BlockSpec((tm, w), lambda i: (i, 0))
    const = lambda shape: pl.BlockSpec(shape, lambda i: (0,) * len(shape))
    n_total = n_p + n_s
    out_shape = (jax.ShapeDtypeStruct((n_total, D_MODEL), F32),
                 jax.ShapeDtypeStruct((n_total * ROW_TILE, LANES), F32),
                 jax.ShapeDtypeStruct((n_total, LANES), F32),
                 jax.ShapeDtypeStruct((8, LANES), F32))
    return pl.pallas_call(
        functools.partial(_out_route_kernel, tm=tm, steps_p=steps_p),
        grid=(steps_p + steps_s,),
        in_specs=[prow(D_MODEL), prow(SB_WIDTH), prow(GDN_WIDTH), srow(D_MODEL), srow(SB_WIDTH), srow(GDN_WIDTH),
                  const((D_MODEL, D_MODEL)), const((1, D_MODEL)), const((D_MODEL, LANES)), const((1, LANES))],
        out_specs=(orow(D_MODEL), pl.BlockSpec((tm * ROW_TILE, LANES), lambda i: (i, 0)), orow(LANES),
                   const((8, LANES))),
        out_shape=out_shape,
        scratch_shapes=[pltpu.VMEM((8, LANES), F32)],
        compiler_params=_cparams(("arbitrary",)),
        name="out_route",
    )(xp, osbp, ogdnp, xs, osbs, ogdns, wo_bf, nf_row, rw_pad, rb_row)


def _scatter_rows_kernel(pos_ref, x_ref, xs_hbm, sem, *, tm):
    def row_copy(t, p):
        return pltpu.make_async_copy(x_ref.at[pl.ds(pl.multiple_of(t * ROW_TILE, ROW_TILE), ROW_TILE)],
                                     xs_hbm.at[pl.ds(pl.multiple_of(p * ROW_TILE, ROW_TILE), ROW_TILE)], sem)

    def issue(t, carry):
        for k in range(TOP_K):
            row_copy(t, pos_ref[k, t]).start(priority=k % 2)
        return carry

    lax.fori_loop(0, tm, issue, 0)

    def drain(t, carry):
        for k in range(TOP_K):
            row_copy(0, 0).wait()
        return carry

    lax.fori_loop(0, tm, drain, 0)


def _scatter_rows(pos_t, xn):
    n = xn.shape[0] // ROW_TILE
    tm = min(TM_ROWS, n)
    return pl.pallas_call(
        functools.partial(_scatter_rows_kernel, tm=tm),
        grid=(n // tm,),
        in_specs=[pl.BlockSpec((TOP_K, tm), lambda i: (0, i), memory_space=pltpu.SMEM),
                  pl.BlockSpec((tm * ROW_TILE, LANES), lambda i: (i, 0))],
        out_specs=pl.BlockSpec(memory_space=pl.ANY),
        out_shape=jax.ShapeDtypeStruct((n * TOP_K * ROW_TILE, LANES), xn.dtype),
        scratch_shapes=[pltpu.SemaphoreType.DMA(())],
        compiler_params=_cparams(("arbitrary",)),
        name="moe_scatter",
    )(pos_t, xn)


def _experts_kernel(tile_ref, exp_ref, lo_ref, hi_ref, first_ref, newexp_ref, nvalid_ref,
                    x_ref, wg_ref, bg_ref, wu_ref, bu_ref, wd_ref, bd_ref, y_ref, wg_bf, wu_bf, wd_bf, *, tmg):
    v = pl.program_id(0)

    @pl.when(jnp.logical_and(v < nvalid_ref[0], newexp_ref[v] == 1))
    def _():
        wg_bf[...] = wg_ref[0].astype(BF16)
        wu_bf[...] = wu_ref[0].astype(BF16)
        wd_bf[...] = wd_ref[0].astype(BF16)

    @pl.when(v < nvalid_ref[0])
    def _():
        sub = tmg // EXPERT_SUBTILES
        subs = range(EXPERT_SUBTILES)
        xs = [_load_row_tiles(x_ref, sub, s * sub).astype(BF16) for s in subs]
        gt, up = [], []
        for s in subs:
            gt.append(jnp.dot(xs[s], wg_bf[...], preferred_element_type=F32))
            up.append(jnp.dot(xs[s], wu_bf[...], preferred_element_type=F32))
        hid = []
        for s in subs:
            g = jnp.minimum(gt[s] + bg_ref[0], SWIGLU_LIMIT)
            u = jnp.clip(up[s] + bu_ref[0], -SWIGLU_LIMIT, SWIGLU_LIMIT)
            hid.append(((u + 1.0) * (g * _sigmoid(SWIGLU_ALPHA * g))).astype(BF16))
        ys = [jnp.dot(hid[s], wd_bf[...], preferred_element_type=F32) + bd_ref[0] for s in subs]
        mine = []
        for s in subs:
            rows = tile_ref[v] * tmg + s * sub + lax.broadcasted_iota(jnp.int32, (sub, 1), 0)
            mine.append(jnp.logical_and(rows >= lo_ref[v], rows < hi_ref[v]))

        @pl.when(first_ref[v] == 1)
        def _():
            for s in subs:
                _store_row_tiles(y_ref, jnp.where(mine[s], ys[s], 0.0), sub, s * sub)

        @pl.when(first_ref[v] == 0)
        def _():
            for s in subs:
                _store_row_tiles(y_ref, jnp.where(mine[s], ys[s], _load_row_tiles(y_ref, sub, s * sub)), sub, s * sub)


def _experts(meta, xs, wg, bg, wu, bu, wd, bd):
    n4 = xs.shape[0] // ROW_TILE
    tmg = min(TM_GROUP, n4)
    n_visits = meta[0].shape[0]
    xspec = pl.BlockSpec((tmg * ROW_TILE, LANES), lambda v, tile, exp, *_: (tile[v], 0))
    wspec = pl.BlockSpec((1, D_MODEL, D_MODEL), lambda v, tile, exp, *_: (exp[v], 0, 0))
    bspec = pl.BlockSpec((1, 1, D_MODEL), lambda v, tile, exp, *_: (exp[v], 0, 0))
    grid_spec = pltpu.PrefetchScalarGridSpec(
        num_scalar_prefetch=len(meta),
        grid=(n_visits,),
        in_specs=[xspec, wspec, bspec, wspec, bspec, wspec, bspec],
        out_specs=xspec,
        scratch_shapes=[pltpu.VMEM((D_MODEL, D_MODEL), BF16)] * 3,
    )
    return pl.pallas_call(
        functools.partial(_experts_kernel, tmg=tmg),
        grid_spec=grid_spec,
        out_shape=jax.ShapeDtypeStruct((n4 * ROW_TILE, LANES), F32),
        compiler_params=_cparams(("arbitrary",), VMEM_LIMIT_EXPERTS),
        name="moe_experts",
    )(*meta, xs, wg, bg, wu, bu, wd, bd)


def _combine_kernel(pos_ref, h_ref, route_ref, nfin_ref, ys_hbm, o_ref, buf, sem, *, tm):
    def row_copy(t, k, p):
        return pltpu.make_async_copy(ys_hbm.at[pl.ds(pl.multiple_of(p * ROW_TILE, ROW_TILE), ROW_TILE)],
                                     buf.at[k, pl.ds(pl.multiple_of(t * ROW_TILE, ROW_TILE), ROW_TILE)], sem)

    def issue(t, carry):
        for k in range(TOP_K):
            row_copy(t, k, pos_ref[k, t]).start(priority=k % 2)
        return carry

    lax.fori_loop(0, tm, issue, 0)

    def drain(t, carry):
        for k in range(TOP_K):
            row_copy(0, k, 0).wait()
        return carry

    lax.fori_loop(0, tm, drain, 0)

    route = route_ref[...]
    h = h_ref[...]
    pieces = []
    for j in range(ROW_TILE):
        piece = h[:, j * LANES:(j + 1) * LANES]
        for k in range(TOP_K):
            piece = piece + route[:, 2 * TOP_K + k:2 * TOP_K + k + 1] * buf[k, pl.ds(j, tm, stride=ROW_TILE), :]
        pieces.append(piece)
    out = jnp.concatenate(pieces, axis=-1)
    o_ref[...] = out * lax.rsqrt(jnp.mean(out * out, axis=-1, keepdims=True) + NORM_EPS) * nfin_ref[...]


def _combine(pos_t, h, route, nfin_row, ys, row0, n):
    tm = min(TM_ROWS, n)
    off = row0 // tm
    return pl.pallas_call(
        functools.partial(_combine_kernel, tm=tm),
        grid=(n // tm,),
        in_specs=[pl.BlockSpec((TOP_K, tm), lambda i: (0, i + off), memory_space=pltpu.SMEM),
                  pl.BlockSpec((tm, D_MODEL), lambda i: (i + off, 0)),
                  pl.BlockSpec((tm, LANES), lambda i: (i + off, 0)),
                  pl.BlockSpec((1, D_MODEL), lambda i: (0, 0)),
                  pl.BlockSpec(memory_space=pl.ANY)],
        out_specs=pl.BlockSpec((tm, D_MODEL), lambda i: (i, 0)),
        out_shape=jax.ShapeDtypeStruct((n, D_MODEL), F32),
        scratch_shapes=[pltpu.VMEM((TOP_K, tm * ROW_TILE, LANES), F32), pltpu.SemaphoreType.DMA(())],
        compiler_params=_cparams(("arbitrary",)),
        name="moe_combine",
    )(pos_t, h, route, nfin_row, ys)


def _group_plan(counts, n4, tmg):
    n_tiles = n4 // tmg
    n_visits = n_tiles + N_EXPERTS - 1
    ends = jnp.cumsum(counts)
    starts = ends - counts
    t_first = starts // tmg
    t_cnt = jnp.where(counts > 0, (ends - 1) // tmg - t_first + 1, 0)
    v_end = jnp.cumsum(t_cnt)
    v_start = v_end - t_cnt
    total = v_end[-1]
    v = jnp.arange(n_visits, dtype=jnp.int32)
    g = jnp.minimum(jnp.sum((v[:, None] >= v_end[None, :]).astype(jnp.int32), axis=1), N_EXPERTS - 1)
    g_last = jnp.max(jnp.where(counts > 0, jnp.arange(N_EXPERTS), 0)).astype(jnp.int32)
    valid = v < total
    g = jnp.where(valid, g, g_last)
    hot = g[:, None] == jnp.arange(N_EXPERTS, dtype=jnp.int32)[None, :]
    pick = lambda table: jnp.sum(jnp.where(hot, table[None, :], 0), axis=1)
    tile = jnp.where(valid, pick(t_first) + (v - pick(v_start)), n_tiles - 1).astype(jnp.int32)
    lo = jnp.where(valid, jnp.maximum(pick(starts), tile * tmg), 0).astype(jnp.int32)
    hi = jnp.where(valid, jnp.minimum(pick(ends), (tile + 1) * tmg), 0).astype(jnp.int32)
    prev_tile = jnp.concatenate([jnp.full((1,), -1, jnp.int32), tile[:-1]])
    first = (tile != prev_tile).astype(jnp.int32)
    prev_g = jnp.concatenate([jnp.full((1,), -1, jnp.int32), g[:-1]])
    newexp = (g != prev_g).astype(jnp.int32)
    meta = (tile, g, lo, hi, first, newexp, total.reshape(1).astype(jnp.int32))
    return meta, starts


def _pad_rows(a, rows):
    return jnp.concatenate([a, jnp.zeros((rows - a.shape[0],) + a.shape[1:], a.dtype)], axis=0)


def kernel(x_prompt, x_sample, cache_sb_k, cache_sb_v, cache_gdn_conv, state_gdn, norm_mix, w_in, conv_w, a_log,
           dt_bias, norm_sb, norm_gdn, w_out, norm_ffn, router_w, router_b, w_gate, b_gate, w_up, b_up, w_down,
           b_down, norm_final):
    bp, lp, _ = x_prompt.shape
    bs, ls, _ = x_sample.shape
    past = cache_sb_k.shape[2]
    n_p, n_s = bp * lp, bs * ls
    n_tot = n_p + n_s
    assert norm_mix.shape[0] == 1, "single-layer trunk"
    tm_min = min(TM_DENSE, TM_ROWS, n_p, n_s)
    assert n_p % tm_min == 0 and n_s % tm_min == 0 and n_p % min(TM_ROWS, n_s) == 0
    assert (n_tot * TOP_K) % TM_GROUP == 0 and n_tot % TM_ROWS == 0
    assert lp % min(TQ_SB, lp) == 0 and lp % min(CHUNK, lp) == 0 and ls % min(CHUNK, ls) == 0 and ls >= 8

    w_in_bf = jnp.pad(w_in[0], ((0, 0), (0, IN_WIDTH_PAD - IN_WIDTH))).astype(BF16)
    nmix_row = norm_mix[0].reshape(1, D_MODEL)
    nsb_row = jnp.tile(norm_sb[0], 2).reshape(1, LANES)
    ng_row = norm_gdn[0].reshape(1, LANES)
    cw8 = _pad_rows(conv_w[0], 8)
    alog_row = jnp.pad(a_log[0], (0, LANES - GDN_HEADS)).reshape(1, LANES)
    dtb_row = jnp.pad(dt_bias[0], (0, LANES - GDN_HEADS)).reshape(1, LANES)
    wo_bf = w_out[0].astype(BF16)
    nf_row = norm_ffn[0].reshape(1, D_MODEL)
    rw_pad = jnp.pad(router_w[0], ((0, 0), (0, LANES - N_EXPERTS)))
    rb_row = jnp.pad(router_b[0], (0, LANES - N_EXPERTS)).reshape(1, LANES)
    nfin_row = norm_final.reshape(1, D_MODEL)

    xp2 = x_prompt.reshape(n_p, D_MODEL)
    xs2 = x_sample.reshape(n_s, D_MODEL)

    qsb, k_prompt, v_prompt, kbf, vbf, gdn_in, gate, ab = _in_proj(xp2, nmix_row, w_in_bf)
    r3 = lambda a, b, l: a.reshape(b, l, a.shape[-1])
    osb_p = _sb_prompt(r3(qsb, bp, lp), r3(kbf, bp, lp), r3(vbf, bp, lp), nsb_row)
    ogdn_p, conv_p, state_p = _gdn(
        r3(gdn_in, bp, lp), r3(gate, bp, lp), r3(ab, bp, lp),
        jnp.zeros((bp, 8, CONV_CH), F32), jnp.zeros((bp, GDN_HEADS, GDN_HEAD_DIM, GDN_HEAD_DIM), F32),
        cw8, alog_row, dtb_row, ng_row)

    qsb, k_sample, v_sample, kbf, vbf, gdn_in, gate, ab = _in_proj(xs2, nmix_row, w_in_bf)
    keys_minor = lambda cache: jnp.transpose(cache, (0, 1, 3, 4, 2))
    osb_s = _sb_sample(r3(qsb, bs, ls), r3(kbf, bs, ls), r3(vbf, bs, ls), keys_minor(cache_sb_k),
                       keys_minor(cache_sb_v), nsb_row)
    conv_init = jnp.concatenate([jnp.zeros((bs, 8 - (CONV_WIDTH - 1), CONV_CH), F32), cache_gdn_conv[0]], axis=1)
    ogdn_s, conv_s, state_s = _gdn(r3(gdn_in, bs, ls), r3(gate, bs, ls), r3(ab, bs, ls), conv_init, state_gdn[0],
                                   cw8, alog_row, dtb_row, ng_row)

    h_buf, xn_buf, route_buf, cnt = _out_route(
        xp2, osb_p.reshape(n_p, SB_WIDTH), ogdn_p.reshape(n_p, GDN_WIDTH),
        xs2, osb_s.reshape(n_s, SB_WIDTH), ogdn_s.reshape(n_s, GDN_WIDTH), wo_bf, nf_row, rw_pad, rb_row)

    counts = cnt[0, :N_EXPERTS].astype(jnp.int32)
    n4 = n_tot * TOP_K
    tmg = min(TM_GROUP, n4)
    meta, starts = _group_plan(counts, n4, tmg)
    idx_t = route_buf[:, 0:TOP_K].T.astype(jnp.int32)
    rank_t = route_buf[:, TOP_K:2 * TOP_K].T.astype(jnp.int32)
    pos_t = rank_t
    for e in range(N_EXPERTS):
        pos_t = pos_t + jnp.where(idx_t == e, starts[e], 0)

    xs_sorted = _scatter_rows(pos_t, xn_buf)
    e3 = lambda bias: bias[0].reshape(N_EXPERTS, 1, D_MODEL)
    ys_sorted = _experts(meta, xs_sorted, w_gate[0], e3(b_gate), w_up[0], e3(b_up), w_down[0], e3(b_down))
    y_prompt = _combine(pos_t, h_buf, route_buf, nfin_row, ys_sorted, 0, n_p).reshape(bp, lp, D_MODEL)
    y_sample = _combine(pos_t, h_buf, route_buf, nfin_row, ys_sorted, n_p, n_s).reshape(bs, ls, D_MODEL)

    heads = lambda a, b, l: a.reshape(1, b, l, SB_HEADS, SB_HEAD_DIM)
    return (y_prompt, y_sample,
            heads(k_prompt, bp, lp), heads(v_prompt, bp, lp),
            conv_p[:, 8 - (CONV_WIDTH - 1):][None], state_p[None],
            heads(k_sample, bs, ls), heads(v_sample, bs, ls),
            conv_s[:, 8 - (CONV_WIDTH - 1):][None], state_s[None])
```

```python
import functools
import math

import jax
import jax.numpy as jnp
from jax import lax
from jax.experimental import pallas as pl
from jax.experimental.pallas import tpu as pltpu

F32 = jnp.float32
BF16 = jnp.bfloat16

D_MODEL = 1024
SB_HEAD_DIM = 64
SB_WIDTH = 512
SB_HEADS = SB_WIDTH // SB_HEAD_DIM
SB_PAIRS = SB_WIDTH // 128
GDN_HEAD_DIM = 128
GDN_HEADS = 4
GDN_WIDTH = 512
CONV_WIDTH = 4
CONV_CH = 3 * GDN_WIDTH
IN_WIDTH = 3 * SB_WIDTH + 4 * GDN_WIDTH + 2 * GDN_HEADS
IN_WIDTH_PAD = 3 * SB_WIDTH + 4 * GDN_WIDTH + 128
N_EXPERTS = 32
TOP_K = 4
SWIGLU_LIMIT = 7.0
SWIGLU_ALPHA = 1.702
NORM_EPS = 1e-6
L2_EPS = 1e-6
CHUNK = 64

LANES = 128
VMEM_LIMIT_BYTES = 48 * 1024 * 1024
VMEM_LIMIT_EXPERTS = 56 * 1024 * 1024
SB_LOG_CUTOFF = 104.0

TM_DENSE = 512
TM_ROUTE = 512
ROUTE_SUBTILES = 2
TQ_SB = 256
TK_SB = 256
SB_PAIRS_PER_STEP = 4
SB_CACHE_WINDOW = 512
GDN_BATCH_PER_STEP = 4
TM_ROWS = 256
TM_GROUP = 512
EXPERT_SUBTILES = 2


def _cparams(sem, limit=VMEM_LIMIT_BYTES):
    return pltpu.CompilerParams(dimension_semantics=sem, vmem_limit_bytes=limit)


def _softplus(z):
    return jnp.maximum(z, 0.0) + jnp.log(1.0 + jnp.exp(-jnp.abs(z)))


def _sigmoid(z):
    return 1.0 / (1.0 + jnp.exp(-z))


def _store_heads(ref, val, tm):
    for h in range(SB_HEADS):
        ref[pl.ds(h, tm, stride=SB_HEADS), :] = val[:, h * SB_HEAD_DIM:(h + 1) * SB_HEAD_DIM]


ROW_TILE = D_MODEL // LANES


def _store_row_tiles(ref, val, tm, t0=0):
    for j in range(ROW_TILE):
        ref[pl.ds(t0 * ROW_TILE + j, tm, stride=ROW_TILE), :] = val[:, j * LANES:(j + 1) * LANES]


def _load_row_tiles(ref, tm, t0=0):
    return jnp.concatenate([ref[pl.ds(t0 * ROW_TILE + j, tm, stride=ROW_TILE), :] for j in range(ROW_TILE)],
                           axis=-1)


def _in_proj_kernel(x_ref, g_ref, w_ref, qsb_ref, ksb_ref, vsb_ref, kbf_ref, vbf_ref, gdn_ref, gate_ref, ab_ref,
                    *, tm):
    x = x_ref[...]
    xn = x * lax.rsqrt(jnp.mean(x * x, axis=-1, keepdims=True) + NORM_EPS) * g_ref[...]
    xn = xn.astype(BF16)

    def mm(lo, hi):
        return jnp.dot(xn, w_ref[:, lo:hi], preferred_element_type=F32)

    qsb_ref[...] = (mm(0, 512) * (SB_HEAD_DIM ** -0.5)).astype(BF16)
    k = mm(512, 1024)
    _store_heads(ksb_ref, k, tm)
    kbf_ref[...] = k.astype(BF16)
    v = mm(1024, 1536)
    _store_heads(vsb_ref, v, tm)
    vbf_ref[...] = v.astype(BF16)
    for j in range(3):
        gdn_ref[:, j * 512:(j + 1) * 512] = mm(1536 + j * 512, 2048 + j * 512)
    gate_ref[...] = mm(3072, 3584)
    ab_ref[...] = mm(3584, 3712)


def _in_proj(x2d, g_row, w_bf):
    n = x2d.shape[0]
    tm = min(TM_DENSE, n)
    row = lambda w: pl.BlockSpec((tm, w), lambda i: (i, 0))
    head_rows = pl.BlockSpec((tm * SB_HEADS, SB_HEAD_DIM), lambda i: (i, 0))
    out_shape = (
        jax.ShapeDtypeStruct((n, SB_WIDTH), BF16),
        jax.ShapeDtypeStruct((n * SB_HEADS, SB_HEAD_DIM), F32),
        jax.ShapeDtypeStruct((n * SB_HEADS, SB_HEAD_DIM), F32),
        jax.ShapeDtypeStruct((n, SB_WIDTH), BF16),
        jax.ShapeDtypeStruct((n, SB_WIDTH), BF16),
        jax.ShapeDtypeStruct((n, CONV_CH), F32),
        jax.ShapeDtypeStruct((n, GDN_WIDTH), F32),
        jax.ShapeDtypeStruct((n, LANES), F32),
    )
    return pl.pallas_call(
        functools.partial(_in_proj_kernel, tm=tm),
        grid=(n // tm,),
        in_specs=[row(D_MODEL), pl.BlockSpec((1, D_MODEL), lambda i: (0, 0)),
                  pl.BlockSpec((D_MODEL, IN_WIDTH_PAD), lambda i: (0, 0))],
        out_specs=(row(SB_WIDTH), head_rows, head_rows, row(SB_WIDTH), row(SB_WIDTH),
                   row(CONV_CH), row(GDN_WIDTH), row(LANES)),
        out_shape=out_shape,
        compiler_params=_cparams(("arbitrary",)),
        name="in_proj",
    )(x2d, g_row, w_bf)


def _sb_step(qs, k_blks, v_blks, qpos, kpos, acc_ref, c_ref, kv_t=False):
    pairs = len(qs)
    chains = [(p, h) for p in range(pairs) for h in range(2)]
    tk = k_blks[0].shape[1 if kv_t else 0]
    lane = lax.broadcasted_iota(jnp.int32, (1, LANES), 1)
    visible = kpos < qpos
    jj = lax.broadcasted_iota(jnp.int32, (tk, tk), 0)
    ss = lax.broadcasted_iota(jnp.int32, (tk, tk), 1)
    tri = jnp.where(jj >= ss, 1.0, 0.0).astype(BF16)
    nt = (((1,), (1,)), ((), ()))

    if kv_t:
        z = {ch: jnp.dot(qs[ch[0]][ch[1]], k_blks[ch[0]], preferred_element_type=F32) for ch in chains}
    else:
        z = {ch: lax.dot_general(qs[ch[0]][ch[1]], k_blks[ch[0]], nt, preferred_element_type=F32) for ch in chains}
    sp = {ch: jnp.where(visible, _softplus(z[ch]), 0.0) for ch in chains}
    sp_hi = {ch: sp[ch].astype(BF16) for ch in chains}
    sp_lo = {ch: (sp[ch] - sp_hi[ch].astype(F32)).astype(BF16) for ch in chains}
    r = {ch: jnp.dot(sp_hi[ch], tri, preferred_element_type=F32) + jnp.dot(sp_lo[ch], tri, preferred_element_type=F32)
         for ch in chains}
    def mass(ch):
        c = c_ref[2 * ch[0] + ch[1]]
        return c[:, :tk] if tk <= LANES else jnp.concatenate([c] * (tk // LANES), axis=-1)

    w = {ch: jnp.where(visible, jnp.exp(z[ch] - r[ch] - mass(ch)), 0.0).astype(BF16) for ch in chains}
    if kv_t:
        pv = {ch: lax.dot_general(w[ch], v_blks[ch[0]], nt, preferred_element_type=F32) for ch in chains}
    else:
        pv = {ch: jnp.dot(w[ch], v_blks[ch[0]], preferred_element_type=F32) for ch in chains}
    cmin = None
    for p in range(pairs):
        acc_ref[p] += jnp.where(lane < SB_HEAD_DIM, pv[(p, 0)], pv[(p, 1)])
    for ch in chains:
        c_new = c_ref[2 * ch[0] + ch[1]] + r[ch][:, 0:1]
        c_ref[2 * ch[0] + ch[1]] = c_new
        m = jnp.min(c_new)
        cmin = m if cmin is None else jnp.minimum(cmin, m)
    return cmin


def _sb_finish(acc, nsb_row):
    lane = lax.broadcasted_iota(jnp.int32, (1, LANES), 1)
    first = lane < SB_HEAD_DIM
    sq = acc * acc
    s_all = jnp.sum(sq, axis=-1, keepdims=True)
    s0 = jnp.sum(jnp.where(first, sq, 0.0), axis=-1, keepdims=True)
    ms = jnp.where(first, s0, s_all - s0) * (1.0 / SB_HEAD_DIM)
    return acc * lax.rsqrt(ms + NORM_EPS) * nsb_row


def _split_heads(q):
    lane = lax.broadcasted_iota(jnp.int32, (1, LANES), 1)
    zero = jnp.zeros_like(q)
    return jnp.where(lane < SB_HEAD_DIM, q, zero), jnp.where(lane >= SB_HEAD_DIM, q, zero)


def _sb_continue(carry):
    j, cmin = carry
    return jnp.logical_and(j >= 0, cmin < SB_LOG_CUTOFF)


def _sb_prompt_kernel(q_ref, k_ref, v_ref, nsb_ref, o_ref, acc_ref, c_ref, *, tq, tk, pairs):
    i = pl.program_id(2)
    lanes = lambda p: slice(p * LANES, (p + 1) * LANES)
    qs = [_split_heads(q_ref[0, :, lanes(p)]) for p in range(pairs)]
    acc_ref[...] = jnp.zeros_like(acc_ref)
    c_ref[...] = jnp.zeros_like(c_ref)
    qpos = i * tq + lax.broadcasted_iota(jnp.int32, (tq, 1), 0)

    def body(carry):
        j, _ = carry
        ks = pl.multiple_of(j * tk, tk)
        kpos = ks + lax.broadcasted_iota(jnp.int32, (1, tk), 1)
        k_blks = [k_ref[0, pl.ds(ks, tk), lanes(p)] for p in range(pairs)]
        v_blks = [v_ref[0, pl.ds(ks, tk), lanes(p)] for p in range(pairs)]
        return j - 1, _sb_step(qs, k_blks, v_blks, qpos, kpos, acc_ref, c_ref)

    lax.while_loop(_sb_continue, body, ((i + 1) * (tq // tk) - 1, jnp.float32(0.0)))
    for p in range(pairs):
        o_ref[0, :, lanes(p)] = _sb_finish(acc_ref[p], nsb_ref[...]).astype(o_ref.dtype)


def _sb_prompt(q, k, v, nsb_row):
    b, l, _ = q.shape
    tq = min(TQ_SB, l)
    tk = min(TK_SB, tq)
    pairs = SB_PAIRS_PER_STEP
    width = pairs * LANES
    qspec = pl.BlockSpec((1, tq, width), lambda bb, hp, i: (bb, i, hp))
    kvspec = pl.BlockSpec((1, l, width), lambda bb, hp, i: (bb, 0, hp))
    return pl.pallas_call(
        functools.partial(_sb_prompt_kernel, tq=tq, tk=tk, pairs=pairs),
        grid=(b, SB_PAIRS // pairs, l // tq),
        in_specs=[qspec, kvspec, kvspec, pl.BlockSpec((1, LANES), lambda bb, hp, i: (0, 0))],
        out_specs=qspec,
        out_shape=jax.ShapeDtypeStruct((b, l, SB_WIDTH), BF16),
        scratch_shapes=[pltpu.VMEM((pairs, tq, LANES), F32), pltpu.VMEM((2 * pairs, tq, LANES), F32)],
        compiler_params=_cparams(("arbitrary", "arbitrary", "arbitrary")),
        name="sb_prompt",
    )(q, k, v, nsb_row)


def _sb_sample_kernel(q_ref, kn_ref, vn_ref, kw_ref, vw_ref, kc_hbm, vc_hbm, nsb_ref, o_ref,
                      acc_ref, c_ref, kbuf, vbuf, sem, *, past, window, tk):
    b = pl.program_id(0)
    hp = pl.program_id(1)
    t = q_ref.shape[1]
    qs = [_split_heads(q_ref[0])]
    acc_ref[...] = jnp.zeros_like(acc_ref)
    c_ref[...] = jnp.zeros_like(c_ref)
    qpos = past + lax.broadcasted_iota(jnp.int32, (t, 1), 0)

    kpos_new = past + lax.broadcasted_iota(jnp.int32, (1, t), 1)
    cmin0 = _sb_step(qs, [kn_ref[0]], [vn_ref[0]], qpos, kpos_new, acc_ref, c_ref)

    def pair_t(blk):
        return blk.reshape(LANES, tk).astype(BF16)

    def wbody(carry):
        j, _ = carry
        ws = pl.multiple_of(j * tk, tk)
        kpos = (past - window) + ws + lax.broadcasted_iota(jnp.int32, (1, tk), 1)
        cmin = _sb_step(qs, [pair_t(kw_ref[:, :, pl.ds(ws, tk)])], [pair_t(vw_ref[:, :, pl.ds(ws, tk)])],
                        qpos, kpos, acc_ref, c_ref, kv_t=True)
        return j - 1, cmin

    _, cmin1 = lax.while_loop(_sb_continue, wbody, (window // tk - 1, cmin0))

    n_old = (past - window) // tk
    if n_old > 0:
        def obody(carry):
            j, _ = carry
            keys = pl.ds(pl.multiple_of(j * tk, tk), tk)
            ck = pltpu.make_async_copy(kc_hbm.at[0, b, pl.ds(2 * hp, 2), :, keys], kbuf, sem.at[0])
            cv = pltpu.make_async_copy(vc_hbm.at[0, b, pl.ds(2 * hp, 2), :, keys], vbuf, sem.at[1])
            ck.start()
            cv.start()
            ck.wait()
            cv.wait()
            kpos = j * tk + lax.broadcasted_iota(jnp.int32, (1, tk), 1)
            cmin = _sb_step(qs, [pair_t(kbuf[...])], [pair_t(vbuf[...])], qpos, kpos, acc_ref, c_ref, kv_t=True)
            return j - 1, cmin

        lax.while_loop(_sb_continue, obody, (n_old - 1, cmin1))

    o_ref[0] = _sb_finish(acc_ref[0], nsb_ref[...]).astype(o_ref.dtype)


def _sb_sample(q, k_new, v_new, k_cache_t, v_cache_t, nsb_row):
    b, t, _ = q.shape
    past = k_cache_t.shape[4]
    window = min(SB_CACHE_WINDOW, past)
    tk = min(TK_SB, window)
    assert past % window == 0 and window % tk == 0 and (past - window) % tk == 0
    new_spec = pl.BlockSpec((1, t, LANES), lambda bb, hp: (bb, 0, hp))
    win_spec = pl.BlockSpec((None, None, 2, SB_HEAD_DIM, window), lambda bb, hp: (0, bb, hp, 0, past // window - 1))
    any_spec = pl.BlockSpec(memory_space=pl.ANY)
    return pl.pallas_call(
        functools.partial(_sb_sample_kernel, past=past, window=window, tk=tk),
        grid=(b, SB_PAIRS),
        in_specs=[new_spec, new_spec, new_spec, win_spec, win_spec, any_spec, any_spec,
                  pl.BlockSpec((1, LANES), lambda bb, hp: (0, 0))],
        out_specs=new_spec,
        out_shape=jax.ShapeDtypeStruct((b, t, SB_WIDTH), BF16),
        scratch_shapes=[pltpu.VMEM((1, t, LANES), F32), pltpu.VMEM((2, t, LANES), F32),
                        pltpu.VMEM((2, SB_HEAD_DIM, tk), F32), pltpu.VMEM((2, SB_HEAD_DIM, tk), F32),
                        pltpu.SemaphoreType.DMA((2,))],
        compiler_params=_cparams(("arbitrary", "arbitrary")),
        name="sb_sample",
    )(q, k_new, v_new, k_cache_t, v_cache_t, k_cache_t, v_cache_t, nsb_row)


def _gdn_chunk(nb, qkv_ref, gate_ref, ab_ref, cw_ref, alog_ref, dtb_ref, ng_ref, o_ref, xp_ref, s_ref, chunk):
    ii = lax.broadcasted_iota(jnp.int32, (chunk, chunk), 0)
    jj = lax.broadcasted_iota(jnp.int32, (chunk, chunk), 1)
    tri = jnp.where(ii >= jj, 1.0, 0.0).astype(F32)
    scale = GDN_HEAD_DIM ** -0.5
    hd = GDN_HEAD_DIM
    n_fac = int(math.log2(chunk)) - 1
    chains = [(bi, h) for bi in range(nb) for h in range(GDN_HEADS)]

    ys, gcs, gcts, betas = [], [], [], []
    for bi in range(nb):
        xp_ref[bi, 8:8 + chunk, :] = qkv_ref[bi]
        y = cw_ref[0:1, :] * xp_ref[bi, 5:5 + chunk, :]
        for i in range(1, CONV_WIDTH):
            y = y + cw_ref[i:i + 1, :] * xp_ref[bi, 5 + i:5 + i + chunk, :]
        ys.append(y * _sigmoid(y))
        xp_ref[bi, 0:8, :] = xp_ref[bi, chunk:chunk + 8, :]
        ab = ab_ref[bi]
        g_tile = -jnp.exp(alog_ref[...]) * _softplus(ab + dtb_ref[...])
        betas.append(_sigmoid(ab))
        gc_tile = jnp.dot(tri, g_tile, preferred_element_type=F32, precision=lax.Precision.HIGHEST)
        gcs.append(gc_tile)
        gcts.append(gc_tile.T)

    kn, qs, kb, vb, decay, egc, glast, gcol = {}, {}, {}, {}, {}, {}, {}, {}
    for ch in chains:
        bi, h = ch
        y = ys[bi]
        qh = y[:, h * hd:(h + 1) * hd]
        kh = y[:, GDN_WIDTH + h * hd:GDN_WIDTH + (h + 1) * hd]
        vh = y[:, 2 * GDN_WIDTH + h * hd:2 * GDN_WIDTH + (h + 1) * hd]
        qs[ch] = qh * lax.rsqrt(jnp.sum(qh * qh, axis=-1, keepdims=True) + L2_EPS) * scale
        kn[ch] = kh * lax.rsqrt(jnp.sum(kh * kh, axis=-1, keepdims=True) + L2_EPS)
        gcol[ch] = gcs[bi][:, h:h + 1]
        grow = gcts[bi][h:h + 1, :]
        glast[ch] = gcs[bi][chunk - 1:chunk, h:h + 1]
        bcol = betas[bi][:, GDN_HEADS + h:GDN_HEADS + h + 1]
        decay[ch] = jnp.exp(jnp.where(ii >= jj, gcol[ch] - grow, -1e30))
        egc[ch] = jnp.exp(gcol[ch])
        kb[ch] = kn[ch] * bcol
        vb[ch] = vh * bcol

    a2 = {ch: lax.dot_general(jnp.concatenate([kb[ch], qs[ch]], axis=0).astype(BF16), kn[ch].astype(BF16),
                              (((1,), (1,)), ((), ())), preferred_element_type=F32) for ch in chains}
    lb = {ch: jnp.where(ii > jj, a2[ch][:chunk] * decay[ch], 0.0).astype(BF16) for ch in chains}
    qk = {ch: (a2[ch][chunk:] * decay[ch]).astype(BF16) for ch in chains}

    rhs = {ch: jnp.concatenate([vb[ch], kb[ch] * egc[ch]], axis=1) for ch in chains}
    sol = {ch: rhs[ch] - jnp.dot(lb[ch], rhs[ch].astype(BF16), preferred_element_type=F32) for ch in chains}
    pw = {ch: jnp.dot(lb[ch], lb[ch], preferred_element_type=F32).astype(BF16) for ch in chains}
    for f in range(n_fac):
        sol = {ch: sol[ch] + jnp.dot(pw[ch], sol[ch].astype(BF16), preferred_element_type=F32) for ch in chains}
        if f + 1 < n_fac:
            pw = {ch: jnp.dot(pw[ch], pw[ch], preferred_element_type=F32).astype(BF16) for ch in chains}

    s_old = {ch: s_ref[ch[0], ch[1]] for ch in chains}
    ws = {ch: jnp.dot(jnp.concatenate([sol[ch][:, hd:], qs[ch] * egc[ch]], axis=0).astype(BF16),
                      s_old[ch].astype(BF16), preferred_element_type=F32) for ch in chains}
    v_new = {ch: (sol[ch][:, :hd] - ws[ch][:chunk]).astype(BF16) for ch in chains}
    o = {ch: ws[ch][chunk:] + jnp.dot(qk[ch], v_new[ch], preferred_element_type=F32) for ch in chains}
    for ch in chains:
        kd = (kn[ch] * jnp.exp(glast[ch] - gcol[ch])).astype(BF16)
        s_ref[ch[0], ch[1]] = s_old[ch] * jnp.exp(glast[ch]) + lax.dot_general(
            kd, v_new[ch], (((0,), (0,)), ((), ())), preferred_element_type=F32)

    for ch in chains:
        bi, h = ch
        sl = slice(h * hd, (h + 1) * hd)
        on = o[ch] * lax.rsqrt(jnp.mean(o[ch] * o[ch], axis=-1, keepdims=True) + NORM_EPS) * ng_ref[...]
        gt = gate_ref[bi][:, sl]
        o_ref[bi, :, sl] = (on * (gt * _sigmoid(gt))).astype(o_ref.dtype)


def _gdn_kernel(qkv_ref, gate_ref, ab_ref, cinit_ref, sinit_ref, cw_ref, alog_ref, dtb_ref, ng_ref,
                o_ref, cout_ref, sout_ref, xp_ref, s_ref, *, chunk, nb):
    c = pl.program_id(1)

    @pl.when(c == 0)
    def _():
        xp_ref[:, 0:8, :] = cinit_ref[...]
        s_ref[...] = sinit_ref[...]

    _gdn_chunk(nb, qkv_ref, gate_ref, ab_ref, cw_ref, alog_ref, dtb_ref, ng_ref, o_ref, xp_ref, s_ref, chunk)

    @pl.when(c == pl.num_programs(1) - 1)
    def _():
        cout_ref[...] = xp_ref[:, 0:8, :]
        sout_ref[...] = s_ref[...]


def _gdn(qkv, gate, ab, conv_init, s_init, cw8, alog_row, dtb_row, ng_row):
    b, l, _ = qkv.shape
    chunk = min(CHUNK, l)
    nc = l // chunk
    nb = min(GDN_BATCH_PER_STEP, b)
    assert b % nb == 0
    tok = lambda w: pl.BlockSpec((nb, chunk, w), lambda bb, c: (bb, c, 0))
    const = lambda shape: pl.BlockSpec(shape, lambda bb, c: (0,) * len(shape))
    per_b3 = pl.BlockSpec((nb, 8, CONV_CH), lambda bb, c: (bb, 0, 0))
    per_b4 = pl.BlockSpec((nb, GDN_HEADS, GDN_HEAD_DIM, GDN_HEAD_DIM), lambda bb, c: (bb, 0, 0, 0))
    return pl.pallas_call(
        functools.partial(_gdn_kernel, chunk=chunk, nb=nb),
        grid=(b // nb, nc),
        in_specs=[tok(CONV_CH), tok(GDN_WIDTH), tok(LANES), per_b3, per_b4,
                  const((8, CONV_CH)), const((1, LANES)), const((1, LANES)), const((1, LANES))],
        out_specs=(tok(GDN_WIDTH), per_b3, per_b4),
        out_shape=(jax.ShapeDtypeStruct((b, l, GDN_WIDTH), BF16),
                   jax.ShapeDtypeStruct((b, 8, CONV_CH), F32),
                   jax.ShapeDtypeStruct((b, GDN_HEADS, GDN_HEAD_DIM, GDN_HEAD_DIM), F32)),
        scratch_shapes=[pltpu.VMEM((nb, chunk + 8, CONV_CH), F32),
                        pltpu.VMEM((nb, GDN_HEADS, GDN_HEAD_DIM, GDN_HEAD_DIM), F32)],
        compiler_params=_cparams(("arbitrary", "arbitrary")),
        name="gdn",
    )(qkv, gate, ab, conv_init, s_init, cw8, alog_row, dtb_row, ng_row)


def _out_route_kernel(xp_ref, osbp_ref, ogdnp_ref, xs_ref, osbs_ref, ogdns_ref, wo_ref, nf_ref, rw_ref, rb_ref,
                      h_ref, xn_ref, route_ref, cnt_ref, run_ref, *, tm, steps_p):
    i = pl.program_id(0)

    @pl.when(i == 0)
    def _():
        run_ref[...] = jnp.zeros_like(run_ref)

    is_p = i < steps_p
    sub = tm // ROUTE_SUBTILES
    subs = range(ROUTE_SUBTILES)
    rows = [pl.ds(s * sub, sub) for s in subs]
    lane = lax.broadcasted_iota(jnp.int32, (1, LANES), 1)
    lane_f = lane.astype(F32)

    x = [jnp.where(is_p, xp_ref[r, :], xs_ref[r, :]) for r in rows]
    osb = [jnp.where(is_p, osbp_ref[r, :], osbs_ref[r, :]) for r in rows]
    ogdn = [jnp.where(is_p, ogdnp_ref[r, :], ogdns_ref[r, :]) for r in rows]
    h = [x[s] + jnp.dot(osb[s], wo_ref[0:SB_WIDTH, :], preferred_element_type=F32)
         + jnp.dot(ogdn[s], wo_ref[SB_WIDTH:, :], preferred_element_type=F32) for s in subs]
    for s in subs:
        h_ref[rows[s], :] = h[s]
    xn = [h[s] * lax.rsqrt(jnp.mean(h[s] * h[s], axis=-1, keepdims=True) + NORM_EPS) * nf_ref[...] for s in subs]
    for s in subs:
        _store_row_tiles(xn_ref, xn[s], sub, s * sub)

    rw = rw_ref[...]
    rw_hi = rw.astype(BF16)
    rw_lo = (rw - rw_hi.astype(F32)).astype(BF16)
    xn_hi = [xn[s].astype(BF16) for s in subs]
    xn_lo = [(xn[s] - xn_hi[s].astype(F32)).astype(BF16) for s in subs]
    logits = [jnp.dot(xn_hi[s], rw_hi, preferred_element_type=F32) + jnp.dot(xn_lo[s], rw_hi, preferred_element_type=F32)
              + jnp.dot(xn_hi[s], rw_lo, preferred_element_type=F32) + rb_ref[...] for s in subs]
    logits = [jnp.where(lane < N_EXPERTS, logits[s], -jnp.inf) for s in subs]
    vals = [[] for _ in subs]
    hots = [[] for _ in subs]
    for _ in range(TOP_K):
        for s in subs:
            m = jnp.max(logits[s], axis=-1, keepdims=True)
            idx = jnp.min(jnp.where(logits[s] == m, lane_f, float(LANES)), axis=-1, keepdims=True)
            hot = lane_f == idx
            logits[s] = jnp.where(hot, -jnp.inf, logits[s])
            vals[s].append(m)
            hots[s].append((hot, idx))

    ii = lax.broadcasted_iota(jnp.int32, (sub, sub), 0)
    jj = lax.broadcasted_iota(jnp.int32, (sub, sub), 1)
    earlier = jnp.where(ii > jj, 1.0, 0.0).astype(BF16)
    multi = []
    for s in subs:
        m_s = jnp.zeros((sub, LANES), F32)
        for hot, _ in hots[s]:
            m_s = jnp.where(hot, 1.0, m_s)
        multi.append(m_s)
    within = [jnp.dot(earlier, multi[s].astype(BF16), preferred_element_type=F32) for s in subs]
    run = run_ref[0:1, :]
    for s in subs:
        rank_all = within[s] + run
        run = run + jnp.sum(multi[s], axis=0, keepdims=True)
        exps = [jnp.exp(v - vals[s][0]) for v in vals[s]]
        denom = exps[0] + exps[1] + exps[2] + exps[3]
        route = jnp.zeros((sub, LANES), F32)
        for k, (hot, idx) in enumerate(hots[s]):
            rank = jnp.sum(jnp.where(hot, rank_all, 0.0), axis=-1, keepdims=True)
            route = jnp.where(lane == k, idx, route)
            route = jnp.where(lane == TOP_K + k, rank, route)
            route = jnp.where(lane == 2 * TOP_K + k, exps[k] / denom, route)
        route_ref[rows[s], :] = route
    run_ref[...] = jnp.broadcast_to(run, run_ref.shape)
    cnt_ref[...] = jnp.broadcast_to(run, cnt_ref.shape)


def _out_route(xp, osbp, ogdnp, xs, osbs, ogdns, wo_bf, nf_row, rw_pad, rb_row):
    n_p, n_s = xp.shape[0], xs.shape[0]
    tm = min(TM_ROUTE, n_p, n_s)
    steps_p, steps_s = n_p // tm, n_s // tm
    prow = lambda w: pl.BlockSpec((tm, w), lambda i: (jnp.minimum(i, steps_p - 1), 0))
    srow = lambda w: pl.BlockSpec((tm, w), lambda i: (jnp.maximum(i - steps_p, 0), 0))
    orow = lambda w: pl.BlockSpec((tm, w), lambda i: (i, 0))
    const = lambda shape: pl.BlockSpec(shape, lambda i: (0,) * len(shape))
    n_total = n_p + n_s
    out_shape = (jax.ShapeDtypeStruct((n_total, D_MODEL), F32),
                 jax.ShapeDtypeStruct((n_total * ROW_TILE, LANES), F32),
                 jax.ShapeDtypeStruct((n_total, LANES), F32),
                 jax.ShapeDtypeStruct((8, LANES), F32))
    return pl.pallas_call(
        functools.partial(_out_route_kernel, tm=tm, steps_p=steps_p),
        grid=(steps_p + steps_s,),
        in_specs=[prow(D_MODEL), prow(SB_WIDTH), prow(GDN_WIDTH), srow(D_MODEL), srow(SB_WIDTH), srow(GDN_WIDTH),
                  const((D_MODEL, D_MODEL)), const((1, D_MODEL)), const((D_MODEL, LANES)), const((1, LANES))],
        out_specs=(orow(D_MODEL), pl.BlockSpec((tm * ROW_TILE, LANES), lambda i: (i, 0)), orow(LANES),
                   const((8, LANES))),
        out_shape=out_shape,
        scratch_shapes=[pltpu.VMEM((8, LANES), F32)],
        compiler_params=_cparams(("arbitrary",)),
        name="out_route",
    )(xp, osbp, ogdnp, xs, osbs, ogdns, wo_bf, nf_row, rw_pad, rb_row)


def _scatter_rows_kernel(pos_ref, x_ref, xs_hbm, sem, *, tm):
    def row_copy(t, p):
        return pltpu.make_async_copy(x_ref.at[pl.ds(pl.multiple_of(t * ROW_TILE, ROW_TILE), ROW_TILE)],
                                     xs_hbm.at[pl.ds(pl.multiple_of(p * ROW_TILE, ROW_TILE), ROW_TILE)], sem)

    def issue(t, carry):
        for k in range(TOP_K):
            row_copy(t, pos_ref[k, t]).start(priority=k % 2)
        return carry

    lax.fori_loop(0, tm, issue, 0)

    def drain(t, carry):
        for k in range(TOP_K):
            row_copy(0, 0).wait()
        return carry

    lax.fori_loop(0, tm, drain, 0)


def _scatter_rows(pos_t, xn):
    n = xn.shape[0] // ROW_TILE
    tm = min(TM_ROWS, n)
    return pl.pallas_call(
        functools.partial(_scatter_rows_kernel, tm=tm),
        grid=(n // tm,),
        in_specs=[pl.BlockSpec((TOP_K, tm), lambda i: (0, i), memory_space=pltpu.SMEM),
                  pl.BlockSpec((tm * ROW_TILE, LANES), lambda i: (i, 0))],
        out_specs=pl.BlockSpec(memory_space=pl.ANY),
        out_shape=jax.ShapeDtypeStruct((n * TOP_K * ROW_TILE, LANES), xn.dtype),
        scratch_shapes=[pltpu.SemaphoreType.DMA(())],
        compiler_params=_cparams(("arbitrary",)),
        name="moe_scatter",
    )(pos_t, xn)


def _experts_kernel(tile_ref, exp_ref, lo_ref, hi_ref, first_ref, newexp_ref, nvalid_ref,
                    x_ref, wg_ref, bg_ref, wu_ref, bu_ref, wd_ref, bd_ref, y_ref, wg_bf, wu_bf, wd_bf, *, tmg):
    v = pl.program_id(0)

    @pl.when(jnp.logical_and(v < nvalid_ref[0], newexp_ref[v] == 1))
    def _():
        wg_bf[...] = wg_ref[0].astype(BF16)
        wu_bf[...] = wu_ref[0].astype(BF16)
        wd_bf[...] = wd_ref[0].astype(BF16)

    @pl.when(v < nvalid_ref[0])
    def _():
        sub = tmg // EXPERT_SUBTILES
        subs = range(EXPERT_SUBTILES)
        xs = [_load_row_tiles(x_ref, sub, s * sub).astype(BF16) for s in subs]
        gt, up = [], []
        for s in subs:
            gt.append(jnp.dot(xs[s], wg_bf[...], preferred_element_type=F32))
            up.append(jnp.dot(xs[s], wu_bf[...], preferred_element_type=F32))
        hid = []
        for s in subs:
            g = jnp.minimum(gt[s] + bg_ref[0], SWIGLU_LIMIT)
            u = jnp.clip(up[s] + bu_ref[0], -SWIGLU_LIMIT, SWIGLU_LIMIT)
            hid.append(((u + 1.0) * (g * _sigmoid(SWIGLU_ALPHA * g))).astype(BF16))
        ys = [jnp.dot(hid[s], wd_bf[...], preferred_element_type=F32) + bd_ref[0] for s in subs]
        mine = []
        for s in subs:
            rows = tile_ref[v] * tmg + s * sub + lax.broadcasted_iota(jnp.int32, (sub, 1), 0)
            mine.append(jnp.logical_and(rows >= lo_ref[v], rows < hi_ref[v]))

        @pl.when(first_ref[v] == 1)
        def _():
            for s in subs:
                _store_row_tiles(y_ref, jnp.where(mine[s], ys[s], 0.0), sub, s * sub)

        @pl.when(first_ref[v] == 0)
        def _():
            for s in subs:
                _store_row_tiles(y_ref, jnp.where(mine[s], ys[s], _load_row_tiles(y_ref, sub, s * sub)), sub, s * sub)


def _experts(meta, xs, wg, bg, wu, bu, wd, bd):
    n4 = xs.shape[0] // ROW_TILE
    tmg = min(TM_GROUP, n4)
    n_visits = meta[0].shape[0]
    xspec = pl.BlockSpec((tmg * ROW_TILE, LANES), lambda v, tile, exp, *_: (tile[v], 0))
    wspec = pl.BlockSpec((1, D_MODEL, D_MODEL), lambda v, tile, exp, *_: (exp[v], 0, 0))
    bspec = pl.BlockSpec((1, 1, D_MODEL), lambda v, tile, exp, *_: (exp[v], 0, 0))
    grid_spec = pltpu.PrefetchScalarGridSpec(
        num_scalar_prefetch=len(meta),
        grid=(n_visits,),
        in_specs=[xspec, wspec, bspec, wspec, bspec, wspec, bspec],
        out_specs=xspec,
        scratch_shapes=[pltpu.VMEM((D_MODEL, D_MODEL), BF16)] * 3,
    )
    return pl.pallas_call(
        functools.partial(_experts_kernel, tmg=tmg),
        grid_spec=grid_spec,
        out_shape=jax.ShapeDtypeStruct((n4 * ROW_TILE, LANES), F32),
        compiler_params=_cparams(("arbitrary",), VMEM_LIMIT_EXPERTS),
        name="moe_experts",
    )(*meta, xs, wg, bg, wu, bu, wd, bd)


def _combine_kernel(pos_ref, posn_ref, h_ref, route_ref, nfin_ref, ys_hbm, o_ref, buf, sem, *, tm):
    i = pl.program_id(0)
    slot = lax.rem(i, 2)

    def row_copy(s, t, k, p):
        return pltpu.make_async_copy(ys_hbm.at[pl.ds(pl.multiple_of(p * ROW_TILE, ROW_TILE), ROW_TILE)],
                                     buf.at[s, k, pl.ds(pl.multiple_of(t * ROW_TILE, ROW_TILE), ROW_TILE)],
                                     sem.at[s])

    def start_gathers(p_ref, s):
        def issue(t, carry):
            for k in range(TOP_K):
                row_copy(s, t, k, p_ref[k, t]).start(priority=k % 2)
            return carry

        lax.fori_loop(0, tm, issue, 0)

    @pl.when(i == 0)
    def _():
        start_gathers(pos_ref, slot)

    @pl.when(i + 1 < pl.num_programs(0))
    def _():
        start_gathers(posn_ref, 1 - slot)

    def drain(t, carry):
        for k in range(TOP_K):
            row_copy(slot, 0, k, 0).wait()
        return carry

    lax.fori_loop(0, tm, drain, 0)

    route = route_ref[...]
    h = h_ref[...]
    pieces = []
    for j in range(ROW_TILE):
        piece = h[:, j * LANES:(j + 1) * LANES]
        for k in range(TOP_K):
            piece = piece + route[:, 2 * TOP_K + k:2 * TOP_K + k + 1] * buf[slot, k, pl.ds(j, tm, stride=ROW_TILE), :]
        pieces.append(piece)
    out = jnp.concatenate(pieces, axis=-1)
    o_ref[...] = out * lax.rsqrt(jnp.mean(out * out, axis=-1, keepdims=True) + NORM_EPS) * nfin_ref[...]


def _combine(pos_t, h, route, nfin_row, ys, row0, n):
    tm = min(TM_ROWS, n)
    off = row0 // tm
    steps = n // tm
    return pl.pallas_call(
        functools.partial(_combine_kernel, tm=tm),
        grid=(steps,),
        in_specs=[pl.BlockSpec((TOP_K, tm), lambda i: (0, i + off), memory_space=pltpu.SMEM),
                  pl.BlockSpec((TOP_K, tm), lambda i: (0, jnp.minimum(i + 1, steps - 1) + off),
                               memory_space=pltpu.SMEM),
                  pl.BlockSpec((tm, D_MODEL), lambda i: (i + off, 0)),
                  pl.BlockSpec((tm, LANES), lambda i: (i + off, 0)),
                  pl.BlockSpec((1, D_MODEL), lambda i: (0, 0)),
                  pl.BlockSpec(memory_space=pl.ANY)],
        out_specs=pl.BlockSpec((tm, D_MODEL), lambda i: (i, 0)),
        out_shape=jax.ShapeDtypeStruct((n, D_MODEL), F32),
        scratch_shapes=[pltpu.VMEM((2, TOP_K, tm * ROW_TILE, LANES), F32), pltpu.SemaphoreType.DMA((2,))],
        compiler_params=_cparams(("arbitrary",)),
        name="moe_combine",
    )(pos_t, pos_t, h, route, nfin_row, ys)


def _group_plan(counts, n4, tmg):
    n_tiles = n4 // tmg
    n_visits = n_tiles + N_EXPERTS - 1
    ends = jnp.cumsum(counts)
    starts = ends - counts
    t_first = starts // tmg
    t_cnt = jnp.where(counts > 0, (ends - 1) // tmg - t_first + 1, 0)
    v_end = jnp.cumsum(t_cnt)
    v_start = v_end - t_cnt
    total = v_end[-1]
    v = jnp.arange(n_visits, dtype=jnp.int32)
    g = jnp.minimum(jnp.sum((v[:, None] >= v_end[None, :]).astype(jnp.int32), axis=1), N_EXPERTS - 1)
    g_last = jnp.max(jnp.where(counts > 0, jnp.arange(N_EXPERTS), 0)).astype(jnp.int32)
    valid = v < total
    g = jnp.where(valid, g, g_last)
    hot = g[:, None] == jnp.arange(N_EXPERTS, dtype=jnp.int32)[None, :]
    pick = lambda table: jnp.sum(jnp.where(hot, table[None, :], 0), axis=1)
    tile = jnp.where(valid, pick(t_first) + (v - pick(v_start)), n_tiles - 1).astype(jnp.int32)
    lo = jnp.where(valid, jnp.maximum(pick(starts), tile * tmg), 0).astype(jnp.int32)
    hi = jnp.where(valid, jnp.minimum(pick(ends), (tile + 1) * tmg), 0).astype(jnp.int32)
    prev_g = jnp.concatenate([jnp.full((1,), -1, jnp.int32), g[:-1]])
    newexp = (g != prev_g).astype(jnp.int32)
    prev_tile = jnp.concatenate([jnp.full((1,), -1, jnp.int32), tile[:-1]])
    first = (tile != prev_tile).astype(jnp.int32)
    meta = (tile, g, lo, hi, first, newexp, total.reshape(1).astype(jnp.int32))
    return meta, starts


def _pad_rows(a, rows):
    return jnp.concatenate([a, jnp.zeros((rows - a.shape[0],) + a.shape[1:], a.dtype)], axis=0)


def kernel(x_prompt, x_sample, cache_sb_k, cache_sb_v, cache_gdn_conv, state_gdn, norm_mix, w_in, conv_w, a_log,
           dt_bias, norm_sb, norm_gdn, w_out, norm_ffn, router_w, router_b, w_gate, b_gate, w_up, b_up, w_down,
           b_down, norm_final):
    bp, lp, _ = x_prompt.shape
    bs, ls, _ = x_sample.shape
    past = cache_sb_k.shape[2]
    n_p, n_s = bp * lp, bs * ls
    n_tot = n_p + n_s
    assert norm_mix.shape[0] == 1, "single-layer trunk"
    for tile_rows in (TM_DENSE, TM_ROUTE, TM_ROWS):
        assert n_p % min(tile_rows, n_p, n_s) == 0 and n_s % min(tile_rows, n_p, n_s) == 0
    assert (n_tot * TOP_K) % TM_GROUP == 0 and n_tot % TM_ROWS == 0
    assert lp % min(TQ_SB, lp) == 0 and lp % min(CHUNK, lp) == 0 and ls % min(CHUNK, ls) == 0 and ls >= 8

    w_in_bf = jnp.pad(w_in[0], ((0, 0), (0, IN_WIDTH_PAD - IN_WIDTH))).astype(BF16)
    nmix_row = norm_mix[0].reshape(1, D_MODEL)
    nsb_row = jnp.tile(norm_sb[0], 2).reshape(1, LANES)
    ng_row = norm_gdn[0].reshape(1, LANES)
    cw8 = _pad_rows(conv_w[0], 8)
    alog_row = jnp.pad(a_log[0], (0, LANES - GDN_HEADS)).reshape(1, LANES)
    dtb_row = jnp.pad(dt_bias[0], (0, LANES - GDN_HEADS)).reshape(1, LANES)
    wo_bf = w_out[0].astype(BF16)
    nf_row = norm_ffn[0].reshape(1, D_MODEL)
    rw_pad = jnp.pad(router_w[0], ((0, 0), (0, LANES - N_EXPERTS)))
    rb_row = jnp.pad(router_b[0], (0, LANES - N_EXPERTS)).reshape(1, LANES)
    nfin_row = norm_final.reshape(1, D_MODEL)

    xp2 = x_prompt.reshape(n_p, D_MODEL)
    xs2 = x_sample.reshape(n_s, D_MODEL)

    qsb, k_prompt, v_prompt, kbf, vbf, gdn_in, gate, ab = _in_proj(xp2, nmix_row, w_in_bf)
    r3 = lambda a, b, l: a.reshape(b, l, a.shape[-1])
    osb_p = _sb_prompt(r3(qsb, bp, lp), r3(kbf, bp, lp), r3(vbf, bp, lp), nsb_row)
    ogdn_p, conv_p, state_p = _gdn(
        r3(gdn_in, bp, lp), r3(gate, bp, lp), r3(ab, bp, lp),
        jnp.zeros((bp, 8, CONV_CH), F32), jnp.zeros((bp, GDN_HEADS, GDN_HEAD_DIM, GDN_HEAD_DIM), F32),
        cw8, alog_row, dtb_row, ng_row)

    qsb, k_sample, v_sample, kbf, vbf, gdn_in, gate, ab = _in_proj(xs2, nmix_row, w_in_bf)
    keys_minor = lambda cache: jnp.transpose(cache, (0, 1, 3, 4, 2))
    osb_s = _sb_sample(r3(qsb, bs, ls), r3(kbf, bs, ls), r3(vbf, bs, ls), keys_minor(cache_sb_k),
                       keys_minor(cache_sb_v), nsb_row)
    conv_init = jnp.concatenate([jnp.zeros((bs, 8 - (CONV_WIDTH - 1), CONV_CH), F32), cache_gdn_conv[0]], axis=1)
    ogdn_s, conv_s, state_s = _gdn(r3(gdn_in, bs, ls), r3(gate, bs, ls), r3(ab, bs, ls), conv_init, state_gdn[0],
                                   cw8, alog_row, dtb_row, ng_row)

    h_buf, xn_buf, route_buf, cnt = _out_route(
        xp2, osb_p.reshape(n_p, SB_WIDTH), ogdn_p.reshape(n_p, GDN_WIDTH),
        xs2, osb_s.reshape(n_s, SB_WIDTH), ogdn_s.reshape(n_s, GDN_WIDTH), wo_bf, nf_row, rw_pad, rb_row)

    counts = cnt[0, :N_EXPERTS].astype(jnp.int32)
    n4 = n_tot * TOP_K
    tmg = min(TM_GROUP, n4)
    meta, starts = _group_plan(counts, n4, tmg)
    idx_t = route_buf[:, 0:TOP_K].T.astype(jnp.int32)
    rank_t = route_buf[:, TOP_K:2 * TOP_K].T.astype(jnp.int32)
    pos_t = rank_t
    for e in range(N_EXPERTS):
        pos_t = pos_t + jnp.where(idx_t == e, starts[e], 0)

    xs_sorted = _scatter_rows(pos_t, xn_buf)
    e3 = lambda bias: bias[0].reshape(N_EXPERTS, 1, D_MODEL)
    ys_sorted = _experts(meta, xs_sorted, w_gate[0], e3(b_gate), w_up[0], e3(b_up), w_down[0], e3(b_down))
    y_prompt = _combine(pos_t, h_buf, route_buf, nfin_row, ys_sorted, 0, n_p).reshape(bp, lp, D_MODEL)
    y_sample = _combine(pos_t, h_buf, route_buf, nfin_row, ys_sorted, n_p, n_s).reshape(bs, ls, D_MODEL)

    heads = lambda a, b, l: a.reshape(1, b, l, SB_HEADS, SB_HEAD_DIM)
    return (y_prompt, y_sample,
            heads(k_prompt, bp, lp), heads(v_prompt, bp, lp),
            conv_p[:, 8 - (CONV_WIDTH - 1):][None], state_p[None],
            heads(k_sample, bs, ls), heads(v_sample, bs, ls),
            conv_s[:, 8 - (CONV_WIDTH - 1):][None], state_s[None])
```

```python
import functools
import math

import jax
import jax.numpy as jnp
from jax import lax
from jax.experimental import pallas as pl
from jax.experimental.pallas import tpu as pltpu

F32 = jnp.float32
BF16 = jnp.bfloat16

D_MODEL = 1024
SB_HEAD_DIM = 64
SB_WIDTH = 512
SB_HEADS = SB_WIDTH // SB_HEAD_DIM
SB_PAIRS = SB_WIDTH // 128
GDN_HEAD_DIM = 128
GDN_HEADS = 4
GDN_WIDTH = 512
CONV_WIDTH = 4
CONV_CH = 3 * GDN_WIDTH
IN_WIDTH = 3 * SB_WIDTH + 4 * GDN_WIDTH + 2 * GDN_HEADS
IN_WIDTH_PAD = 3 * SB_WIDTH + 4 * GDN_WIDTH + 128
N_EXPERTS = 32
TOP_K = 4
SWIGLU_LIMIT = 7.0
SWIGLU_ALPHA = 1.702
NORM_EPS = 1e-6
L2_EPS = 1e-6
CHUNK = 64

LANES = 128
VMEM_LIMIT_BYTES = 48 * 1024 * 1024
VMEM_LIMIT_EXPERTS = 56 * 1024 * 1024
SB_LOG_CUTOFF = 104.0

TM_DENSE = 512
TM_ROUTE = 512
ROUTE_SUBTILES = 2
TQ_SB = 256
TK_SB = 256
SB_PAIRS_PER_STEP = 4
SB_CACHE_WINDOW = 512
GDN_BATCH_PER_STEP = 4
TM_ROWS = 256
TM_GROUP = 512
EXPERT_SUBTILES = 2


def _cparams(sem, limit=VMEM_LIMIT_BYTES):
    return pltpu.CompilerParams(dimension_semantics=sem, vmem_limit_bytes=limit)


def _softplus(z):
    return jnp.maximum(z, 0.0) + jnp.log(1.0 + jnp.exp(-jnp.abs(z)))


def _sigmoid(z):
    return 1.0 / (1.0 + jnp.exp(-z))


def _store_heads(ref, val, tm):
    for h in range(SB_HEADS):
        ref[pl.ds(h, tm, stride=SB_HEADS), :] = val[:, h * SB_HEAD_DIM:(h + 1) * SB_HEAD_DIM]


ROW_TILE = D_MODEL // LANES


def _store_row_tiles(ref, val, tm, t0=0):
    for j in range(ROW_TILE):
        ref[pl.ds(t0 * ROW_TILE + j, tm, stride=ROW_TILE), :] = val[:, j * LANES:(j + 1) * LANES]


def _load_row_tiles(ref, tm, t0=0):
    return jnp.concatenate([ref[pl.ds(t0 * ROW_TILE + j, tm, stride=ROW_TILE), :] for j in range(ROW_TILE)],
                           axis=-1)


def _in_proj_kernel(x_ref, g_ref, w_ref, qsb_ref, ksb_ref, vsb_ref, kbf_ref, vbf_ref, gdn_ref, gate_ref, ab_ref,
                    *, tm):
    x = x_ref[...]
    xn = x * lax.rsqrt(jnp.mean(x * x, axis=-1, keepdims=True) + NORM_EPS) * g_ref[...]
    xn = xn.astype(BF16)

    def mm(lo, hi):
        return jnp.dot(xn, w_ref[:, lo:hi], preferred_element_type=F32)

    qsb_ref[...] = (mm(0, 512) * (SB_HEAD_DIM ** -0.5)).astype(BF16)
    k = mm(512, 1024)
    _store_heads(ksb_ref, k, tm)
    kbf_ref[...] = k.astype(BF16)
    v = mm(1024, 1536)
    _store_heads(vsb_ref, v, tm)
    vbf_ref[...] = v.astype(BF16)
    for j in range(3):
        gdn_ref[:, j * 512:(j + 1) * 512] = mm(1536 + j * 512, 2048 + j * 512)
    gate_ref[...] = mm(3072, 3584)
    ab_ref[...] = mm(3584, 3712)


def _in_proj(x2d, g_row, w_bf):
    n = x2d.shape[0]
    tm = min(TM_DENSE, n)
    row = lambda w: pl.BlockSpec((tm, w), lambda i: (i, 0))
    head_rows = pl.BlockSpec((tm * SB_HEADS, SB_HEAD_DIM), lambda i: (i, 0))
    out_shape = (
        jax.ShapeDtypeStruct((n, SB_WIDTH), BF16),
        jax.ShapeDtypeStruct((n * SB_HEADS, SB_HEAD_DIM), F32),
        jax.ShapeDtypeStruct((n * SB_HEADS, SB_HEAD_DIM), F32),
        jax.ShapeDtypeStruct((n, SB_WIDTH), BF16),
        jax.ShapeDtypeStruct((n, SB_WIDTH), BF16),
        jax.ShapeDtypeStruct((n, CONV_CH), F32),
        jax.ShapeDtypeStruct((n, GDN_WIDTH), F32),
        jax.ShapeDtypeStruct((n, LANES), F32),
    )
    return pl.pallas_call(
        functools.partial(_in_proj_kernel, tm=tm),
        grid=(n // tm,),
        in_specs=[row(D_MODEL), pl.BlockSpec((1, D_MODEL), lambda i: (0, 0)),
                  pl.BlockSpec((D_MODEL, IN_WIDTH_PAD), lambda i: (0, 0))],
        out_specs=(row(SB_WIDTH), head_rows, head_rows, row(SB_WIDTH), row(SB_WIDTH),
                   row(CONV_CH), row(GDN_WIDTH), row(LANES)),
        out_shape=out_shape,
        compiler_params=_cparams(("arbitrary",)),
        name="in_proj",
    )(x2d, g_row, w_bf)


def _sb_step(qs, k_blks, v_blks, qpos, kpos, acc_ref, c_ref, kv_t=False, causal=True):
    pairs = len(qs)
    chains = [(p, h) for p in range(pairs) for h in range(2)]
    tk = k_blks[0].shape[1 if kv_t else 0]
    lane = lax.broadcasted_iota(jnp.int32, (1, LANES), 1)
    visible = kpos < qpos
    jj = lax.broadcasted_iota(jnp.int32, (tk, tk), 0)
    ss = lax.broadcasted_iota(jnp.int32, (tk, tk), 1)
    tri = jnp.where(jj >= ss, 1.0, 0.0).astype(BF16)
    nt = (((1,), (1,)), ((), ()))

    if kv_t:
        z = {ch: jnp.dot(qs[ch[0]][ch[1]], k_blks[ch[0]], preferred_element_type=F32) for ch in chains}
    else:
        z = {ch: lax.dot_general(qs[ch[0]][ch[1]], k_blks[ch[0]], nt, preferred_element_type=F32) for ch in chains}
    keep = (lambda a: jnp.where(visible, a, 0.0)) if causal else (lambda a: a)
    sp = {ch: keep(_softplus(z[ch])) for ch in chains}
    sp_hi = {ch: sp[ch].astype(BF16) for ch in chains}
    sp_lo = {ch: (sp[ch] - sp_hi[ch].astype(F32)).astype(BF16) for ch in chains}
    r = {ch: jnp.dot(sp_hi[ch], tri, preferred_element_type=F32) + jnp.dot(sp_lo[ch], tri, preferred_element_type=F32)
         for ch in chains}
    def mass(ch):
        c = c_ref[2 * ch[0] + ch[1]]
        return c[:, :tk] if tk <= LANES else jnp.concatenate([c] * (tk // LANES), axis=-1)

    w = {ch: keep(jnp.exp(z[ch] - r[ch] - mass(ch))).astype(BF16) for ch in chains}
    if kv_t:
        pv = {ch: lax.dot_general(w[ch], v_blks[ch[0]], nt, preferred_element_type=F32) for ch in chains}
    else:
        pv = {ch: jnp.dot(w[ch], v_blks[ch[0]], preferred_element_type=F32) for ch in chains}
    cmin = None
    for p in range(pairs):
        acc_ref[p] += jnp.where(lane < SB_HEAD_DIM, pv[(p, 0)], pv[(p, 1)])
    for ch in chains:
        c_new = c_ref[2 * ch[0] + ch[1]] + r[ch][:, 0:1]
        c_ref[2 * ch[0] + ch[1]] = c_new
        m = jnp.min(c_new)
        cmin = m if cmin is None else jnp.minimum(cmin, m)
    return cmin


def _sb_finish(acc, nsb_row):
    lane = lax.broadcasted_iota(jnp.int32, (1, LANES), 1)
    first = lane < SB_HEAD_DIM
    sq = acc * acc
    s_all = jnp.sum(sq, axis=-1, keepdims=True)
    s0 = jnp.sum(jnp.where(first, sq, 0.0), axis=-1, keepdims=True)
    ms = jnp.where(first, s0, s_all - s0) * (1.0 / SB_HEAD_DIM)
    return acc * lax.rsqrt(ms + NORM_EPS) * nsb_row


def _split_heads(q):
    lane = lax.broadcasted_iota(jnp.int32, (1, LANES), 1)
    zero = jnp.zeros_like(q)
    return jnp.where(lane < SB_HEAD_DIM, q, zero), jnp.where(lane >= SB_HEAD_DIM, q, zero)


def _sb_continue(carry):
    j, cmin = carry
    return jnp.logical_and(j >= 0, cmin < SB_LOG_CUTOFF)


def _sb_prompt_kernel(q_ref, k_ref, v_ref, nsb_ref, o_ref, acc_ref, c_ref, *, tq, tk, pairs):
    i = pl.program_id(2)
    lanes = lambda p: slice(p * LANES, (p + 1) * LANES)
    qs = [_split_heads(q_ref[0, :, lanes(p)]) for p in range(pairs)]
    acc_ref[...] = jnp.zeros_like(acc_ref)
    c_ref[...] = jnp.zeros_like(c_ref)
    qpos = i * tq + lax.broadcasted_iota(jnp.int32, (tq, 1), 0)

    def block(j, causal):
        ks = pl.multiple_of(j * tk, tk)
        kpos = ks + lax.broadcasted_iota(jnp.int32, (1, tk), 1)
        k_blks = [k_ref[0, pl.ds(ks, tk), lanes(p)] for p in range(pairs)]
        v_blks = [v_ref[0, pl.ds(ks, tk), lanes(p)] for p in range(pairs)]
        return _sb_step(qs, k_blks, v_blks, qpos, kpos, acc_ref, c_ref, causal=causal)

    j_top = (i + 1) * (tq // tk) - 1
    cmin = jnp.float32(0.0)
    for d in range(tq // tk):
        cmin = block(j_top - d, True)

    def body(carry):
        j, _ = carry
        return j - 1, block(j, False)

    lax.while_loop(_sb_continue, body, (j_top - tq // tk, cmin))
    for p in range(pairs):
        o_ref[0, :, lanes(p)] = _sb_finish(acc_ref[p], nsb_ref[...]).astype(o_ref.dtype)


def _sb_prompt(q, k, v, nsb_row):
    b, l, _ = q.shape
    tq = min(TQ_SB, l)
    tk = min(TK_SB, tq)
    pairs = SB_PAIRS_PER_STEP
    width = pairs * LANES
    qspec = pl.BlockSpec((1, tq, width), lambda bb, hp, i: (bb, i, hp))
    kvspec = pl.BlockSpec((1, l, width), lambda bb, hp, i: (bb, 0, hp))
    return pl.pallas_call(
        functools.partial(_sb_prompt_kernel, tq=tq, tk=tk, pairs=pairs),
        grid=(b, SB_PAIRS // pairs, l // tq),
        in_specs=[qspec, kvspec, kvspec, pl.BlockSpec((1, LANES), lambda bb, hp, i: (0, 0))],
        out_specs=qspec,
        out_shape=jax.ShapeDtypeStruct((b, l, SB_WIDTH), BF16),
        scratch_shapes=[pltpu.VMEM((pairs, tq, LANES), F32), pltpu.VMEM((2 * pairs, tq, LANES), F32)],
        compiler_params=_cparams(("arbitrary", "arbitrary", "arbitrary")),
        name="sb_prompt",
    )(q, k, v, nsb_row)


def _sb_sample_kernel(q_ref, kn_ref, vn_ref, kw_ref, vw_ref, kc_hbm, vc_hbm, nsb_ref, o_ref,
                      acc_ref, c_ref, kbuf, vbuf, sem, *, past, window, tk):
    b = pl.program_id(0)
    hp = pl.program_id(1)
    t = q_ref.shape[1]
    qs = [_split_heads(q_ref[0])]
    acc_ref[...] = jnp.zeros_like(acc_ref)
    c_ref[...] = jnp.zeros_like(c_ref)
    qpos = past + lax.broadcasted_iota(jnp.int32, (t, 1), 0)

    kpos_new = past + lax.broadcasted_iota(jnp.int32, (1, t), 1)
    cmin0 = _sb_step(qs, [kn_ref[0]], [vn_ref[0]], qpos, kpos_new, acc_ref, c_ref)

    def pair_t(blk):
        return blk.reshape(LANES, tk).astype(BF16)

    def wbody(carry):
        j, _ = carry
        ws = pl.multiple_of(j * tk, tk)
        kpos = (past - window) + ws + lax.broadcasted_iota(jnp.int32, (1, tk), 1)
        cmin = _sb_step(qs, [pair_t(kw_ref[:, :, pl.ds(ws, tk)])], [pair_t(vw_ref[:, :, pl.ds(ws, tk)])],
                        qpos, kpos, acc_ref, c_ref, kv_t=True, causal=False)
        return j - 1, cmin

    _, cmin1 = lax.while_loop(_sb_continue, wbody, (window // tk - 1, cmin0))

    n_old = (past - window) // tk
    if n_old > 0:
        def obody(carry):
            j, _ = carry
            keys = pl.ds(pl.multiple_of(j * tk, tk), tk)
            ck = pltpu.make_async_copy(kc_hbm.at[0, b, pl.ds(2 * hp, 2), :, keys], kbuf, sem.at[0])
            cv = pltpu.make_async_copy(vc_hbm.at[0, b, pl.ds(2 * hp, 2), :, keys], vbuf, sem.at[1])
            ck.start()
            cv.start()
            ck.wait()
            cv.wait()
            kpos = j * tk + lax.broadcasted_iota(jnp.int32, (1, tk), 1)
            cmin = _sb_step(qs, [pair_t(kbuf[...])], [pair_t(vbuf[...])], qpos, kpos, acc_ref, c_ref, kv_t=True,
                            causal=False)
            return j - 1, cmin

        lax.while_loop(_sb_continue, obody, (n_old - 1, cmin1))

    o_ref[0] = _sb_finish(acc_ref[0], nsb_ref[...]).astype(o_ref.dtype)


def _sb_sample(q, k_new, v_new, k_cache_t, v_cache_t, nsb_row):
    b, t, _ = q.shape
    past = k_cache_t.shape[4]
    window = min(SB_CACHE_WINDOW, past)
    tk = min(TK_SB, window)
    assert past % window == 0 and window % tk == 0 and (past - window) % tk == 0
    new_spec = pl.BlockSpec((1, t, LANES), lambda bb, hp: (bb, 0, hp))
    win_spec = pl.BlockSpec((None, None, 2, SB_HEAD_DIM, window), lambda bb, hp: (0, bb, hp, 0, past // window - 1))
    any_spec = pl.BlockSpec(memory_space=pl.ANY)
    return pl.pallas_call(
        functools.partial(_sb_sample_kernel, past=past, window=window, tk=tk),
        grid=(b, SB_PAIRS),
        in_specs=[new_spec, new_spec, new_spec, win_spec, win_spec, any_spec, any_spec,
                  pl.BlockSpec((1, LANES), lambda bb, hp: (0, 0))],
        out_specs=new_spec,
        out_shape=jax.ShapeDtypeStruct((b, t, SB_WIDTH), BF16),
        scratch_shapes=[pltpu.VMEM((1, t, LANES), F32), pltpu.VMEM((2, t, LANES), F32),
                        pltpu.VMEM((2, SB_HEAD_DIM, tk), F32), pltpu.VMEM((2, SB_HEAD_DIM, tk), F32),
                        pltpu.SemaphoreType.DMA((2,))],
        compiler_params=_cparams(("arbitrary", "arbitrary")),
        name="sb_sample",
    )(q, k_new, v_new, k_cache_t, v_cache_t, k_cache_t, v_cache_t, nsb_row)


def _gdn_chunk(nb, qkv_ref, gate_ref, ab_ref, cw_ref, alog_ref, dtb_ref, ng_ref, o_ref, xp_ref, s_ref, chunk):
    ii = lax.broadcasted_iota(jnp.int32, (chunk, chunk), 0)
    jj = lax.broadcasted_iota(jnp.int32, (chunk, chunk), 1)
    tri = jnp.where(ii >= jj, 1.0, 0.0).astype(F32)
    scale = GDN_HEAD_DIM ** -0.5
    hd = GDN_HEAD_DIM
    n_fac = int(math.log2(chunk)) - 1
    chains = [(bi, h) for bi in range(nb) for h in range(GDN_HEADS)]

    ys, gcs, gcts, betas = [], [], [], []
    for bi in range(nb):
        xp_ref[bi, 8:8 + chunk, :] = qkv_ref[bi]
        y = cw_ref[0:1, :] * xp_ref[bi, 5:5 + chunk, :]
        for i in range(1, CONV_WIDTH):
            y = y + cw_ref[i:i + 1, :] * xp_ref[bi, 5 + i:5 + i + chunk, :]
        ys.append(y * _sigmoid(y))
        xp_ref[bi, 0:8, :] = xp_ref[bi, chunk:chunk + 8, :]
        ab = ab_ref[bi]
        g_tile = -jnp.exp(alog_ref[...]) * _softplus(ab + dtb_ref[...])
        betas.append(_sigmoid(ab))
        gc_tile = jnp.dot(tri, g_tile, preferred_element_type=F32, precision=lax.Precision.HIGHEST)
        gcs.append(gc_tile)
        gcts.append(gc_tile.T)

    kn, qs, kb, vb, decay, egc, glast, gcol = {}, {}, {}, {}, {}, {}, {}, {}
    for ch in chains:
        bi, h = ch
        y = ys[bi]
        qh = y[:, h * hd:(h + 1) * hd]
        kh = y[:, GDN_WIDTH + h * hd:GDN_WIDTH + (h + 1) * hd]
        vh = y[:, 2 * GDN_WIDTH + h * hd:2 * GDN_WIDTH + (h + 1) * hd]
        qs[ch] = qh * lax.rsqrt(jnp.sum(qh * qh, axis=-1, keepdims=True) + L2_EPS) * scale
        kn[ch] = kh * lax.rsqrt(jnp.sum(kh * kh, axis=-1, keepdims=True) + L2_EPS)
        gcol[ch] = gcs[bi][:, h:h + 1]
        grow = gcts[bi][h:h + 1, :]
        glast[ch] = gcs[bi][chunk - 1:chunk, h:h + 1]
        bcol = betas[bi][:, GDN_HEADS + h:GDN_HEADS + h + 1]
        decay[ch] = jnp.exp(jnp.where(ii >= jj, gcol[ch] - grow, -1e30))
        egc[ch] = jnp.exp(gcol[ch])
        kb[ch] = kn[ch] * bcol
        vb[ch] = vh * bcol

    a2 = {ch: lax.dot_general(jnp.concatenate([kb[ch], qs[ch]], axis=0).astype(BF16), kn[ch].astype(BF16),
                              (((1,), (1,)), ((), ())), preferred_element_type=F32) for ch in chains}
    lb = {ch: jnp.where(ii > jj, a2[ch][:chunk] * decay[ch], 0.0).astype(BF16) for ch in chains}
    qk = {ch: (a2[ch][chunk:] * decay[ch]).astype(BF16) for ch in chains}

    rhs = {ch: jnp.concatenate([vb[ch], kb[ch] * egc[ch]], axis=1) for ch in chains}
    sol = {ch: rhs[ch] - jnp.dot(lb[ch], rhs[ch].astype(BF16), preferred_element_type=F32) for ch in chains}
    pw = {ch: jnp.dot(lb[ch], lb[ch], preferred_element_type=F32).astype(BF16) for ch in chains}
    for f in range(n_fac):
        sol = {ch: sol[ch] + jnp.dot(pw[ch], sol[ch].astype(BF16), preferred_element_type=F32) for ch in chains}
        if f + 1 < n_fac:
            pw = {ch: jnp.dot(pw[ch], pw[ch], preferred_element_type=F32).astype(BF16) for ch in chains}

    s_old = {ch: s_ref[ch[0], ch[1]] for ch in chains}
    ws = {ch: jnp.dot(jnp.concatenate([sol[ch][:, hd:], qs[ch] * egc[ch]], axis=0).astype(BF16),
                      s_old[ch].astype(BF16), preferred_element_type=F32) for ch in chains}
    v_new = {ch: (sol[ch][:, :hd] - ws[ch][:chunk]).astype(BF16) for ch in chains}
    o = {ch: ws[ch][chunk:] + jnp.dot(qk[ch], v_new[ch], preferred_element_type=F32) for ch in chains}
    for ch in chains:
        kd = (kn[ch] * jnp.exp(glast[ch] - gcol[ch])).astype(BF16)
        s_ref[ch[0], ch[1]] = s_old[ch] * jnp.exp(glast[ch]) + lax.dot_general(
            kd, v_new[ch], (((0,), (0,)), ((), ())), preferred_element_type=F32)

    for ch in chains:
        bi, h = ch
        sl = slice(h * hd, (h + 1) * hd)
        on = o[ch] * lax.rsqrt(jnp.mean(o[ch] * o[ch], axis=-1, keepdims=True) + NORM_EPS) * ng_ref[...]
        gt = gate_ref[bi][:, sl]
        o_ref[bi, :, sl] = (on * (gt * _sigmoid(gt))).astype(o_ref.dtype)


def _gdn_kernel(qkv_ref, gate_ref, ab_ref, cinit_ref, sinit_ref, cw_ref, alog_ref, dtb_ref, ng_ref,
                o_ref, cout_ref, sout_ref, xp_ref, s_ref, *, chunk, nb):
    c = pl.program_id(1)

    @pl.when(c == 0)
    def _():
        xp_ref[:, 0:8, :] = cinit_ref[...]
        s_ref[...] = sinit_ref[...]

    _gdn_chunk(nb, qkv_ref, gate_ref, ab_ref, cw_ref, alog_ref, dtb_ref, ng_ref, o_ref, xp_ref, s_ref, chunk)

    @pl.when(c == pl.num_programs(1) - 1)
    def _():
        cout_ref[...] = xp_ref[:, 0:8, :]
        sout_ref[...] = s_ref[...]


def _gdn(qkv, gate, ab, conv_init, s_init, cw8, alog_row, dtb_row, ng_row):
    b, l, _ = qkv.shape
    chunk = min(CHUNK, l)
    nc = l // chunk
    nb = min(GDN_BATCH_PER_STEP, b)
    assert b % nb == 0
    tok = lambda w: pl.BlockSpec((nb, chunk, w), lambda bb, c: (bb, c, 0))
    const = lambda shape: pl.BlockSpec(shape, lambda bb, c: (0,) * len(shape))
    per_b3 = pl.BlockSpec((nb, 8, CONV_CH), lambda bb, c: (bb, 0, 0))
    per_b4 = pl.BlockSpec((nb, GDN_HEADS, GDN_HEAD_DIM, GDN_HEAD_DIM), lambda bb, c: (bb, 0, 0, 0))
    return pl.pallas_call(
        functools.partial(_gdn_kernel, chunk=chunk, nb=nb),
        grid=(b // nb, nc),
        in_specs=[tok(CONV_CH), tok(GDN_WIDTH), tok(LANES), per_b3, per_b4,
                  const((8, CONV_CH)), const((1, LANES)), const((1, LANES)), const((1, LANES))],
        out_specs=(tok(GDN_WIDTH), per_b3, per_b4),
        out_shape=(jax.ShapeDtypeStruct((b, l, GDN_WIDTH), BF16),
                   jax.ShapeDtypeStruct((b, 8, CONV_CH), F32),
                   jax.ShapeDtypeStruct((b, GDN_HEADS, GDN_HEAD_DIM, GDN_HEAD_DIM), F32)),
        scratch_shapes=[pltpu.VMEM((nb, chunk + 8, CONV_CH), F32),
                        pltpu.VMEM((nb, GDN_HEADS, GDN_HEAD_DIM, GDN_HEAD_DIM), F32)],
        compiler_params=_cparams(("arbitrary", "arbitrary")),
        name="gdn",
    )(qkv, gate, ab, conv_init, s_init, cw8, alog_row, dtb_row, ng_row)


def _out_route_kernel(xp_ref, osbp_ref, ogdnp_ref, xs_ref, osbs_ref, ogdns_ref, wo_ref, nf_ref, rw_ref, rb_ref,
                      h_ref, xn_ref, route_ref, cnt_ref, run_ref, *, tm, steps_p):
    i = pl.program_id(0)

    @pl.when(i == 0)
    def _():
        run_ref[...] = jnp.zeros_like(run_ref)

    is_p = i < steps_p
    sub = tm // ROUTE_SUBTILES
    subs = range(ROUTE_SUBTILES)
    rows = [pl.ds(s * sub, sub) for s in subs]
    lane = lax.broadcasted_iota(jnp.int32, (1, LANES), 1)
    lane_f = lane.astype(F32)

    x = [jnp.where(is_p, xp_ref[r, :], xs_ref[r, :]) for r in rows]
    osb = [jnp.where(is_p, osbp_ref[r, :], osbs_ref[r, :]) for r in rows]
    ogdn = [jnp.where(is_p, ogdnp_ref[r, :], ogdns_ref[r, :]) for r in rows]
    h = [x[s] + jnp.dot(osb[s], wo_ref[0:SB_WIDTH, :], preferred_element_type=F32)
         + jnp.dot(ogdn[s], wo_ref[SB_WIDTH:, :], preferred_element_type=F32) for s in subs]
    for s in subs:
        h_ref[rows[s], :] = h[s]
    xn = [h[s] * lax.rsqrt(jnp.mean(h[s] * h[s], axis=-1, keepdims=True) + NORM_EPS) * nf_ref[...] for s in subs]
    for s in subs:
        _store_row_tiles(xn_ref, xn[s], sub, s * sub)

    rw = rw_ref[...]
    rw_hi = rw.astype(BF16)
    rw_lo = (rw - rw_hi.astype(F32)).astype(BF16)
    xn_hi = [xn[s].astype(BF16) for s in subs]
    xn_lo = [(xn[s] - xn_hi[s].astype(F32)).astype(BF16) for s in subs]
    logits = [jnp.dot(xn_hi[s], rw_hi, preferred_element_type=F32) + jnp.dot(xn_lo[s], rw_hi, preferred_element_type=F32)
              + jnp.dot(xn_hi[s], rw_lo, preferred_element_type=F32) + rb_ref[...] for s in subs]
    logits = [jnp.where(lane < N_EXPERTS, logits[s], -jnp.inf) for s in subs]
    vals = [[] for _ in subs]
    hots = [[] for _ in subs]
    for _ in range(TOP_K):
        for s in subs:
            m = jnp.max(logits[s], axis=-1, keepdims=True)
            idx = jnp.min(jnp.where(logits[s] == m, lane_f, float(LANES)), axis=-1, keepdims=True)
            hot = lane_f == idx
            logits[s] = jnp.where(hot, -jnp.inf, logits[s])
            vals[s].append(m)
            hots[s].append((hot, idx))

    ii = lax.broadcasted_iota(jnp.int32, (sub, sub), 0)
    jj = lax.broadcasted_iota(jnp.int32, (sub, sub), 1)
    earlier = jnp.where(ii > jj, 1.0, 0.0).astype(BF16)
    multi = []
    for s in subs:
        m_s = jnp.zeros((sub, LANES), F32)
        for hot, _ in hots[s]:
            m_s = jnp.where(hot, 1.0, m_s)
        multi.append(m_s)
    within = [jnp.dot(earlier, multi[s].astype(BF16), preferred_element_type=F32) for s in subs]
    run = run_ref[0:1, :]
    for s in subs:
        rank_all = within[s] + run
        run = run + jnp.sum(multi[s], axis=0, keepdims=True)
        exps = [jnp.exp(v - vals[s][0]) for v in vals[s]]
        denom = exps[0] + exps[1] + exps[2] + exps[3]
        route = jnp.zeros((sub, LANES), F32)
        for k, (hot, idx) in enumerate(hots[s]):
            rank = jnp.sum(jnp.where(hot, rank_all, 0.0), axis=-1, keepdims=True)
            route = jnp.where(lane == k, idx, route)
            route = jnp.where(lane == TOP_K + k, rank, route)
            route = jnp.where(lane == 2 * TOP_K + k, exps[k] / denom, route)
        route_ref[rows[s], :] = route
    run_ref[...] = jnp.broadcast_to(run, run_ref.shape)
    cnt_ref[...] = jnp.broadcast_to(run, cnt_ref.shape)


def _out_route(xp, osbp, ogdnp, xs, osbs, ogdns, wo_bf, nf_row, rw_pad, rb_row):
    n_p, n_s = xp.shape[0], xs.shape[0]
    tm = min(TM_ROUTE, n_p, n_s)
    steps_p, steps_s = n_p // tm, n_s // tm
    prow = lambda w: pl.BlockSpec((tm, w), lambda i: (jnp.minimum(i, steps_p - 1), 0))
    srow = lambda w: pl.BlockSpec((tm, w), lambda i: (jnp.maximum(i - steps_p, 0), 0))
    orow = lambda w: pl.BlockSpec((tm, w), lambda i: (i, 0))
    const = lambda shape: pl.BlockSpec(shape, lambda i: (0,) * len(shape))
    n_total = n_p + n_s
    out_shape = (jax.ShapeDtypeStruct((n_total, D_MODEL), F32),
                 jax.ShapeDtypeStruct((n_total * ROW_TILE, LANES), F32),
                 jax.ShapeDtypeStruct((n_total, LANES), F32),
                 jax.ShapeDtypeStruct((8, LANES), F32))
    return pl.pallas_call(
        functools.partial(_out_route_kernel, tm=tm, steps_p=steps_p),
        grid=(steps_p + steps_s,),
        in_specs=[prow(D_MODEL), prow(SB_WIDTH), prow(GDN_WIDTH), srow(D_MODEL), srow(SB_WIDTH), srow(GDN_WIDTH),
                  const((D_MODEL, D_MODEL)), const((1, D_MODEL)), const((D_MODEL, LANES)), const((1, LANES))],
        out_specs=(orow(D_MODEL), pl.BlockSpec((tm * ROW_TILE, LANES), lambda i: (i, 0)), orow(LANES),
                   const((8, LANES))),
        out_shape=out_shape,
        scratch_shapes=[pltpu.VMEM((8, LANES), F32)],
        compiler_params=_cparams(("arbitrary",)),
        name="out_route",
    )(xp, osbp, ogdnp, xs, osbs, ogdns, wo_bf, nf_row, rw_pad, rb_row)


def _scatter_rows_kernel(pos_ref, x_ref, xs_hbm, sem, *, tm):
    def row_copy(t, p):
        return pltpu.make_async_copy(x_ref.at[pl.ds(pl.multiple_of(t * ROW_TILE, ROW_TILE), ROW_TILE)],
                                     xs_hbm.at[pl.ds(pl.multiple_of(p * ROW_TILE, ROW_TILE), ROW_TILE)], sem)

    def issue(t, carry):
        for k in range(TOP_K):
            row_copy(t, pos_ref[k, t]).start(priority=k % 2)
        return carry

    lax.fori_loop(0, tm, issue, 0)

    def drain(t, carry):
        for k in range(TOP_K):
            row_copy(0, 0).wait()
        return carry

    lax.fori_loop(0, tm, drain, 0)


def _scatter_rows(pos_t, xn):
    n = xn.shape[0] // ROW_TILE
    tm = min(TM_ROWS, n)
    return pl.pallas_call(
        functools.partial(_scatter_rows_kernel, tm=tm),
        grid=(n // tm,),
        in_specs=[pl.BlockSpec((TOP_K, tm), lambda i: (0, i), memory_space=pltpu.SMEM),
                  pl.BlockSpec((tm * ROW_TILE, LANES), lambda i: (i, 0))],
        out_specs=pl.BlockSpec(memory_space=pl.ANY),
        out_shape=jax.ShapeDtypeStruct((n * TOP_K * ROW_TILE, LANES), xn.dtype),
        scratch_shapes=[pltpu.SemaphoreType.DMA(())],
        compiler_params=_cparams(("arbitrary",)),
        name="moe_scatter",
    )(pos_t, xn)


def _experts_kernel(tile_ref, exp_ref, lo_ref, hi_ref, first_ref, newexp_ref, nvalid_ref,
                    x_ref, wg_ref, bg_ref, wu_ref, bu_ref, wd_ref, bd_ref, y_ref, wg_bf, wu_bf, wd_bf, *, tmg):
    v = pl.program_id(0)

    @pl.when(jnp.logical_and(v < nvalid_ref[0], newexp_ref[v] == 1))
    def _():
        wg_bf[...] = wg_ref[0].astype(BF16)
        wu_bf[...] = wu_ref[0].astype(BF16)
        wd_bf[...] = wd_ref[0].astype(BF16)

    @pl.when(v < nvalid_ref[0])
    def _():
        sub = tmg // EXPERT_SUBTILES
        subs = range(EXPERT_SUBTILES)
        xs = [_load_row_tiles(x_ref, sub, s * sub).astype(BF16) for s in subs]
        gt, up = [], []
        for s in subs:
            gt.append(jnp.dot(xs[s], wg_bf[...], preferred_element_type=F32))
            up.append(jnp.dot(xs[s], wu_bf[...], preferred_element_type=F32))
        hid = []
        for s in subs:
            g = jnp.minimum(gt[s] + bg_ref[0], SWIGLU_LIMIT)
            u = jnp.clip(up[s] + bu_ref[0], -SWIGLU_LIMIT, SWIGLU_LIMIT)
            hid.append(((u + 1.0) * (g * _sigmoid(SWIGLU_ALPHA * g))).astype(BF16))
        ys = [jnp.dot(hid[s], wd_bf[...], preferred_element_type=F32) + bd_ref[0] for s in subs]
        mine = []
        for s in subs:
            rows = tile_ref[v] * tmg + s * sub + lax.broadcasted_iota(jnp.int32, (sub, 1), 0)
            mine.append(jnp.logical_and(rows >= lo_ref[v], rows < hi_ref[v]))

        @pl.when(first_ref[v] == 1)
        def _():
            for s in subs:
                _store_row_tiles(y_ref, jnp.where(mine[s], ys[s], 0.0), sub, s * sub)

        @pl.when(first_ref[v] == 0)
        def _():
            for s in subs:
                _store_row_tiles(y_ref, jnp.where(mine[s], ys[s], _load_row_tiles(y_ref, sub, s * sub)), sub, s * sub)


def _experts(meta, xs, wg, bg, wu, bu, wd, bd):
    n4 = xs.shape[0] // ROW_TILE
    tmg = min(TM_GROUP, n4)
    n_visits = meta[0].shape[0]
    xspec = pl.BlockSpec((tmg * ROW_TILE, LANES), lambda v, tile, exp, *_: (tile[v], 0))
    wspec = pl.BlockSpec((1, D_MODEL, D_MODEL), lambda v, tile, exp, *_: (exp[v], 0, 0))
    bspec = pl.BlockSpec((1, 1, D_MODEL), lambda v, tile, exp, *_: (exp[v], 0, 0))
    grid_spec = pltpu.PrefetchScalarGridSpec(
        num_scalar_prefetch=len(meta),
        grid=(n_visits,),
        in_specs=[xspec, wspec, bspec, wspec, bspec, wspec, bspec],
        out_specs=xspec,
        scratch_shapes=[pltpu.VMEM((D_MODEL, D_MODEL), BF16)] * 3,
    )
    return pl.pallas_call(
        functools.partial(_experts_kernel, tmg=tmg),
        grid_spec=grid_spec,
        out_shape=jax.ShapeDtypeStruct((n4 * ROW_TILE, LANES), F32),
        compiler_params=_cparams(("arbitrary",), VMEM_LIMIT_EXPERTS),
        name="moe_experts",
    )(*meta, xs, wg, bg, wu, bu, wd, bd)


def _combine_kernel(pos_ref, posn_ref, h_ref, route_ref, nfin_ref, ys_hbm, o_ref, buf, sem, *, tm):
    i = pl.program_id(0)
    slot = lax.rem(i, 2)

    def row_copy(s, t, k, p):
        return pltpu.make_async_copy(ys_hbm.at[pl.ds(pl.multiple_of(p * ROW_TILE, ROW_TILE), ROW_TILE)],
                                     buf.at[s, k, pl.ds(pl.multiple_of(t * ROW_TILE, ROW_TILE), ROW_TILE)],
                                     sem.at[s])

    def start_gathers(p_ref, s):
        def issue(t, carry):
            for k in range(TOP_K):
                row_copy(s, t, k, p_ref[k, t]).start(priority=k % 2)
            return carry

        lax.fori_loop(0, tm, issue, 0)

    @pl.when(i == 0)
    def _():
        start_gathers(pos_ref, slot)

    @pl.when(i + 1 < pl.num_programs(0))
    def _():
        start_gathers(posn_ref, 1 - slot)

    def drain(t, carry):
        for k in range(TOP_K):
            row_copy(slot, 0, k, 0).wait()
        return carry

    lax.fori_loop(0, tm, drain, 0)

    route = route_ref[...]
    h = h_ref[...]
    pieces = []
    for j in range(ROW_TILE):
        piece = h[:, j * LANES:(j + 1) * LANES]
        for k in range(TOP_K):
            piece = piece + route[:, 2 * TOP_K + k:2 * TOP_K + k + 1] * buf[slot, k, pl.ds(j, tm, stride=ROW_TILE), :]
        pieces.append(piece)
    out = jnp.concatenate(pieces, axis=-1)
    o_ref[...] = out * lax.rsqrt(jnp.mean(out * out, axis=-1, keepdims=True) + NORM_EPS) * nfin_ref[...]


def _combine(pos_t, h, route, nfin_row, ys, row0, n):
    tm = min(TM_ROWS, n)
    off = row0 // tm
    steps = n // tm
    return pl.pallas_call(
        functools.partial(_combine_kernel, tm=tm),
        grid=(steps,),
        in_specs=[pl.BlockSpec((TOP_K, tm), lambda i: (0, i + off), memory_space=pltpu.SMEM),
                  pl.BlockSpec((TOP_K, tm), lambda i: (0, jnp.minimum(i + 1, steps - 1) + off),
                               memory_space=pltpu.SMEM),
                  pl.BlockSpec((tm, D_MODEL), lambda i: (i + off, 0)),
                  pl.BlockSpec((tm, LANES), lambda i: (i + off, 0)),
                  pl.BlockSpec((1, D_MODEL), lambda i: (0, 0)),
                  pl.BlockSpec(memory_space=pl.ANY)],
        out_specs=pl.BlockSpec((tm, D_MODEL), lambda i: (i, 0)),
        out_shape=jax.ShapeDtypeStruct((n, D_MODEL), F32),
        scratch_shapes=[pltpu.VMEM((2, TOP_K, tm * ROW_TILE, LANES), F32), pltpu.SemaphoreType.DMA((2,))],
        compiler_params=_cparams(("arbitrary",)),
        name="moe_combine",
    )(pos_t, pos_t, h, route, nfin_row, ys)


def _group_plan(counts, n4, tmg):
    n_tiles = n4 // tmg
    n_visits = n_tiles + N_EXPERTS - 1
    ends = jnp.cumsum(counts)
    starts = ends - counts
    t_first = starts // tmg
    t_cnt = jnp.where(counts > 0, (ends - 1) // tmg - t_first + 1, 0)
    v_end = jnp.cumsum(t_cnt)
    v_start = v_end - t_cnt
    total = v_end[-1]
    v = jnp.arange(n_visits, dtype=jnp.int32)
    g = jnp.minimum(jnp.sum((v[:, None] >= v_end[None, :]).astype(jnp.int32), axis=1), N_EXPERTS - 1)
    g_last = jnp.max(jnp.where(counts > 0, jnp.arange(N_EXPERTS), 0)).astype(jnp.int32)
    valid = v < total
    g = jnp.where(valid, g, g_last)
    hot = g[:, None] == jnp.arange(N_EXPERTS, dtype=jnp.int32)[None, :]
    pick = lambda table: jnp.sum(jnp.where(hot, table[None, :], 0), axis=1)
    tile = jnp.where(valid, pick(t_first) + (v - pick(v_start)), n_tiles - 1).astype(jnp.int32)
    lo = jnp.where(valid, jnp.maximum(pick(starts), tile * tmg), 0).astype(jnp.int32)
    hi = jnp.where(valid, jnp.minimum(pick(ends), (tile + 1) * tmg), 0).astype(jnp.int32)
    prev_g = jnp.concatenate([jnp.full((1,), -1, jnp.int32), g[:-1]])
    newexp = (g != prev_g).astype(jnp.int32)
    prev_tile = jnp.concatenate([jnp.full((1,), -1, jnp.int32), tile[:-1]])
    first = (tile != prev_tile).astype(jnp.int32)
    meta = (tile, g, lo, hi, first, newexp, total.reshape(1).astype(jnp.int32))
    return meta, starts


def _pad_rows(a, rows):
    return jnp.concatenate([a, jnp.zeros((rows - a.shape[0],) + a.shape[1:], a.dtype)], axis=0)


def kernel(x_prompt, x_sample, cache_sb_k, cache_sb_v, cache_gdn_conv, state_gdn, norm_mix, w_in, conv_w, a_log,
           dt_bias, norm_sb, norm_gdn, w_out, norm_ffn, router_w, router_b, w_gate, b_gate, w_up, b_up, w_down,
           b_down, norm_final):
    bp, lp, _ = x_prompt.shape
    bs, ls, _ = x_sample.shape
    past = cache_sb_k.shape[2]
    n_p, n_s = bp * lp, bs * ls
    n_tot = n_p + n_s
    assert norm_mix.shape[0] == 1, "single-layer trunk"
    for tile_rows in (TM_DENSE, TM_ROUTE, TM_ROWS):
        assert n_p % min(tile_rows, n_p, n_s) == 0 and n_s % min(tile_rows, n_p, n_s) == 0
    assert (n_tot * TOP_K) % TM_GROUP == 0 and n_tot % TM_ROWS == 0
    assert lp % min(TQ_SB, lp) == 0 and lp % min(CHUNK, lp) == 0 and ls % min(CHUNK, ls) == 0 and ls >= 8

    w_in_bf = jnp.pad(w_in[0], ((0, 0), (0, IN_WIDTH_PAD - IN_WIDTH))).astype(BF16)
    nmix_row = norm_mix[0].reshape(1, D_MODEL)
    nsb_row = jnp.tile(norm_sb[0], 2).reshape(1, LANES)
    ng_row = norm_gdn[0].reshape(1, LANES)
    cw8 = _pad_rows(conv_w[0], 8)
    alog_row = jnp.pad(a_log[0], (0, LANES - GDN_HEADS)).reshape(1, LANES)
    dtb_row = jnp.pad(dt_bias[0], (0, LANES - GDN_HEADS)).reshape(1, LANES)
    wo_bf = w_out[0].astype(BF16)
    nf_row = norm_ffn[0].reshape(1, D_MODEL)
    rw_pad = jnp.pad(router_w[0], ((0, 0), (0, LANES - N_EXPERTS)))
    rb_row = jnp.pad(router_b[0], (0, LANES - N_EXPERTS)).reshape(1, LANES)
    nfin_row = norm_final.reshape(1, D_MODEL)

    xp2 = x_prompt.reshape(n_p, D_MODEL)
    xs2 = x_sample.reshape(n_s, D_MODEL)

    qsb, k_prompt, v_prompt, kbf, vbf, gdn_in, gate, ab = _in_proj(xp2, nmix_row, w_in_bf)
    r3 = lambda a, b, l: a.reshape(b, l, a.shape[-1])
    osb_p = _sb_prompt(r3(qsb, bp, lp), r3(kbf, bp, lp), r3(vbf, bp, lp), nsb_row)
    ogdn_p, conv_p, state_p = _gdn(
        r3(gdn_in, bp, lp), r3(gate, bp, lp), r3(ab, bp, lp),
        jnp.zeros((bp, 8, CONV_CH), F32), jnp.zeros((bp, GDN_HEADS, GDN_HEAD_DIM, GDN_HEAD_DIM), F32),
        cw8, alog_row, dtb_row, ng_row)

    qsb, k_sample, v_sample, kbf, vbf, gdn_in, gate, ab = _in_proj(xs2, nmix_row, w_in_bf)
    keys_minor = lambda cache: jnp.transpose(cache, (0, 1, 3, 4, 2))
    osb_s = _sb_sample(r3(qsb, bs, ls), r3(kbf, bs, ls), r3(vbf, bs, ls), keys_minor(cache_sb_k),
                       keys_minor(cache_sb_v), nsb_row)
    conv_init = jnp.concatenate([jnp.zeros((bs, 8 - (CONV_WIDTH - 1), CONV_CH), F32), cache_gdn_conv[0]], axis=1)
    ogdn_s, conv_s, state_s = _gdn(r3(gdn_in, bs, ls), r3(gate, bs, ls), r3(ab, bs, ls), conv_init, state_gdn[0],
                                   cw8, alog_row, dtb_row, ng_row)

    h_buf, xn_buf, route_buf, cnt = _out_route(
        xp2, osb_p.reshape(n_p, SB_WIDTH), ogdn_p.reshape(n_p, GDN_WIDTH),
        xs2, osb_s.reshape(n_s, SB_WIDTH), ogdn_s.reshape(n_s, GDN_WIDTH), wo_bf, nf_row, rw_pad, rb_row)

    counts = cnt[0, :N_EXPERTS].astype(jnp.int32)
    n4 = n_tot * TOP_K
    tmg = min(TM_GROUP, n4)
    meta, starts = _group_plan(counts, n4, tmg)
    idx_t = route_buf[:, 0:TOP_K].T.astype(jnp.int32)
    rank_t = route_buf[:, TOP_K:2 * TOP_K].T.astype(jnp.int32)
    pos_t = rank_t
    for e in range(N_EXPERTS):
        pos_t = pos_t + jnp.where(idx_t == e, starts[e], 0)

    xs_sorted = _scatter_rows(pos_t, xn_buf)
    e3 = lambda bias: bias[0].reshape(N_EXPERTS, 1, D_MODEL)
    ys_sorted = _experts(meta, xs_sorted, w_gate[0], e3(b_gate), w_up[0], e3(b_up), w_down[0], e3(b_down))
    y_prompt = _combine(pos_t, h_buf, route_buf, nfin_row, ys_sorted, 0, n_p).reshape(bp, lp, D_MODEL)
    y_sample = _combine(pos_t, h_buf, route_buf, nfin_row, ys_sorted, n_p, n_s).reshape(bs, ls, D_MODEL)

    heads = lambda a, b, l: a.reshape(1, b, l, SB_HEADS, SB_HEAD_DIM)
    return (y_prompt, y_sample,
            heads(k_prompt, bp, lp), heads(v_prompt, bp, lp),
            conv_p[:, 8 - (CONV_WIDTH - 1):][None], state_p[None],
            heads(k_sample, bs, ls), heads(v_sample, bs, ls),
            conv_s[:, 8 - (CONV_WIDTH - 1):][None], state_s[None])
```

```python
import functools
import math

import jax
import jax.numpy as jnp
from jax import lax
from jax.experimental import pallas as pl
from jax.experimental.pallas import tpu as pltpu

F32 = jnp.float32
BF16 = jnp.bfloat16

D_MODEL = 1024
SB_HEAD_DIM = 64
SB_WIDTH = 512
SB_HEADS = SB_WIDTH // SB_HEAD_DIM
SB_PAIRS = SB_WIDTH // 128
GDN_HEAD_DIM = 128
GDN_HEADS = 4
GDN_WIDTH = 512
CONV_WIDTH = 4
CONV_CH = 3 * GDN_WIDTH
IN_WIDTH = 3 * SB_WIDTH + 4 * GDN_WIDTH + 2 * GDN_HEADS
IN_WIDTH_PAD = 3 * SB_WIDTH + 4 * GDN_WIDTH + 128
N_EXPERTS = 32
TOP_K = 4
SWIGLU_LIMIT = 7.0
SWIGLU_ALPHA = 1.702
NORM_EPS = 1e-6
L2_EPS = 1e-6
CHUNK = 64

LANES = 128
VMEM_LIMIT_BYTES = 48 * 1024 * 1024
VMEM_LIMIT_EXPERTS = 56 * 1024 * 1024
SB_LOG_CUTOFF = 104.0

TM_DENSE = 512
TM_ROUTE = 512
ROUTE_SUBTILES = 2
TQ_SB = 256
TK_SB = 256
SB_PAIRS_PER_STEP = 4
SB_CACHE_WINDOW = 512
GDN_BATCH_PER_STEP = 4
TM_ROWS = 256
TM_GROUP = 512
EXPERT_SUBTILES = 2


def _cparams(sem, limit=VMEM_LIMIT_BYTES):
    return pltpu.CompilerParams(dimension_semantics=sem, vmem_limit_bytes=limit)


def _softplus(z):
    return jnp.maximum(z, 0.0) + jnp.log(1.0 + jnp.exp(-jnp.abs(z)))


def _sigmoid(z):
    return 1.0 / (1.0 + jnp.exp(-z))


def _store_heads(ref, val, tm):
    for h in range(SB_HEADS):
        ref[pl.ds(h, tm, stride=SB_HEADS), :] = val[:, h * SB_HEAD_DIM:(h + 1) * SB_HEAD_DIM]


ROW_TILE = D_MODEL // LANES


def _store_row_tiles(ref, val, tm, t0=0):
    for j in range(ROW_TILE):
        ref[pl.ds(t0 * ROW_TILE + j, tm, stride=ROW_TILE), :] = val[:, j * LANES:(j + 1) * LANES]


def _load_row_tiles(ref, tm, t0=0):
    return jnp.concatenate([ref[pl.ds(t0 * ROW_TILE + j, tm, stride=ROW_TILE), :] for j in range(ROW_TILE)],
                           axis=-1)


def _in_proj_kernel(x_ref, g_ref, w_ref, qsb_ref, ksb_ref, vsb_ref, kbf_ref, vbf_ref, gdn_ref, gate_ref, ab_ref,
                    *, tm):
    x = x_ref[...]
    xn = x * lax.rsqrt(jnp.mean(x * x, axis=-1, keepdims=True) + NORM_EPS) * g_ref[...]
    xn = xn.astype(BF16)

    def mm(lo, hi):
        return jnp.dot(xn, w_ref[:, lo:hi], preferred_element_type=F32)

    qsb_ref[...] = (mm(0, 512) * (SB_HEAD_DIM ** -0.5)).astype(BF16)
    k = mm(512, 1024)
    _store_heads(ksb_ref, k, tm)
    kbf_ref[...] = k.astype(BF16)
    v = mm(1024, 1536)
    _store_heads(vsb_ref, v, tm)
    vbf_ref[...] = v.astype(BF16)
    for j in range(3):
        gdn_ref[:, j * 512:(j + 1) * 512] = mm(1536 + j * 512, 2048 + j * 512)
    gate_ref[...] = mm(3072, 3584)
    ab_ref[...] = mm(3584, 3712)


def _in_proj(x2d, g_row, w_bf):
    n = x2d.shape[0]
    tm = min(TM_DENSE, n)
    row = lambda w: pl.BlockSpec((tm, w), lambda i: (i, 0))
    head_rows = pl.BlockSpec((tm * SB_HEADS, SB_HEAD_DIM), lambda i: (i, 0))
    out_shape = (
        jax.ShapeDtypeStruct((n, SB_WIDTH), BF16),
        jax.ShapeDtypeStruct((n * SB_HEADS, SB_HEAD_DIM), F32),
        jax.ShapeDtypeStruct((n * SB_HEADS, SB_HEAD_DIM), F32),
        jax.ShapeDtypeStruct((n, SB_WIDTH), BF16),
        jax.ShapeDtypeStruct((n, SB_WIDTH), BF16),
        jax.ShapeDtypeStruct((n, CONV_CH), F32),
        jax.ShapeDtypeStruct((n, GDN_WIDTH), F32),
        jax.ShapeDtypeStruct((n, LANES), F32),
    )
    return pl.pallas_call(
        functools.partial(_in_proj_kernel, tm=tm),
        grid=(n // tm,),
        in_specs=[row(D_MODEL), pl.BlockSpec((1, D_MODEL), lambda i: (0, 0)),
                  pl.BlockSpec((D_MODEL, IN_WIDTH_PAD), lambda i: (0, 0))],
        out_specs=(row(SB_WIDTH), head_rows, head_rows, row(SB_WIDTH), row(SB_WIDTH),
                   row(CONV_CH), row(GDN_WIDTH), row(LANES)),
        out_shape=out_shape,
        compiler_params=_cparams(("arbitrary",)),
        name="in_proj",
    )(x2d, g_row, w_bf)


def _sb_step(qs, k_blks, v_blks, qpos, kpos, acc_ref, c_ref, kv_t=False, causal=True):
    pairs = len(qs)
    chains = [(p, h) for p in range(pairs) for h in range(2)]
    tk = k_blks[0].shape[1 if kv_t else 0]
    lane = lax.broadcasted_iota(jnp.int32, (1, LANES), 1)
    visible = kpos < qpos
    jj = lax.broadcasted_iota(jnp.int32, (tk, tk), 0)
    ss = lax.broadcasted_iota(jnp.int32, (tk, tk), 1)
    tri = jnp.where(jj >= ss, 1.0, 0.0).astype(BF16)
    nt = (((1,), (1,)), ((), ()))

    if kv_t:
        z = {ch: jnp.dot(qs[ch[0]][ch[1]], k_blks[ch[0]], preferred_element_type=F32) for ch in chains}
    else:
        z = {ch: lax.dot_general(qs[ch[0]][ch[1]], k_blks[ch[0]], nt, preferred_element_type=F32) for ch in chains}
    keep = (lambda a: jnp.where(visible, a, 0.0)) if causal else (lambda a: a)
    sp = {ch: keep(_softplus(z[ch])) for ch in chains}
    sp_hi = {ch: sp[ch].astype(BF16) for ch in chains}
    sp_lo = {ch: (sp[ch] - sp_hi[ch].astype(F32)).astype(BF16) for ch in chains}
    r = {ch: jnp.dot(sp_hi[ch], tri, preferred_element_type=F32) + jnp.dot(sp_lo[ch], tri, preferred_element_type=F32)
         for ch in chains}
    def mass(ch):
        c = c_ref[2 * ch[0] + ch[1]]
        return c[:, :tk] if tk <= LANES else jnp.concatenate([c] * (tk // LANES), axis=-1)

    w = {ch: keep(jnp.exp(z[ch] - r[ch] - mass(ch))).astype(BF16) for ch in chains}
    if kv_t:
        pv = {ch: lax.dot_general(w[ch], v_blks[ch[0]], nt, preferred_element_type=F32) for ch in chains}
    else:
        pv = {ch: jnp.dot(w[ch], v_blks[ch[0]], preferred_element_type=F32) for ch in chains}
    cmin = None
    for p in range(pairs):
        acc_ref[p] += jnp.where(lane < SB_HEAD_DIM, pv[(p, 0)], pv[(p, 1)])
    for ch in chains:
        c_new = c_ref[2 * ch[0] + ch[1]] + r[ch][:, 0:1]
        c_ref[2 * ch[0] + ch[1]] = c_new
        m = jnp.min(c_new)
        cmin = m if cmin is None else jnp.minimum(cmin, m)
    return cmin


def _sb_finish(acc, nsb_row):
    lane = lax.broadcasted_iota(jnp.int32, (1, LANES), 1)
    first = lane < SB_HEAD_DIM
    sq = acc * acc
    s_all = jnp.sum(sq, axis=-1, keepdims=True)
    s0 = jnp.sum(jnp.where(first, sq, 0.0), axis=-1, keepdims=True)
    ms = jnp.where(first, s0, s_all - s0) * (1.0 / SB_HEAD_DIM)
    return acc * lax.rsqrt(ms + NORM_EPS) * nsb_row


def _split_heads(q):
    lane = lax.broadcasted_iota(jnp.int32, (1, LANES), 1)
    zero = jnp.zeros_like(q)
    return jnp.where(lane < SB_HEAD_DIM, q, zero), jnp.where(lane >= SB_HEAD_DIM, q, zero)


def _sb_continue(carry):
    j, cmin = carry
    return jnp.logical_and(j >= 0, cmin < SB_LOG_CUTOFF)


def _sb_prompt_kernel(q_ref, k_ref, v_ref, nsb_ref, o_ref, acc_ref, c_ref, *, tq, tk, pairs):
    i = pl.program_id(2)
    lanes = lambda p: slice(p * LANES, (p + 1) * LANES)
    qs = [_split_heads(q_ref[0, :, lanes(p)]) for p in range(pairs)]
    acc_ref[...] = jnp.zeros_like(acc_ref)
    c_ref[...] = jnp.zeros_like(c_ref)
    qpos = i * tq + lax.broadcasted_iota(jnp.int32, (tq, 1), 0)

    def block(j, causal):
        ks = pl.multiple_of(j * tk, tk)
        kpos = ks + lax.broadcasted_iota(jnp.int32, (1, tk), 1)
        k_blks = [k_ref[0, pl.ds(ks, tk), lanes(p)] for p in range(pairs)]
        v_blks = [v_ref[0, pl.ds(ks, tk), lanes(p)] for p in range(pairs)]
        return _sb_step(qs, k_blks, v_blks, qpos, kpos, acc_ref, c_ref, causal=causal)

    j_top = (i + 1) * (tq // tk) - 1
    cmin = jnp.float32(0.0)
    for d in range(tq // tk):
        cmin = block(j_top - d, True)

    def body(carry):
        j, _ = carry
        return j - 1, block(j, False)

    lax.while_loop(_sb_continue, body, (j_top - tq // tk, cmin))
    for p in range(pairs):
        o_ref[0, :, lanes(p)] = _sb_finish(acc_ref[p], nsb_ref[...]).astype(o_ref.dtype)


def _sb_prompt(q, k, v, nsb_row):
    b, l, _ = q.shape
    tq = min(TQ_SB, l)
    tk = min(TK_SB, tq)
    pairs = SB_PAIRS_PER_STEP
    width = pairs * LANES
    qspec = pl.BlockSpec((1, tq, width), lambda bb, hp, i: (bb, i, hp))
    kvspec = pl.BlockSpec((1, l, width), lambda bb, hp, i: (bb, 0, hp))
    return pl.pallas_call(
        functools.partial(_sb_prompt_kernel, tq=tq, tk=tk, pairs=pairs),
        grid=(b, SB_PAIRS // pairs, l // tq),
        in_specs=[qspec, kvspec, kvspec, pl.BlockSpec((1, LANES), lambda bb, hp, i: (0, 0))],
        out_specs=qspec,
        out_shape=jax.ShapeDtypeStruct((b, l, SB_WIDTH), BF16),
        scratch_shapes=[pltpu.VMEM((pairs, tq, LANES), F32), pltpu.VMEM((2 * pairs, tq, LANES), F32)],
        compiler_params=_cparams(("arbitrary", "arbitrary", "arbitrary")),
        name="sb_prompt",
    )(q, k, v, nsb_row)


def _sb_sample_kernel(q_ref, kn_ref, vn_ref, kw_ref, vw_ref, kc_hbm, vc_hbm, nsb_ref, o_ref,
                      acc_ref, c_ref, kbuf, vbuf, sem, *, past, window, tk):
    b = pl.program_id(0)
    hp = pl.program_id(1)
    t = q_ref.shape[1]
    qs = [_split_heads(q_ref[0])]
    acc_ref[...] = jnp.zeros_like(acc_ref)
    c_ref[...] = jnp.zeros_like(c_ref)
    qpos = past + lax.broadcasted_iota(jnp.int32, (t, 1), 0)

    kpos_new = past + lax.broadcasted_iota(jnp.int32, (1, t), 1)
    cmin0 = _sb_step(qs, [kn_ref[0]], [vn_ref[0]], qpos, kpos_new, acc_ref, c_ref)

    def pair_t(blk):
        return blk.reshape(LANES, tk).astype(BF16)

    def wbody(carry):
        j, _ = carry
        ws = pl.multiple_of(j * tk, tk)
        kpos = (past - window) + ws + lax.broadcasted_iota(jnp.int32, (1, tk), 1)
        cmin = _sb_step(qs, [pair_t(kw_ref[:, :, pl.ds(ws, tk)])], [pair_t(vw_ref[:, :, pl.ds(ws, tk)])],
                        qpos, kpos, acc_ref, c_ref, kv_t=True, causal=False)
        return j - 1, cmin

    _, cmin1 = lax.while_loop(_sb_continue, wbody, (window // tk - 1, cmin0))

    n_old = (past - window) // tk
    if n_old > 0:
        def obody(carry):
            j, _ = carry
            keys = pl.ds(pl.multiple_of(j * tk, tk), tk)
            ck = pltpu.make_async_copy(kc_hbm.at[0, b, pl.ds(2 * hp, 2), :, keys], kbuf, sem.at[0])
            cv = pltpu.make_async_copy(vc_hbm.at[0, b, pl.ds(2 * hp, 2), :, keys], vbuf, sem.at[1])
            ck.start()
            cv.start()
            ck.wait()
            cv.wait()
            kpos = j * tk + lax.broadcasted_iota(jnp.int32, (1, tk), 1)
            cmin = _sb_step(qs, [pair_t(kbuf[...])], [pair_t(vbuf[...])], qpos, kpos, acc_ref, c_ref, kv_t=True,
                            causal=False)
            return j - 1, cmin

        lax.while_loop(_sb_continue, obody, (n_old - 1, cmin1))

    o_ref[0] = _sb_finish(acc_ref[0], nsb_ref[...]).astype(o_ref.dtype)


def _sb_sample(q, k_new, v_new, k_cache_t, v_cache_t, nsb_row):
    b, t, _ = q.shape
    past = k_cache_t.shape[4]
    window = min(SB_CACHE_WINDOW, past)
    tk = min(TK_SB, window)
    assert past % window == 0 and window % tk == 0 and (past - window) % tk == 0
    new_spec = pl.BlockSpec((1, t, LANES), lambda bb, hp: (bb, 0, hp))
    win_spec = pl.BlockSpec((None, None, 2, SB_HEAD_DIM, window), lambda bb, hp: (0, bb, hp, 0, past // window - 1))
    any_spec = pl.BlockSpec(memory_space=pl.ANY)
    return pl.pallas_call(
        functools.partial(_sb_sample_kernel, past=past, window=window, tk=tk),
        grid=(b, SB_PAIRS),
        in_specs=[new_spec, new_spec, new_spec, win_spec, win_spec, any_spec, any_spec,
                  pl.BlockSpec((1, LANES), lambda bb, hp: (0, 0))],
        out_specs=new_spec,
        out_shape=jax.ShapeDtypeStruct((b, t, SB_WIDTH), BF16),
        scratch_shapes=[pltpu.VMEM((1, t, LANES), F32), pltpu.VMEM((2, t, LANES), F32),
                        pltpu.VMEM((2, SB_HEAD_DIM, tk), F32), pltpu.VMEM((2, SB_HEAD_DIM, tk), F32),
                        pltpu.SemaphoreType.DMA((2,))],
        compiler_params=_cparams(("arbitrary", "arbitrary")),
        name="sb_sample",
    )(q, k_new, v_new, k_cache_t, v_cache_t, k_cache_t, v_cache_t, nsb_row)


def _gdn_chunk(nb, qkv_ref, gate_ref, ab_ref, cw_ref, alog_ref, dtb_ref, ng_ref, o_ref, xp_ref, s_ref, chunk):
    ii = lax.broadcasted_iota(jnp.int32, (chunk, chunk), 0)
    jj = lax.broadcasted_iota(jnp.int32, (chunk, chunk), 1)
    tri = jnp.where(ii >= jj, 1.0, 0.0).astype(F32)
    scale = GDN_HEAD_DIM ** -0.5
    hd = GDN_HEAD_DIM
    n_fac = int(math.log2(chunk)) - 1
    chains = [(bi, h) for bi in range(nb) for h in range(GDN_HEADS)]

    ys, gcs, gcts, betas = [], [], [], []
    for bi in range(nb):
        xp_ref[bi, 8:8 + chunk, :] = qkv_ref[bi]
        y = cw_ref[0:1, :] * xp_ref[bi, 5:5 + chunk, :]
        for i in range(1, CONV_WIDTH):
            y = y + cw_ref[i:i + 1, :] * xp_ref[bi, 5 + i:5 + i + chunk, :]
        ys.append(y * _sigmoid(y))
        xp_ref[bi, 0:8, :] = xp_ref[bi, chunk:chunk + 8, :]
        ab = ab_ref[bi]
        g_tile = -jnp.exp(alog_ref[...]) * _softplus(ab + dtb_ref[...])
        betas.append(_sigmoid(ab))
        gc_tile = jnp.dot(tri, g_tile, preferred_element_type=F32, precision=lax.Precision.HIGHEST)
        gcs.append(gc_tile)
        gcts.append(gc_tile.T)

    kn, qs, kb, vb, decay, egc, glast, gcol = {}, {}, {}, {}, {}, {}, {}, {}
    for ch in chains:
        bi, h = ch
        y = ys[bi]
        qh = y[:, h * hd:(h + 1) * hd]
        kh = y[:, GDN_WIDTH + h * hd:GDN_WIDTH + (h + 1) * hd]
        vh = y[:, 2 * GDN_WIDTH + h * hd:2 * GDN_WIDTH + (h + 1) * hd]
        qs[ch] = qh * lax.rsqrt(jnp.sum(qh * qh, axis=-1, keepdims=True) + L2_EPS) * scale
        kn[ch] = kh * lax.rsqrt(jnp.sum(kh * kh, axis=-1, keepdims=True) + L2_EPS)
        gcol[ch] = gcs[bi][:, h:h + 1]
        grow = gcts[bi][h:h + 1, :]
        glast[ch] = gcs[bi][chunk - 1:chunk, h:h + 1]
        bcol = betas[bi][:, GDN_HEADS + h:GDN_HEADS + h + 1]
        decay[ch] = jnp.exp(jnp.where(ii >= jj, gcol[ch] - grow, -1e30))
        egc[ch] = jnp.exp(gcol[ch])
        kb[ch] = kn[ch] * bcol
        vb[ch] = vh * bcol

    a2 = {ch: lax.dot_general(jnp.concatenate([kb[ch], qs[ch]], axis=0).astype(BF16), kn[ch].astype(BF16),
                              (((1,), (1,)), ((), ())), preferred_element_type=F32) for ch in chains}
    low = {ch: jnp.where(ii > jj, a2[ch][:chunk] * decay[ch], 0.0) for ch in chains}
    lb = {ch: low[ch].astype(BF16) for ch in chains}
    qk = {ch: (a2[ch][chunk:] * decay[ch]).astype(BF16) for ch in chains}

    m = {ch: -low[ch] for ch in chains}
    pw = {ch: jnp.dot(lb[ch], lb[ch], preferred_element_type=F32) for ch in chains}
    for f in range(n_fac):
        pb = {ch: pw[ch].astype(BF16) for ch in chains}
        m = {ch: m[ch] + pw[ch] + jnp.dot(pb[ch], m[ch].astype(BF16), preferred_element_type=F32) for ch in chains}
        if f + 1 < n_fac:
            pw = {ch: jnp.dot(pb[ch], pb[ch], preferred_element_type=F32) for ch in chains}
    rhs = {ch: jnp.concatenate([vb[ch], kb[ch] * egc[ch]], axis=1) for ch in chains}
    sol = {ch: rhs[ch] + jnp.dot(m[ch].astype(BF16), rhs[ch].astype(BF16), preferred_element_type=F32)
           for ch in chains}

    s_old = {ch: s_ref[ch[0], ch[1]] for ch in chains}
    ws = {ch: jnp.dot(jnp.concatenate([sol[ch][:, hd:], qs[ch] * egc[ch]], axis=0).astype(BF16),
                      s_old[ch].astype(BF16), preferred_element_type=F32) for ch in chains}
    v_new = {ch: (sol[ch][:, :hd] - ws[ch][:chunk]).astype(BF16) for ch in chains}
    o = {ch: ws[ch][chunk:] + jnp.dot(qk[ch], v_new[ch], preferred_element_type=F32) for ch in chains}
    for ch in chains:
        kd = (kn[ch] * jnp.exp(glast[ch] - gcol[ch])).astype(BF16)
        s_ref[ch[0], ch[1]] = s_old[ch] * jnp.exp(glast[ch]) + lax.dot_general(
            kd, v_new[ch], (((0,), (0,)), ((), ())), preferred_element_type=F32)

    for ch in chains:
        bi, h = ch
        sl = slice(h * hd, (h + 1) * hd)
        on = o[ch] * lax.rsqrt(jnp.mean(o[ch] * o[ch], axis=-1, keepdims=True) + NORM_EPS) * ng_ref[...]
        gt = gate_ref[bi][:, sl]
        o_ref[bi, :, sl] = (on * (gt * _sigmoid(gt))).astype(o_ref.dtype)


def _gdn_kernel(qkv_ref, gate_ref, ab_ref, cinit_ref, sinit_ref, cw_ref, alog_ref, dtb_ref, ng_ref,
                o_ref, cout_ref, sout_ref, xp_ref, s_ref, *, chunk, nb):
    c = pl.program_id(1)

    @pl.when(c == 0)
    def _():
        xp_ref[:, 0:8, :] = cinit_ref[...]
        s_ref[...] = sinit_ref[...]

    _gdn_chunk(nb, qkv_ref, gate_ref, ab_ref, cw_ref, alog_ref, dtb_ref, ng_ref, o_ref, xp_ref, s_ref, chunk)

    @pl.when(c == pl.num_programs(1) - 1)
    def _():
        cout_ref[...] = xp_ref[:, 0:8, :]
        sout_ref[...] = s_ref[...]


def _gdn(qkv, gate, ab, conv_init, s_init, cw8, alog_row, dtb_row, ng_row):
    b, l, _ = qkv.shape
    chunk = min(CHUNK, l)
    nc = l // chunk
    nb = min(GDN_BATCH_PER_STEP, b)
    assert b % nb == 0
    tok = lambda w: pl.BlockSpec((nb, chunk, w), lambda bb, c: (bb, c, 0))
    const = lambda shape: pl.BlockSpec(shape, lambda bb, c: (0,) * len(shape))
    per_b3 = pl.BlockSpec((nb, 8, CONV_CH), lambda bb, c: (bb, 0, 0))
    per_b4 = pl.BlockSpec((nb, GDN_HEADS, GDN_HEAD_DIM, GDN_HEAD_DIM), lambda bb, c: (bb, 0, 0, 0))
    return pl.pallas_call(
        functools.partial(_gdn_kernel, chunk=chunk, nb=nb),
        grid=(b // nb, nc),
        in_specs=[tok(CONV_CH), tok(GDN_WIDTH), tok(LANES), per_b3, per_b4,
                  const((8, CONV_CH)), const((1, LANES)), const((1, LANES)), const((1, LANES))],
        out_specs=(tok(GDN_WIDTH), per_b3, per_b4),
        out_shape=(jax.ShapeDtypeStruct((b, l, GDN_WIDTH), BF16),
                   jax.ShapeDtypeStruct((b, 8, CONV_CH), F32),
                   jax.ShapeDtypeStruct((b, GDN_HEADS, GDN_HEAD_DIM, GDN_HEAD_DIM), F32)),
        scratch_shapes=[pltpu.VMEM((nb, chunk + 8, CONV_CH), F32),
                        pltpu.VMEM((nb, GDN_HEADS, GDN_HEAD_DIM, GDN_HEAD_DIM), F32)],
        compiler_params=_cparams(("arbitrary", "arbitrary")),
        name="gdn",
    )(qkv, gate, ab, conv_init, s_init, cw8, alog_row, dtb_row, ng_row)


def _out_route_kernel(xp_ref, osbp_ref, ogdnp_ref, xs_ref, osbs_ref, ogdns_ref, wo_ref, nf_ref, rw_ref, rb_ref,
                      h_ref, xn_ref, route_ref, cnt_ref, run_ref, *, tm, steps_p):
    i = pl.program_id(0)

    @pl.when(i == 0)
    def _():
        run_ref[...] = jnp.zeros_like(run_ref)

    is_p = i < steps_p
    sub = tm // ROUTE_SUBTILES
    subs = range(ROUTE_SUBTILES)
    rows = [pl.ds(s * sub, sub) for s in subs]
    lane = lax.broadcasted_iota(jnp.int32, (1, LANES), 1)
    lane_f = lane.astype(F32)

    x = [jnp.where(is_p, xp_ref[r, :], xs_ref[r, :]) for r in rows]
    osb = [jnp.where(is_p, osbp_ref[r, :], osbs_ref[r, :]) for r in rows]
    ogdn = [jnp.where(is_p, ogdnp_ref[r, :], ogdns_ref[r, :]) for r in rows]
    h = [x[s] + jnp.dot(osb[s], wo_ref[0:SB_WIDTH, :], preferred_element_type=F32)
         + jnp.dot(ogdn[s], wo_ref[SB_WIDTH:, :], preferred_element_type=F32) for s in subs]
    for s in subs:
        h_ref[rows[s], :] = h[s]
    xn = [h[s] * lax.rsqrt(jnp.mean(h[s] * h[s], axis=-1, keepdims=True) + NORM_EPS) * nf_ref[...] for s in subs]
    for s in subs:
        _store_row_tiles(xn_ref, xn[s], sub, s * sub)

    rw = rw_ref[...]
    rw_hi = rw.astype(BF16)
    rw_lo = (rw - rw_hi.astype(F32)).astype(BF16)
    xn_hi = [xn[s].astype(BF16) for s in subs]
    xn_lo = [(xn[s] - xn_hi[s].astype(F32)).astype(BF16) for s in subs]
    logits = [jnp.dot(xn_hi[s], rw_hi, preferred_element_type=F32) + jnp.dot(xn_lo[s], rw_hi, preferred_element_type=F32)
              + jnp.dot(xn_hi[s], rw_lo, preferred_element_type=F32) + rb_ref[...] for s in subs]
    logits = [jnp.where(lane < N_EXPERTS, logits[s], -jnp.inf) for s in subs]
    vals = [[] for _ in subs]
    hots = [[] for _ in subs]
    for _ in range(TOP_K):
        for s in subs:
            m = jnp.max(logits[s], axis=-1, keepdims=True)
            idx = jnp.min(jnp.where(logits[s] == m, lane_f, float(LANES)), axis=-1, keepdims=True)
            hot = lane_f == idx
            logits[s] = jnp.where(hot, -jnp.inf, logits[s])
            vals[s].append(m)
            hots[s].append((hot, idx))

    ii = lax.broadcasted_iota(jnp.int32, (sub, sub), 0)
    jj = lax.broadcasted_iota(jnp.int32, (sub, sub), 1)
    earlier = jnp.where(ii > jj, 1.0, 0.0).astype(BF16)
    multi = []
    for s in subs:
        m_s = jnp.zeros((sub, LANES), F32)
        for hot, _ in hots[s]:
            m_s = jnp.where(hot, 1.0, m_s)
        multi.append(m_s)
    within = [jnp.dot(earlier, multi[s].astype(BF16), preferred_element_type=F32) for s in subs]
    run = run_ref[0:1, :]
    for s in subs:
        rank_all = within[s] + run
        run = run + jnp.sum(multi[s], axis=0, keepdims=True)
        exps = [jnp.exp(v - vals[s][0]) for v in vals[s]]
        denom = exps[0] + exps[1] + exps[2] + exps[3]
        route = jnp.zeros((sub, LANES), F32)
        for k, (hot, idx) in enumerate(hots[s]):
            rank = jnp.sum(jnp.where(hot, rank_all, 0.0), axis=-1, keepdims=True)
            route = jnp.where(lane == k, idx, route)
            route = jnp.where(lane == TOP_K + k, rank, route)
            route = jnp.where(lane == 2 * TOP_K + k, exps[k] / denom, route)
        route_ref[rows[s], :] = route
    run_ref[...] = jnp.broadcast_to(run, run_ref.shape)
    cnt_ref[...] = jnp.broadcast_to(run, cnt_ref.shape)


def _out_route(xp, osbp, ogdnp, xs, osbs, ogdns, wo_bf, nf_row, rw_pad, rb_row):
    n_p, n_s = xp.shape[0], xs.shape[0]
    tm = min(TM_ROUTE, n_p, n_s)
    steps_p, steps_s = n_p // tm, n_s // tm
    prow = lambda w: pl.BlockSpec((tm, w), lambda i: (jnp.minimum(i, steps_p - 1), 0))
    srow = lambda w: pl.BlockSpec((tm, w), lambda i: (jnp.maximum(i - steps_p, 0), 0))
    orow = lambda w: pl.BlockSpec((tm, w), lambda i: (i, 0))
    const = lambda shape: pl.BlockSpec(shape, lambda i: (0,) * len(shape))
    n_total = n_p + n_s
    out_shape = (jax.ShapeDtypeStruct((n_total, D_MODEL), F32),
                 jax.ShapeDtypeStruct((n_total * ROW_TILE, LANES), F32),
                 jax.ShapeDtypeStruct((n_total, LANES), F32),
                 jax.ShapeDtypeStruct((8, LANES), F32))
    return pl.pallas_call(
        functools.partial(_out_route_kernel, tm=tm, steps_p=steps_p),
        grid=(steps_p + steps_s,),
        in_specs=[prow(D_MODEL), prow(SB_WIDTH), prow(GDN_WIDTH), srow(D_MODEL), srow(SB_WIDTH), srow(GDN_WIDTH),
                  const((D_MODEL, D_MODEL)), const((1, D_MODEL)), const((D_MODEL, LANES)), const((1, LANES))],
        out_specs=(orow(D_MODEL), pl.BlockSpec((tm * ROW_TILE, LANES), lambda i: (i, 0)), orow(LANES),
                   const((8, LANES))),
        out_shape=out_shape,
        scratch_shapes=[pltpu.VMEM((8, LANES), F32)],
        compiler_params=_cparams(("arbitrary",)),
        name="out_route",
    )(xp, osbp, ogdnp, xs, osbs, ogdns, wo_bf, nf_row, rw_pad, rb_row)


def _scatter_rows_kernel(pos_ref, x_ref, xs_hbm, sem, *, tm):
    def row_copy(t, p):
        return pltpu.make_async_copy(x_ref.at[pl.ds(pl.multiple_of(t * ROW_TILE, ROW_TILE), ROW_TILE)],
                                     xs_hbm.at[pl.ds(pl.multiple_of(p * ROW_TILE, ROW_TILE), ROW_TILE)], sem)

    def issue(t, carry):
        for k in range(TOP_K):
            row_copy(t, pos_ref[k, t]).start(priority=k % 2)
        return carry

    lax.fori_loop(0, tm, issue, 0)

    def drain(t, carry):
        for k in range(TOP_K):
            row_copy(0, 0).wait()
        return carry

    lax.fori_loop(0, tm, drain, 0)


def _scatter_rows(pos_t, xn):
    n = xn.shape[0] // ROW_TILE
    tm = min(TM_ROWS, n)
    return pl.pallas_call(
        functools.partial(_scatter_rows_kernel, tm=tm),
        grid=(n // tm,),
        in_specs=[pl.BlockSpec((TOP_K, tm), lambda i: (0, i), memory_space=pltpu.SMEM),
                  pl.BlockSpec((tm * ROW_TILE, LANES), lambda i: (i, 0))],
        out_specs=pl.BlockSpec(memory_space=pl.ANY),
        out_shape=jax.ShapeDtypeStruct((n * TOP_K * ROW_TILE, LANES), xn.dtype),
        scratch_shapes=[pltpu.SemaphoreType.DMA(())],
        compiler_params=_cparams(("arbitrary",)),
        name="moe_scatter",
    )(pos_t, xn)


def _experts_kernel(tile_ref, exp_ref, lo_ref, hi_ref, first_ref, newexp_ref, nvalid_ref,
                    x_ref, wg_ref, bg_ref, wu_ref, bu_ref, wd_ref, bd_ref, y_ref, wg_bf, wu_bf, wd_bf, *, tmg):
    v = pl.program_id(0)

    @pl.when(jnp.logical_and(v < nvalid_ref[0], newexp_ref[v] == 1))
    def _():
        wg_bf[...] = wg_ref[0].astype(BF16)
        wu_bf[...] = wu_ref[0].astype(BF16)
        wd_bf[...] = wd_ref[0].astype(BF16)

    @pl.when(v < nvalid_ref[0])
    def _():
        sub = tmg // EXPERT_SUBTILES
        subs = range(EXPERT_SUBTILES)
        xs = [_load_row_tiles(x_ref, sub, s * sub).astype(BF16) for s in subs]
        gt, up = [], []
        for s in subs:
            gt.append(jnp.dot(xs[s], wg_bf[...], preferred_element_type=F32))
            up.append(jnp.dot(xs[s], wu_bf[...], preferred_element_type=F32))
        hid = []
        for s in subs:
            g = jnp.minimum(gt[s] + bg_ref[0], SWIGLU_LIMIT)
            u = jnp.clip(up[s] + bu_ref[0], -SWIGLU_LIMIT, SWIGLU_LIMIT)
            hid.append(((u + 1.0) * (g * _sigmoid(SWIGLU_ALPHA * g))).astype(BF16))
        ys = [jnp.dot(hid[s], wd_bf[...], preferred_element_type=F32) + bd_ref[0] for s in subs]
        mine = []
        for s in subs:
            rows = tile_ref[v] * tmg + s * sub + lax.broadcasted_iota(jnp.int32, (sub, 1), 0)
            mine.append(jnp.logical_and(rows >= lo_ref[v], rows < hi_ref[v]))

        @pl.when(first_ref[v] == 1)
        def _():
            for s in subs:
                _store_row_tiles(y_ref, jnp.where(mine[s], ys[s], 0.0), sub, s * sub)

        @pl.when(first_ref[v] == 0)
        def _():
            for s in subs:
                _store_row_tiles(y_ref, jnp.where(mine[s], ys[s], _load_row_tiles(y_ref, sub, s * sub)), sub, s * sub)


def _experts(meta, xs, wg, bg, wu, bu, wd, bd):
    n4 = xs.shape[0] // ROW_TILE
    tmg = min(TM_GROUP, n4)
    n_visits = meta[0].shape[0]
    xspec = pl.BlockSpec((tmg * ROW_TILE, LANES), lambda v, tile, exp, *_: (tile[v], 0))
    wspec = pl.BlockSpec((1, D_MODEL, D_MODEL), lambda v, tile, exp, *_: (exp[v], 0, 0))
    bspec = pl.BlockSpec((1, 1, D_MODEL), lambda v, tile, exp, *_: (exp[v], 0, 0))
    grid_spec = pltpu.PrefetchScalarGridSpec(
        num_scalar_prefetch=len(meta),
        grid=(n_visits,),
        in_specs=[xspec, wspec, bspec, wspec, bspec, wspec, bspec],
        out_specs=xspec,
        scratch_shapes=[pltpu.VMEM((D_MODEL, D_MODEL), BF16)] * 3,
    )
    return pl.pallas_call(
        functools.partial(_experts_kernel, tmg=tmg),
        grid_spec=grid_spec,
        out_shape=jax.ShapeDtypeStruct((n4 * ROW_TILE, LANES), F32),
        compiler_params=_cparams(("arbitrary",), VMEM_LIMIT_EXPERTS),
        name="moe_experts",
    )(*meta, xs, wg, bg, wu, bu, wd, bd)


def _combine_kernel(pos_ref, posn_ref, h_ref, route_ref, nfin_ref, ys_hbm, o_ref, buf, sem, *, tm):
    i = pl.program_id(0)
    slot = lax.rem(i, 2)

    def row_copy(s, t, k, p):
        return pltpu.make_async_copy(ys_hbm.at[pl.ds(pl.multiple_of(p * ROW_TILE, ROW_TILE), ROW_TILE)],
                                     buf.at[s, k, pl.ds(pl.multiple_of(t * ROW_TILE, ROW_TILE), ROW_TILE)],
                                     sem.at[s])

    def start_gathers(p_ref, s):
        def issue(t, carry):
            for k in range(TOP_K):
                row_copy(s, t, k, p_ref[k, t]).start(priority=k % 2)
            return carry

        lax.fori_loop(0, tm, issue, 0)

    @pl.when(i == 0)
    def _():
        start_gathers(pos_ref, slot)

    @pl.when(i + 1 < pl.num_programs(0))
    def _():
        start_gathers(posn_ref, 1 - slot)

    def drain(t, carry):
        for k in range(TOP_K):
            row_copy(slot, 0, k, 0).wait()
        return carry

    lax.fori_loop(0, tm, drain, 0)

    route = route_ref[...]
    h = h_ref[...]
    pieces = []
    for j in range(ROW_TILE):
        piece = h[:, j * LANES:(j + 1) * LANES]
        for k in range(TOP_K):
            piece = piece + route[:, 2 * TOP_K + k:2 * TOP_K + k + 1] * buf[slot, k, pl.ds(j, tm, stride=ROW_TILE), :]
        pieces.append(piece)
    out = jnp.concatenate(pieces, axis=-1)
    o_ref[...] = out * lax.rsqrt(jnp.mean(out * out, axis=-1, keepdims=True) + NORM_EPS) * nfin_ref[...]


def _combine(pos_t, h, route, nfin_row, ys, row0, n):
    tm = min(TM_ROWS, n)
    off = row0 // tm
    steps = n // tm
    return pl.pallas_call(
        functools.partial(_combine_kernel, tm=tm),
        grid=(steps,),
        in_specs=[pl.BlockSpec((TOP_K, tm), lambda i: (0, i + off), memory_space=pltpu.SMEM),
                  pl.BlockSpec((TOP_K, tm), lambda i: (0, jnp.minimum(i + 1, steps - 1) + off),
                               memory_space=pltpu.SMEM),
                  pl.BlockSpec((tm, D_MODEL), lambda i: (i + off, 0)),
                  pl.BlockSpec((tm, LANES), lambda i: (i + off, 0)),
                  pl.BlockSpec((1, D_MODEL), lambda i: (0, 0)),
                  pl.BlockSpec(memory_space=pl.ANY)],
        out_specs=pl.BlockSpec((tm, D_MODEL), lambda i: (i, 0)),
        out_shape=jax.ShapeDtypeStruct((n, D_MODEL), F32),
        scratch_shapes=[pltpu.VMEM((2, TOP_K, tm * ROW_TILE, LANES), F32), pltpu.SemaphoreType.DMA((2,))],
        compiler_params=_cparams(("arbitrary",)),
        name="moe_combine",
    )(pos_t, pos_t, h, route, nfin_row, ys)


def _group_plan(counts, n4, tmg):
    n_tiles = n4 // tmg
    n_visits = n_tiles + N_EXPERTS - 1
    ends = jnp.cumsum(counts)
    starts = ends - counts
    t_first = starts // tmg
    t_cnt = jnp.where(counts > 0, (ends - 1) // tmg - t_first + 1, 0)
    v_end = jnp.cumsum(t_cnt)
    v_start = v_end - t_cnt
    total = v_end[-1]
    v = jnp.arange(n_visits, dtype=jnp.int32)
    g = jnp.minimum(jnp.sum((v[:, None] >= v_end[None, :]).astype(jnp.int32), axis=1), N_EXPERTS - 1)
    g_last = jnp.max(jnp.where(counts > 0, jnp.arange(N_EXPERTS), 0)).astype(jnp.int32)
    valid = v < total
    g = jnp.where(valid, g, g_last)
    hot = g[:, None] == jnp.arange(N_EXPERTS, dtype=jnp.int32)[None, :]
    pick = lambda table: jnp.sum(jnp.where(hot, table[None, :], 0), axis=1)
    tile = jnp.where(valid, pick(t_first) + (v - pick(v_start)), n_tiles - 1).astype(jnp.int32)
    lo = jnp.where(valid, jnp.maximum(pick(starts), tile * tmg), 0).astype(jnp.int32)
    hi = jnp.where(valid, jnp.minimum(pick(ends), (tile + 1) * tmg), 0).astype(jnp.int32)
    prev_g = jnp.concatenate([jnp.full((1,), -1, jnp.int32), g[:-1]])
    newexp = (g != prev_g).astype(jnp.int32)
    prev_tile = jnp.concatenate([jnp.full((1,), -1, jnp.int32), tile[:-1]])
    first = (tile != prev_tile).astype(jnp.int32)
    meta = (tile, g, lo, hi, first, newexp, total.reshape(1).astype(jnp.int32))
    return meta, starts


def _pad_rows(a, rows):
    return jnp.concatenate([a, jnp.zeros((rows - a.shape[0],) + a.shape[1:], a.dtype)], axis=0)


def kernel(x_prompt, x_sample, cache_sb_k, cache_sb_v, cache_gdn_conv, state_gdn, norm_mix, w_in, conv_w, a_log,
           dt_bias, norm_sb, norm_gdn, w_out, norm_ffn, router_w, router_b, w_gate, b_gate, w_up, b_up, w_down,
           b_down, norm_final):
    bp, lp, _ = x_prompt.shape
    bs, ls, _ = x_sample.shape
    past = cache_sb_k.shape[2]
    n_p, n_s = bp * lp, bs * ls
    n_tot = n_p + n_s
    assert norm_mix.shape[0] == 1, "single-layer trunk"
    for tile_rows in (TM_DENSE, TM_ROUTE, TM_ROWS):
        assert n_p % min(tile_rows, n_p, n_s) == 0 and n_s % min(tile_rows, n_p, n_s) == 0
    assert (n_tot * TOP_K) % TM_GROUP == 0 and n_tot % TM_ROWS == 0
    assert lp % min(TQ_SB, lp) == 0 and lp % min(CHUNK, lp) == 0 and ls % min(CHUNK, ls) == 0 and ls >= 8

    w_in_bf = jnp.pad(w_in[0], ((0, 0), (0, IN_WIDTH_PAD - IN_WIDTH))).astype(BF16)
    nmix_row = norm_mix[0].reshape(1, D_MODEL)
    nsb_row = jnp.tile(norm_sb[0], 2).reshape(1, LANES)
    ng_row = norm_gdn[0].reshape(1, LANES)
    cw8 = _pad_rows(conv_w[0], 8)
    alog_row = jnp.pad(a_log[0], (0, LANES - GDN_HEADS)).reshape(1, LANES)
    dtb_row = jnp.pad(dt_bias[0], (0, LANES - GDN_HEADS)).reshape(1, LANES)
    wo_bf = w_out[0].astype(BF16)
    nf_row = norm_ffn[0].reshape(1, D_MODEL)
    rw_pad = jnp.pad(router_w[0], ((0, 0), (0, LANES - N_EXPERTS)))
    rb_row = jnp.pad(router_b[0], (0, LANES - N_EXPERTS)).reshape(1, LANES)
    nfin_row = norm_final.reshape(1, D_MODEL)

    xp2 = x_prompt.reshape(n_p, D_MODEL)
    xs2 = x_sample.reshape(n_s, D_MODEL)

    qsb, k_prompt, v_prompt, kbf, vbf, gdn_in, gate, ab = _in_proj(xp2, nmix_row, w_in_bf)
    r3 = lambda a, b, l: a.reshape(b, l, a.shape[-1])
    osb_p = _sb_prompt(r3(qsb, bp, lp), r3(kbf, bp, lp), r3(vbf, bp, lp), nsb_row)
    ogdn_p, conv_p, state_p = _gdn(
        r3(gdn_in, bp, lp), r3(gate, bp, lp), r3(ab, bp, lp),
        jnp.zeros((bp, 8, CONV_CH), F32), jnp.zeros((bp, GDN_HEADS, GDN_HEAD_DIM, GDN_HEAD_DIM), F32),
        cw8, alog_row, dtb_row, ng_row)

    qsb, k_sample, v_sample, kbf, vbf, gdn_in, gate, ab = _in_proj(xs2, nmix_row, w_in_bf)
    keys_minor = lambda cache: jnp.transpose(cache, (0, 1, 3, 4, 2))
    osb_s = _sb_sample(r3(qsb, bs, ls), r3(kbf, bs, ls), r3(vbf, bs, ls), keys_minor(cache_sb_k),
                       keys_minor(cache_sb_v), nsb_row)
    conv_init = jnp.concatenate([jnp.zeros((bs, 8 - (CONV_WIDTH - 1), CONV_CH), F32), cache_gdn_conv[0]], axis=1)
    ogdn_s, conv_s, state_s = _gdn(r3(gdn_in, bs, ls), r3(gate, bs, ls), r3(ab, bs, ls), conv_init, state_gdn[0],
                                   cw8, alog_row, dtb_row, ng_row)

    h_buf, xn_buf, route_buf, cnt = _out_route(
        xp2, osb_p.reshape(n_p, SB_WIDTH), ogdn_p.reshape(n_p, GDN_WIDTH),
        xs2, osb_s.reshape(n_s, SB_WIDTH), ogdn_s.reshape(n_s, GDN_WIDTH), wo_bf, nf_row, rw_pad, rb_row)

    counts = cnt[0, :N_EXPERTS].astype(jnp.int32)
    n4 = n_tot * TOP_K
    tmg = min(TM_GROUP, n4)
    meta, starts = _group_plan(counts, n4, tmg)
    idx_t = route_buf[:, 0:TOP_K].T.astype(jnp.int32)
    rank_t = route_buf[:, TOP_K:2 * TOP_K].T.astype(jnp.int32)
    pos_t = rank_t
    for e in range(N_EXPERTS):
        pos_t = pos_t + jnp.where(idx_t == e, starts[e], 0)

    xs_sorted = _scatter_rows(pos_t, xn_buf)
    e3 = lambda bias: bias[0].reshape(N_EXPERTS, 1, D_MODEL)
    ys_sorted = _experts(meta, xs_sorted, w_gate[0], e3(b_gate), w_up[0], e3(b_up), w_down[0], e3(b_down))
    y_prompt = _combine(pos_t, h_buf, route_buf, nfin_row, ys_sorted, 0, n_p).reshape(bp, lp, D_MODEL)
    y_sample = _combine(pos_t, h_buf, route_buf, nfin_row, ys_sorted, n_p, n_s).reshape(bs, ls, D_MODEL)

    heads = lambda a, b, l: a.reshape(1, b, l, SB_HEADS, SB_HEAD_DIM)
    return (y_prompt, y_sample,
            heads(k_prompt, bp, lp), heads(v_prompt, bp, lp),
            conv_p[:, 8 - (CONV_WIDTH - 1):][None], state_p[None],
            heads(k_sample, bs, ls), heads(v_sample, bs, ls),
            conv_s[:, 8 - (CONV_WIDTH - 1):][None], state_s[None])
```

```python
import functools
import math

import jax
import jax.numpy as jnp
from jax import lax
from jax.experimental import pallas as pl
from jax.experimental.pallas import tpu as pltpu

F32 = jnp.float32
BF16 = jnp.bfloat16

D_MODEL = 1024
SB_HEAD_DIM = 64
SB_WIDTH = 512
SB_HEADS = SB_WIDTH // SB_HEAD_DIM
SB_PAIRS = SB_WIDTH // 128
GDN_HEAD_DIM = 128
GDN_HEADS = 4
GDN_WIDTH = 512
CONV_WIDTH = 4
CONV_CH = 3 * GDN_WIDTH
IN_WIDTH = 3 * SB_WIDTH + 4 * GDN_WIDTH + 2 * GDN_HEADS
IN_WIDTH_PAD = 3 * SB_WIDTH + 4 * GDN_WIDTH + 128
N_EXPERTS = 32
TOP_K = 4
SWIGLU_LIMIT = 7.0
SWIGLU_ALPHA = 1.702
NORM_EPS = 1e-6
L2_EPS = 1e-6
CHUNK = 64

LANES = 128
VMEM_LIMIT_BYTES = 48 * 1024 * 1024
VMEM_LIMIT_EXPERTS = 56 * 1024 * 1024
SB_LOG_CUTOFF = 104.0

TM_DENSE = 512
TM_ROUTE = 512
ROUTE_SUBTILES = 2
TQ_SB = 256
TK_SB = 256
SB_PAIRS_PER_STEP = 4
SB_CACHE_WINDOW = 512
GDN_BATCH_PER_STEP = 4
TM_ROWS = 256
TM_GROUP = 512
EXPERT_SUBTILES = 2


def _cparams(sem, limit=VMEM_LIMIT_BYTES):
    return pltpu.CompilerParams(dimension_semantics=sem, vmem_limit_bytes=limit)


def _softplus(z):
    return jnp.maximum(z, 0.0) + jnp.log(1.0 + jnp.exp(-jnp.abs(z)))


def _sigmoid(z):
    return 1.0 / (1.0 + jnp.exp(-z))


def _store_heads(ref, val, tm):
    for h in range(SB_HEADS):
        ref[pl.ds(h, tm, stride=SB_HEADS), :] = val[:, h * SB_HEAD_DIM:(h + 1) * SB_HEAD_DIM]


ROW_TILE = D_MODEL // LANES


def _store_row_tiles(ref, val, tm, t0=0):
    for j in range(ROW_TILE):
        ref[pl.ds(t0 * ROW_TILE + j, tm, stride=ROW_TILE), :] = val[:, j * LANES:(j + 1) * LANES]


def _load_row_tiles(ref, tm, t0=0):
    return jnp.concatenate([ref[pl.ds(t0 * ROW_TILE + j, tm, stride=ROW_TILE), :] for j in range(ROW_TILE)],
                           axis=-1)


def _in_proj_kernel(x_ref, g_ref, w_ref, qsb_ref, ksb_ref, vsb_ref, kbf_ref, vbf_ref, gdn_ref, gate_ref, ab_ref,
                    *, tm):
    x = x_ref[...]
    xn = x * lax.rsqrt(jnp.mean(x * x, axis=-1, keepdims=True) + NORM_EPS) * g_ref[...]
    xn = xn.astype(BF16)

    def mm(lo, hi):
        return jnp.dot(xn, w_ref[:, lo:hi], preferred_element_type=F32)

    qsb_ref[...] = (mm(0, 512) * (SB_HEAD_DIM ** -0.5)).astype(BF16)
    k = mm(512, 1024)
    _store_heads(ksb_ref, k, tm)
    kbf_ref[...] = k.astype(BF16)
    v = mm(1024, 1536)
    _store_heads(vsb_ref, v, tm)
    vbf_ref[...] = v.astype(BF16)
    for j in range(3):
        gdn_ref[:, j * 512:(j + 1) * 512] = mm(1536 + j * 512, 2048 + j * 512)
    gate_ref[...] = mm(3072, 3584)
    ab_ref[...] = mm(3584, 3712)


def _in_proj(x2d, g_row, w_bf):
    n = x2d.shape[0]
    tm = min(TM_DENSE, n)
    row = lambda w: pl.BlockSpec((tm, w), lambda i: (i, 0))
    head_rows = pl.BlockSpec((tm * SB_HEADS, SB_HEAD_DIM), lambda i: (i, 0))
    out_shape = (
        jax.ShapeDtypeStruct((n, SB_WIDTH), BF16),
        jax.ShapeDtypeStruct((n * SB_HEADS, SB_HEAD_DIM), F32),
        jax.ShapeDtypeStruct((n * SB_HEADS, SB_HEAD_DIM), F32),
        jax.ShapeDtypeStruct((n, SB_WIDTH), BF16),
        jax.ShapeDtypeStruct((n, SB_WIDTH), BF16),
        jax.ShapeDtypeStruct((n, CONV_CH), F32),
        jax.ShapeDtypeStruct((n, GDN_WIDTH), F32),
        jax.ShapeDtypeStruct((n, LANES), F32),
    )
    return pl.pallas_call(
        functools.partial(_in_proj_kernel, tm=tm),
        grid=(n // tm,),
        in_specs=[row(D_MODEL), pl.BlockSpec((1, D_MODEL), lambda i: (0, 0)),
                  pl.BlockSpec((D_MODEL, IN_WIDTH_PAD), lambda i: (0, 0))],
        out_specs=(row(SB_WIDTH), head_rows, head_rows, row(SB_WIDTH), row(SB_WIDTH),
                   row(CONV_CH), row(GDN_WIDTH), row(LANES)),
        out_shape=out_shape,
        compiler_params=_cparams(("arbitrary",)),
        name="in_proj",
    )(x2d, g_row, w_bf)


def _sb_step(qs, k_blks, v_blks, qpos, kpos, acc_ref, c_ref, kv_t=False, causal=True):
    pairs = len(qs)
    chains = [(p, h) for p in range(pairs) for h in range(2)]
    tk = k_blks[0].shape[1 if kv_t else 0]
    lane = lax.broadcasted_iota(jnp.int32, (1, LANES), 1)
    visible = kpos < qpos
    jj = lax.broadcasted_iota(jnp.int32, (tk, tk), 0)
    ss = lax.broadcasted_iota(jnp.int32, (tk, tk), 1)
    tri = jnp.where(jj >= ss, 1.0, 0.0).astype(BF16)
    nt = (((1,), (1,)), ((), ()))

    if kv_t:
        z = {ch: jnp.dot(qs[ch[0]][ch[1]], k_blks[ch[0]], preferred_element_type=F32) for ch in chains}
    else:
        z = {ch: lax.dot_general(qs[ch[0]][ch[1]], k_blks[ch[0]], nt, preferred_element_type=F32) for ch in chains}
    keep = (lambda a: jnp.where(visible, a, 0.0)) if causal else (lambda a: a)
    sp = {ch: keep(_softplus(z[ch])) for ch in chains}
    sp_hi = {ch: sp[ch].astype(BF16) for ch in chains}
    sp_lo = {ch: (sp[ch] - sp_hi[ch].astype(F32)).astype(BF16) for ch in chains}
    r = {ch: jnp.dot(sp_hi[ch], tri, preferred_element_type=F32) + jnp.dot(sp_lo[ch], tri, preferred_element_type=F32)
         for ch in chains}
    def mass(ch):
        c = c_ref[2 * ch[0] + ch[1]]
        return c[:, :tk] if tk <= LANES else jnp.concatenate([c] * (tk // LANES), axis=-1)

    w = {ch: keep(jnp.exp(z[ch] - r[ch] - mass(ch))).astype(BF16) for ch in chains}
    if kv_t:
        pv = {ch: lax.dot_general(w[ch], v_blks[ch[0]], nt, preferred_element_type=F32) for ch in chains}
    else:
        pv = {ch: jnp.dot(w[ch], v_blks[ch[0]], preferred_element_type=F32) for ch in chains}
    cmin = None
    for p in range(pairs):
        acc_ref[p] += jnp.where(lane < SB_HEAD_DIM, pv[(p, 0)], pv[(p, 1)])
    for ch in chains:
        c_new = c_ref[2 * ch[0] + ch[1]] + r[ch][:, 0:1]
        c_ref[2 * ch[0] + ch[1]] = c_new
        m = jnp.min(c_new)
        cmin = m if cmin is None else jnp.minimum(cmin, m)
    return cmin


def _sb_finish(acc, nsb_row):
    lane = lax.broadcasted_iota(jnp.int32, (1, LANES), 1)
    first = lane < SB_HEAD_DIM
    sq = acc * acc
    s_all = jnp.sum(sq, axis=-1, keepdims=True)
    s0 = jnp.sum(jnp.where(first, sq, 0.0), axis=-1, keepdims=True)
    ms = jnp.where(first, s0, s_all - s0) * (1.0 / SB_HEAD_DIM)
    return acc * lax.rsqrt(ms + NORM_EPS) * nsb_row


def _split_heads(q):
    lane = lax.broadcasted_iota(jnp.int32, (1, LANES), 1)
    zero = jnp.zeros_like(q)
    return jnp.where(lane < SB_HEAD_DIM, q, zero), jnp.where(lane >= SB_HEAD_DIM, q, zero)


def _sb_continue(carry):
    j, cmin = carry
    return jnp.logical_and(j >= 0, cmin < SB_LOG_CUTOFF)


def _sb_prompt_kernel(q_ref, k_ref, v_ref, nsb_ref, o_ref, acc_ref, c_ref, *, tq, tk, pairs):
    i = pl.program_id(2)
    lanes = lambda p: slice(p * LANES, (p + 1) * LANES)
    qs = [_split_heads(q_ref[0, :, lanes(p)]) for p in range(pairs)]
    acc_ref[...] = jnp.zeros_like(acc_ref)
    c_ref[...] = jnp.zeros_like(c_ref)
    qpos = i * tq + lax.broadcasted_iota(jnp.int32, (tq, 1), 0)

    def block(j, causal):
        ks = pl.multiple_of(j * tk, tk)
        kpos = ks + lax.broadcasted_iota(jnp.int32, (1, tk), 1)
        k_blks = [k_ref[0, pl.ds(ks, tk), lanes(p)] for p in range(pairs)]
        v_blks = [v_ref[0, pl.ds(ks, tk), lanes(p)] for p in range(pairs)]
        return _sb_step(qs, k_blks, v_blks, qpos, kpos, acc_ref, c_ref, causal=causal)

    j_top = (i + 1) * (tq // tk) - 1
    cmin = jnp.float32(0.0)
    for d in range(tq // tk):
        cmin = block(j_top - d, True)

    def body(carry):
        j, _ = carry
        return j - 1, block(j, False)

    lax.while_loop(_sb_continue, body, (j_top - tq // tk, cmin))
    for p in range(pairs):
        o_ref[0, :, lanes(p)] = _sb_finish(acc_ref[p], nsb_ref[...]).astype(o_ref.dtype)


def _sb_prompt(q, k, v, nsb_row):
    b, l, _ = q.shape
    tq = min(TQ_SB, l)
    tk = min(TK_SB, tq)
    pairs = SB_PAIRS_PER_STEP
    width = pairs * LANES
    qspec = pl.BlockSpec((1, tq, width), lambda bb, hp, i: (bb, i, hp))
    kvspec = pl.BlockSpec((1, l, width), lambda bb, hp, i: (bb, 0, hp))
    return pl.pallas_call(
        functools.partial(_sb_prompt_kernel, tq=tq, tk=tk, pairs=pairs),
        grid=(b, SB_PAIRS // pairs, l // tq),
        in_specs=[qspec, kvspec, kvspec, pl.BlockSpec((1, LANES), lambda bb, hp, i: (0, 0))],
        out_specs=qspec,
        out_shape=jax.ShapeDtypeStruct((b, l, SB_WIDTH), BF16),
        scratch_shapes=[pltpu.VMEM((pairs, tq, LANES), F32), pltpu.VMEM((2 * pairs, tq, LANES), F32)],
        compiler_params=_cparams(("arbitrary", "arbitrary", "arbitrary")),
        name="sb_prompt",
    )(q, k, v, nsb_row)


def _sb_sample_kernel(q_ref, kn_ref, vn_ref, kw_ref, vw_ref, kc_hbm, vc_hbm, nsb_ref, o_ref,
                      acc_ref, c_ref, kbuf, vbuf, sem, *, past, window, tk):
    b = pl.program_id(0)
    t = q_ref.shape[1]
    pairs = range(SB_PAIRS)
    lanes = lambda p: slice(p * LANES, (p + 1) * LANES)
    qs = [_split_heads(q_ref[0, :, lanes(p)]) for p in pairs]
    acc_ref[...] = jnp.zeros_like(acc_ref)
    c_ref[...] = jnp.zeros_like(c_ref)
    qpos = past + lax.broadcasted_iota(jnp.int32, (t, 1), 0)

    kpos_new = past + lax.broadcasted_iota(jnp.int32, (1, t), 1)
    cmin0 = _sb_step(qs, [kn_ref[0, :, lanes(p)] for p in pairs], [vn_ref[0, :, lanes(p)] for p in pairs],
                     qpos, kpos_new, acc_ref, c_ref)

    def pair_t(blk):
        return blk.reshape(LANES, tk).astype(BF16)

    def wbody(carry):
        j, _ = carry
        ws = pl.multiple_of(j * tk, tk)
        kpos = (past - window) + ws + lax.broadcasted_iota(jnp.int32, (1, tk), 1)
        cmin = _sb_step(qs, [pair_t(kw_ref[2 * p:2 * p + 2, :, pl.ds(ws, tk)]) for p in pairs],
                        [pair_t(vw_ref[2 * p:2 * p + 2, :, pl.ds(ws, tk)]) for p in pairs],
                        qpos, kpos, acc_ref, c_ref, kv_t=True, causal=False)
        return j - 1, cmin

    _, cmin1 = lax.while_loop(_sb_continue, wbody, (window // tk - 1, cmin0))

    n_old = (past - window) // tk
    if n_old > 0:
        def obody(carry):
            j, _ = carry
            keys = pl.ds(pl.multiple_of(j * tk, tk), tk)
            ck = pltpu.make_async_copy(kc_hbm.at[0, b, :, :, keys], kbuf, sem.at[0])
            cv = pltpu.make_async_copy(vc_hbm.at[0, b, :, :, keys], vbuf, sem.at[1])
            ck.start()
            cv.start()
            ck.wait()
            cv.wait()
            kpos = j * tk + lax.broadcasted_iota(jnp.int32, (1, tk), 1)
            cmin = _sb_step(qs, [pair_t(kbuf[2 * p:2 * p + 2]) for p in pairs],
                            [pair_t(vbuf[2 * p:2 * p + 2]) for p in pairs], qpos, kpos, acc_ref, c_ref, kv_t=True,
                            causal=False)
            return j - 1, cmin

        lax.while_loop(_sb_continue, obody, (n_old - 1, cmin1))

    for p in pairs:
        o_ref[0, :, lanes(p)] = _sb_finish(acc_ref[p], nsb_ref[...]).astype(o_ref.dtype)


def _sb_sample(q, k_new, v_new, k_cache_t, v_cache_t, nsb_row):
    b, t, _ = q.shape
    past = k_cache_t.shape[4]
    window = min(SB_CACHE_WINDOW, past)
    tk = min(TK_SB, window)
    assert past % window == 0 and window % tk == 0 and (past - window) % tk == 0
    new_spec = pl.BlockSpec((1, t, SB_WIDTH), lambda bb: (bb, 0, 0))
    win_spec = pl.BlockSpec((None, None, SB_HEADS, SB_HEAD_DIM, window), lambda bb: (0, bb, 0, 0, past // window - 1))
    any_spec = pl.BlockSpec(memory_space=pl.ANY)
    return pl.pallas_call(
        functools.partial(_sb_sample_kernel, past=past, window=window, tk=tk),
        grid=(b,),
        in_specs=[new_spec, new_spec, new_spec, win_spec, win_spec, any_spec, any_spec,
                  pl.BlockSpec((1, LANES), lambda bb: (0, 0))],
        out_specs=new_spec,
        out_shape=jax.ShapeDtypeStruct((b, t, SB_WIDTH), BF16),
        scratch_shapes=[pltpu.VMEM((SB_PAIRS, t, LANES), F32), pltpu.VMEM((2 * SB_PAIRS, t, LANES), F32),
                        pltpu.VMEM((SB_HEADS, SB_HEAD_DIM, tk), F32), pltpu.VMEM((SB_HEADS, SB_HEAD_DIM, tk), F32),
                        pltpu.SemaphoreType.DMA((2,))],
        compiler_params=_cparams(("arbitrary",)),
        name="sb_sample",
    )(q, k_new, v_new, k_cache_t, v_cache_t, k_cache_t, v_cache_t, nsb_row)


def _gdn_chunk(nb, qkv_ref, gate_ref, ab_ref, cw_ref, alog_ref, dtb_ref, ng_ref, o_ref, xp_ref, s_ref, chunk):
    ii = lax.broadcasted_iota(jnp.int32, (chunk, chunk), 0)
    jj = lax.broadcasted_iota(jnp.int32, (chunk, chunk), 1)
    tri = jnp.where(ii >= jj, 1.0, 0.0).astype(F32)
    scale = GDN_HEAD_DIM ** -0.5
    hd = GDN_HEAD_DIM
    n_fac = int(math.log2(chunk)) - 1
    chains = [(bi, h) for bi in range(nb) for h in range(GDN_HEADS)]

    ys, gcs, gcts, betas = [], [], [], []
    for bi in range(nb):
        xp_ref[bi, 8:8 + chunk, :] = qkv_ref[bi]
        y = cw_ref[0:1, :] * xp_ref[bi, 5:5 + chunk, :]
        for i in range(1, CONV_WIDTH):
            y = y + cw_ref[i:i + 1, :] * xp_ref[bi, 5 + i:5 + i + chunk, :]
        ys.append(y * _sigmoid(y))
        xp_ref[bi, 0:8, :] = xp_ref[bi, chunk:chunk + 8, :]
        ab = ab_ref[bi]
        g_tile = -jnp.exp(alog_ref[...]) * _softplus(ab + dtb_ref[...])
        betas.append(_sigmoid(ab))
        gc_tile = jnp.dot(tri, g_tile, preferred_element_type=F32, precision=lax.Precision.HIGHEST)
        gcs.append(gc_tile)
        gcts.append(gc_tile.T)

    kn, qs, kb, vb, decay, egc, glast, gcol = {}, {}, {}, {}, {}, {}, {}, {}
    for ch in chains:
        bi, h = ch
        y = ys[bi]
        qh = y[:, h * hd:(h + 1) * hd]
        kh = y[:, GDN_WIDTH + h * hd:GDN_WIDTH + (h + 1) * hd]
        vh = y[:, 2 * GDN_WIDTH + h * hd:2 * GDN_WIDTH + (h + 1) * hd]
        qs[ch] = qh * lax.rsqrt(jnp.sum(qh * qh, axis=-1, keepdims=True) + L2_EPS) * scale
        kn[ch] = kh * lax.rsqrt(jnp.sum(kh * kh, axis=-1, keepdims=True) + L2_EPS)
        gcol[ch] = gcs[bi][:, h:h + 1]
        grow = gcts[bi][h:h + 1, :]
        glast[ch] = gcs[bi][chunk - 1:chunk, h:h + 1]
        bcol = betas[bi][:, GDN_HEADS + h:GDN_HEADS + h + 1]
        decay[ch] = jnp.exp(jnp.where(ii >= jj, gcol[ch] - grow, -1e30))
        egc[ch] = jnp.exp(gcol[ch])
        kb[ch] = kn[ch] * bcol
        vb[ch] = vh * bcol

    a2 = {ch: lax.dot_general(jnp.concatenate([kb[ch], qs[ch]], axis=0).astype(BF16), kn[ch].astype(BF16),
                              (((1,), (1,)), ((), ())), preferred_element_type=F32) for ch in chains}
    low = {ch: jnp.where(ii > jj, a2[ch][:chunk] * decay[ch], 0.0) for ch in chains}
    lb = {ch: low[ch].astype(BF16) for ch in chains}
    qk = {ch: (a2[ch][chunk:] * decay[ch]).astype(BF16) for ch in chains}

    m = {ch: -low[ch] for ch in chains}
    pw = {ch: jnp.dot(lb[ch], lb[ch], preferred_element_type=F32) for ch in chains}
    for f in range(n_fac):
        pb = {ch: pw[ch].astype(BF16) for ch in chains}
        m = {ch: m[ch] + pw[ch] + jnp.dot(pb[ch], m[ch].astype(BF16), preferred_element_type=F32) for ch in chains}
        if f + 1 < n_fac:
            pw = {ch: jnp.dot(pb[ch], pb[ch], preferred_element_type=F32) for ch in chains}
    rhs = {ch: jnp.concatenate([vb[ch], kb[ch] * egc[ch]], axis=1) for ch in chains}
    sol = {ch: rhs[ch] + jnp.dot(m[ch].astype(BF16), rhs[ch].astype(BF16), preferred_element_type=F32)
           for ch in chains}

    s_old = {ch: s_ref[ch[0], ch[1]] for ch in chains}
    ws = {ch: jnp.dot(jnp.concatenate([sol[ch][:, hd:], qs[ch] * egc[ch]], axis=0).astype(BF16),
                      s_old[ch].astype(BF16), preferred_element_type=F32) for ch in chains}
    v_new = {ch: (sol[ch][:, :hd] - ws[ch][:chunk]).astype(BF16) for ch in chains}
    o = {ch: ws[ch][chunk:] + jnp.dot(qk[ch], v_new[ch], preferred_element_type=F32) for ch in chains}
    for ch in chains:
        kd = (kn[ch] * jnp.exp(glast[ch] - gcol[ch])).astype(BF16)
        s_ref[ch[0], ch[1]] = s_old[ch] * jnp.exp(glast[ch]) + lax.dot_general(
            kd, v_new[ch], (((0,), (0,)), ((), ())), preferred_element_type=F32)

    for ch in chains:
        bi, h = ch
        sl = slice(h * hd, (h + 1) * hd)
        on = o[ch] * lax.rsqrt(jnp.mean(o[ch] * o[ch], axis=-1, keepdims=True) + NORM_EPS) * ng_ref[...]
        gt = gate_ref[bi][:, sl]
        o_ref[bi, :, sl] = (on * (gt * _sigmoid(gt))).astype(o_ref.dtype)


def _gdn_kernel(qkv_ref, gate_ref, ab_ref, cinit_ref, sinit_ref, cw_ref, alog_ref, dtb_ref, ng_ref,
                o_ref, cout_ref, sout_ref, xp_ref, s_ref, *, chunk, nb):
    c = pl.program_id(1)

    @pl.when(c == 0)
    def _():
        xp_ref[:, 0:8, :] = cinit_ref[...]
        s_ref[...] = sinit_ref[...]

    _gdn_chunk(nb, qkv_ref, gate_ref, ab_ref, cw_ref, alog_ref, dtb_ref, ng_ref, o_ref, xp_ref, s_ref, chunk)

    @pl.when(c == pl.num_programs(1) - 1)
    def _():
        cout_ref[...] = xp_ref[:, 0:8, :]
        sout_ref[...] = s_ref[...]


def _gdn(qkv, gate, ab, conv_init, s_init, cw8, alog_row, dtb_row, ng_row):
    b, l, _ = qkv.shape
    chunk = min(CHUNK, l)
    nc = l // chunk
    nb = min(GDN_BATCH_PER_STEP, b)
    assert b % nb == 0
    tok = lambda w: pl.BlockSpec((nb, chunk, w), lambda bb, c: (bb, c, 0))
    const = lambda shape: pl.BlockSpec(shape, lambda bb, c: (0,) * len(shape))
    per_b3 = pl.BlockSpec((nb, 8, CONV_CH), lambda bb, c: (bb, 0, 0))
    per_b4 = pl.BlockSpec((nb, GDN_HEADS, GDN_HEAD_DIM, GDN_HEAD_DIM), lambda bb, c: (bb, 0, 0, 0))
    return pl.pallas_call(
        functools.partial(_gdn_kernel, chunk=chunk, nb=nb),
        grid=(b // nb, nc),
        in_specs=[tok(CONV_CH), tok(GDN_WIDTH), tok(LANES), per_b3, per_b4,
                  const((8, CONV_CH)), const((1, LANES)), const((1, LANES)), const((1, LANES))],
        out_specs=(tok(GDN_WIDTH), per_b3, per_b4),
        out_shape=(jax.ShapeDtypeStruct((b, l, GDN_WIDTH), BF16),
                   jax.ShapeDtypeStruct((b, 8, CONV_CH), F32),
                   jax.ShapeDtypeStruct((b, GDN_HEADS, GDN_HEAD_DIM, GDN_HEAD_DIM), F32)),
        scratch_shapes=[pltpu.VMEM((nb, chunk + 8, CONV_CH), F32),
                        pltpu.VMEM((nb, GDN_HEADS, GDN_HEAD_DIM, GDN_HEAD_DIM), F32)],
        compiler_params=_cparams(("arbitrary", "arbitrary")),
        name="gdn",
    )(qkv, gate, ab, conv_init, s_init, cw8, alog_row, dtb_row, ng_row)


def _out_route_kernel(xp_ref, osbp_ref, ogdnp_ref, xs_ref, osbs_ref, ogdns_ref, wo_ref, nf_ref, rw_ref, rb_ref,
                      h_ref, xn_ref, route_ref, cnt_ref, run_ref, *, tm, steps_p):
    i = pl.program_id(0)

    @pl.when(i == 0)
    def _():
        run_ref[...] = jnp.zeros_like(run_ref)

    is_p = i < steps_p
    sub = tm // ROUTE_SUBTILES
    subs = range(ROUTE_SUBTILES)
    rows = [pl.ds(s * sub, sub) for s in subs]
    lane = lax.broadcasted_iota(jnp.int32, (1, LANES), 1)
    lane_f = lane.astype(F32)

    x = [jnp.where(is_p, xp_ref[r, :], xs_ref[r, :]) for r in rows]
    osb = [jnp.where(is_p, osbp_ref[r, :], osbs_ref[r, :]) for r in rows]
    ogdn = [jnp.where(is_p, ogdnp_ref[r, :], ogdns_ref[r, :]) for r in rows]
    h = [x[s] + jnp.dot(osb[s], wo_ref[0:SB_WIDTH, :], preferred_element_type=F32)
         + jnp.dot(ogdn[s], wo_ref[SB_WIDTH:, :], preferred_element_type=F32) for s in subs]
    for s in subs:
        h_ref[rows[s], :] = h[s]
    xn = [h[s] * lax.rsqrt(jnp.mean(h[s] * h[s], axis=-1, keepdims=True) + NORM_EPS) * nf_ref[...] for s in subs]
    for s in subs:
        _store_row_tiles(xn_ref, xn[s], sub, s * sub)

    rw = rw_ref[...]
    rw_hi = rw.astype(BF16)
    rw_lo = (rw - rw_hi.astype(F32)).astype(BF16)
    xn_hi = [xn[s].astype(BF16) for s in subs]
    xn_lo = [(xn[s] - xn_hi[s].astype(F32)).astype(BF16) for s in subs]
    logits = [jnp.dot(xn_hi[s], rw_hi, preferred_element_type=F32) + jnp.dot(xn_lo[s], rw_hi, preferred_element_type=F32)
              + jnp.dot(xn_hi[s], rw_lo, preferred_element_type=F32) + rb_ref[...] for s in subs]
    logits = [jnp.where(lane < N_EXPERTS, logits[s], -jnp.inf) for s in subs]
    vals = [[] for _ in subs]
    hots = [[] for _ in subs]
    for _ in range(TOP_K):
        for s in subs:
            m = jnp.max(logits[s], axis=-1, keepdims=True)
            idx = jnp.min(jnp.where(logits[s] == m, lane_f, float(LANES)), axis=-1, keepdims=True)
            hot = lane_f == idx
            logits[s] = jnp.where(hot, -jnp.inf, logits[s])
            vals[s].append(m)
            hots[s].append((hot, idx))

    ii = lax.broadcasted_iota(jnp.int32, (sub, sub), 0)
    jj = lax.broadcasted_iota(jnp.int32, (sub, sub), 1)
    earlier = jnp.where(ii > jj, 1.0, 0.0).astype(BF16)
    multi = []
    for s in subs:
        m_s = jnp.zeros((sub, LANES), F32)
        for hot, _ in hots[s]:
            m_s = jnp.where(hot, 1.0, m_s)
        multi.append(m_s)
    within = [jnp.dot(earlier, multi[s].astype(BF16), preferred_element_type=F32) for s in subs]
    run = run_ref[0:1, :]
    for s in subs:
        rank_all = within[s] + run
        run = run + jnp.sum(multi[s], axis=0, keepdims=True)
        exps = [jnp.exp(v - vals[s][0]) for v in vals[s]]
        denom = exps[0] + exps[1] + exps[2] + exps[3]
        route = jnp.zeros((sub, LANES), F32)
        for k, (hot, idx) in enumerate(hots[s]):
            rank = jnp.sum(jnp.where(hot, rank_all, 0.0), axis=-1, keepdims=True)
            route = jnp.where(lane == k, idx, route)
            route = jnp.where(lane == TOP_K + k, rank, route)
            route = jnp.where(lane == 2 * TOP_K + k, exps[k] / denom, route)
        route_ref[rows[s], :] = route
    run_ref[...] = jnp.broadcast_to(run, run_ref.shape)
    cnt_ref[...] = jnp.broadcast_to(run, cnt_ref.shape)


def _out_route(xp, osbp, ogdnp, xs, osbs, ogdns, wo_bf, nf_row, rw_pad, rb_row):
    n_p, n_s = xp.shape[0], xs.shape[0]
    tm = min(TM_ROUTE, n_p, n_s)
    steps_p, steps_s = n_p // tm, n_s // tm
    prow = lambda w: pl.BlockSpec((tm, w), lambda i: (jnp.minimum(i, steps_p - 1), 0))
    srow = lambda w: pl.BlockSpec((tm, w), lambda i: (jnp.maximum(i - steps_p, 0), 0))
    orow = lambda w: pl.BlockSpec((tm, w), lambda i: (i, 0))
    const = lambda shape: pl.BlockSpec(shape, lambda i: (0,) * len(shape))
    n_total = n_p + n_s
    out_shape = (jax.ShapeDtypeStruct((n_total, D_MODEL), F32),
                 jax.ShapeDtypeStruct((n_total * ROW_TILE, LANES), F32),
                 jax.ShapeDtypeStruct((n_total, LANES), F32),
                 jax.ShapeDtypeStruct((8, LANES), F32))
    return pl.pallas_call(
        functools.partial(_out_route_kernel, tm=tm, steps_p=steps_p),
        grid=(steps_p + steps_s,),
        in_specs=[prow(D_MODEL), prow(SB_WIDTH), prow(GDN_WIDTH), srow(D_MODEL), srow(SB_WIDTH), srow(GDN_WIDTH),
                  const((D_MODEL, D_MODEL)), const((1, D_MODEL)), const((D_MODEL, LANES)), const((1, LANES))],
        out_specs=(orow(D_MODEL), pl.BlockSpec((tm * ROW_TILE, LANES), lambda i: (i, 0)), orow(LANES),
                   const((8, LANES))),
        out_shape=out_shape,
        scratch_shapes=[pltpu.VMEM((8, LANES), F32)],
        compiler_params=_cparams(("arbitrary",)),
        name="out_route",
    )(xp, osbp, ogdnp, xs, osbs, ogdns, wo_bf, nf_row, rw_pad, rb_row)


def _scatter_rows_kernel(pos_ref, x_ref, xs_hbm, sem, *, tm):
    def row_copy(t, p):
        return pltpu.make_async_copy(x_ref.at[pl.ds(pl.multiple_of(t * ROW_TILE, ROW_TILE), ROW_TILE)],
                                     xs_hbm.at[pl.ds(pl.multiple_of(p * ROW_TILE, ROW_TILE), ROW_TILE)], sem)

    def issue(t, carry):
        for k in range(TOP_K):
            row_copy(t, pos_ref[k, t]).start(priority=k % 2)
        return carry

    lax.fori_loop(0, tm, issue, 0)

    def drain(t, carry):
        for k in range(TOP_K):
            row_copy(0, 0).wait()
        return carry

    lax.fori_loop(0, tm, drain, 0)


def _scatter_rows(pos_t, xn):
    n = xn.shape[0] // ROW_TILE
    tm = min(TM_ROWS, n)
    return pl.pallas_call(
        functools.partial(_scatter_rows_kernel, tm=tm),
        grid=(n // tm,),
        in_specs=[pl.BlockSpec((TOP_K, tm), lambda i: (0, i), memory_space=pltpu.SMEM),
                  pl.BlockSpec((tm * ROW_TILE, LANES), lambda i: (i, 0))],
        out_specs=pl.BlockSpec(memory_space=pl.ANY),
        out_shape=jax.ShapeDtypeStruct((n * TOP_K * ROW_TILE, LANES), xn.dtype),
        scratch_shapes=[pltpu.SemaphoreType.DMA(())],
        compiler_params=_cparams(("arbitrary",)),
        name="moe_scatter",
    )(pos_t, xn)


def _experts_kernel(tile_ref, exp_ref, lo_ref, hi_ref, first_ref, newexp_ref, nvalid_ref,
                    x_ref, wg_ref, bg_ref, wu_ref, bu_ref, wd_ref, bd_ref, y_ref, wg_bf, wu_bf, wd_bf, *, tmg):
    v = pl.program_id(0)

    @pl.when(jnp.logical_and(v < nvalid_ref[0], newexp_ref[v] == 1))
    def _():
        wg_bf[...] = wg_ref[0].astype(BF16)
        wu_bf[...] = wu_ref[0].astype(BF16)
        wd_bf[...] = wd_ref[0].astype(BF16)

    @pl.when(v < nvalid_ref[0])
    def _():
        sub = tmg // EXPERT_SUBTILES
        subs = range(EXPERT_SUBTILES)
        xs = [_load_row_tiles(x_ref, sub, s * sub).astype(BF16) for s in subs]
        gt, up = [], []
        for s in subs:
            gt.append(jnp.dot(xs[s], wg_bf[...], preferred_element_type=F32))
            up.append(jnp.dot(xs[s], wu_bf[...], preferred_element_type=F32))
        hid = []
        for s in subs:
            g = jnp.minimum(gt[s] + bg_ref[0], SWIGLU_LIMIT)
            u = jnp.clip(up[s] + bu_ref[0], -SWIGLU_LIMIT, SWIGLU_LIMIT)
            hid.append(((u + 1.0) * (g * _sigmoid(SWIGLU_ALPHA * g))).astype(BF16))
        ys = [jnp.dot(hid[s], wd_bf[...], preferred_element_type=F32) + bd_ref[0] for s in subs]
        mine = []
        for s in subs:
            rows = tile_ref[v] * tmg + s * sub + lax.broadcasted_iota(jnp.int32, (sub, 1), 0)
            mine.append(jnp.logical_and(rows >= lo_ref[v], rows < hi_ref[v]))

        @pl.when(first_ref[v] == 1)
        def _():
            for s in subs:
                _store_row_tiles(y_ref, jnp.where(mine[s], ys[s], 0.0), sub, s * sub)

        @pl.when(first_ref[v] == 0)
        def _():
            for s in subs:
                _store_row_tiles(y_ref, jnp.where(mine[s], ys[s], _load_row_tiles(y_ref, sub, s * sub)), sub, s * sub)


def _experts(meta, xs, wg, bg, wu, bu, wd, bd):
    n4 = xs.shape[0] // ROW_TILE
    tmg = min(TM_GROUP, n4)
    n_visits = meta[0].shape[0]
    xspec = pl.BlockSpec((tmg * ROW_TILE, LANES), lambda v, tile, exp, *_: (tile[v], 0))
    wspec = pl.BlockSpec((1, D_MODEL, D_MODEL), lambda v, tile, exp, *_: (exp[v], 0, 0))
    bspec = pl.BlockSpec((1, 1, D_MODEL), lambda v, tile, exp, *_: (exp[v], 0, 0))
    grid_spec = pltpu.PrefetchScalarGridSpec(
        num_scalar_prefetch=len(meta),
        grid=(n_visits,),
        in_specs=[xspec, wspec, bspec, wspec, bspec, wspec, bspec],
        out_specs=xspec,
        scratch_shapes=[pltpu.VMEM((D_MODEL, D_MODEL), BF16)] * 3,
    )
    return pl.pallas_call(
        functools.partial(_experts_kernel, tmg=tmg),
        grid_spec=grid_spec,
        out_shape=jax.ShapeDtypeStruct((n4 * ROW_TILE, LANES), F32),
        compiler_params=_cparams(("arbitrary",), VMEM_LIMIT_EXPERTS),
        name="moe_experts",
    )(*meta, xs, wg, bg, wu, bu, wd, bd)


def _combine_kernel(pos_ref, posn_ref, h_ref, route_ref, nfin_ref, ys_hbm, o_ref, buf, sem, *, tm):
    i = pl.program_id(0)
    slot = lax.rem(i, 2)

    def row_copy(s, t, k, p):
        return pltpu.make_async_copy(ys_hbm.at[pl.ds(pl.multiple_of(p * ROW_TILE, ROW_TILE), ROW_TILE)],
                                     buf.at[s, k, pl.ds(pl.multiple_of(t * ROW_TILE, ROW_TILE), ROW_TILE)],
                                     sem.at[s])

    def start_gathers(p_ref, s):
        def issue(t, carry):
            for k in range(TOP_K):
                row_copy(s, t, k, p_ref[k, t]).start(priority=k % 2)
            return carry

        lax.fori_loop(0, tm, issue, 0)

    @pl.when(i == 0)
    def _():
        start_gathers(pos_ref, slot)

    @pl.when(i + 1 < pl.num_programs(0))
    def _():
        start_gathers(posn_ref, 1 - slot)

    def drain(t, carry):
        for k in range(TOP_K):
            row_copy(slot, 0, k, 0).wait()
        return carry

    lax.fori_loop(0, tm, drain, 0)

    route = route_ref[...]
    h = h_ref[...]
    pieces = []
    for j in range(ROW_TILE):
        piece = h[:, j * LANES:(j + 1) * LANES]
        for k in range(TOP_K):
            piece = piece + route[:, 2 * TOP_K + k:2 * TOP_K + k + 1] * buf[slot, k, pl.ds(j, tm, stride=ROW_TILE), :]
        pieces.append(piece)
    out = jnp.concatenate(pieces, axis=-1)
    o_ref[...] = out * lax.rsqrt(jnp.mean(out * out, axis=-1, keepdims=True) + NORM_EPS) * nfin_ref[...]


def _combine(pos_t, h, route, nfin_row, ys, row0, n):
    tm = min(TM_ROWS, n)
    off = row0 // tm
    steps = n // tm
    return pl.pallas_call(
        functools.partial(_combine_kernel, tm=tm),
        grid=(steps,),
        in_specs=[pl.BlockSpec((TOP_K, tm), lambda i: (0, i + off), memory_space=pltpu.SMEM),
                  pl.BlockSpec((TOP_K, tm), lambda i: (0, jnp.minimum(i + 1, steps - 1) + off),
                               memory_space=pltpu.SMEM),
                  pl.BlockSpec((tm, D_MODEL), lambda i: (i + off, 0)),
                  pl.BlockSpec((tm, LANES), lambda i: (i + off, 0)),
                  pl.BlockSpec((1, D_MODEL), lambda i: (0, 0)),
                  pl.BlockSpec(memory_space=pl.ANY)],
        out_specs=pl.BlockSpec((tm, D_MODEL), lambda i: (i, 0)),
        out_shape=jax.ShapeDtypeStruct((n, D_MODEL), F32),
        scratch_shapes=[pltpu.VMEM((2, TOP_K, tm * ROW_TILE, LANES), F32), pltpu.SemaphoreType.DMA((2,))],
        compiler_params=_cparams(("arbitrary",)),
        name="moe_combine",
    )(pos_t, pos_t, h, route, nfin_row, ys)


def _group_plan(counts, n4, tmg):
    n_tiles = n4 // tmg
    n_visits = n_tiles + N_EXPERTS - 1
    ends = jnp.cumsum(counts)
    starts = ends - counts
    t_first = starts // tmg
    t_cnt = jnp.where(counts > 0, (ends - 1) // tmg - t_first + 1, 0)
    v_end = jnp.cumsum(t_cnt)
    v_start = v_end - t_cnt
    total = v_end[-1]
    v = jnp.arange(n_visits, dtype=jnp.int32)
    g = jnp.minimum(jnp.sum((v[:, None] >= v_end[None, :]).astype(jnp.int32), axis=1), N_EXPERTS - 1)
    g_last = jnp.max(jnp.where(counts > 0, jnp.arange(N_EXPERTS), 0)).astype(jnp.int32)
    valid = v < total
    g = jnp.where(valid, g, g_last)
    hot = g[:, None] == jnp.arange(N_EXPERTS, dtype=jnp.int32)[None, :]
    pick = lambda table: jnp.sum(jnp.where(hot, table[None, :], 0), axis=1)
    tile = jnp.where(valid, pick(t_first) + (v - pick(v_start)), n_tiles - 1).astype(jnp.int32)
    lo = jnp.where(valid, jnp.maximum(pick(starts), tile * tmg), 0).astype(jnp.int32)
    hi = jnp.where(valid, jnp.minimum(pick(ends), (tile + 1) * tmg), 0).astype(jnp.int32)
    prev_g = jnp.concatenate([jnp.full((1,), -1, jnp.int32), g[:-1]])
    newexp = (g != prev_g).astype(jnp.int32)
    prev_tile = jnp.concatenate([jnp.full((1,), -1, jnp.int32), tile[:-1]])
    first = (tile != prev_tile).astype(jnp.int32)
    meta = (tile, g, lo, hi, first, newexp, total.reshape(1).astype(jnp.int32))
    return meta, starts


def _pad_rows(a, rows):
    return jnp.concatenate([a, jnp.zeros((rows - a.shape[0],) + a.shape[1:], a.dtype)], axis=0)


def kernel(x_prompt, x_sample, cache_sb_k, cache_sb_v, cache_gdn_conv, state_gdn, norm_mix, w_in, conv_w, a_log,
           dt_bias, norm_sb, norm_gdn, w_out, norm_ffn, router_w, router_b, w_gate, b_gate, w_up, b_up, w_down,
           b_down, norm_final):
    bp, lp, _ = x_prompt.shape
    bs, ls, _ = x_sample.shape
    past = cache_sb_k.shape[2]
    n_p, n_s = bp * lp, bs * ls
    n_tot = n_p + n_s
    assert norm_mix.shape[0] == 1, "single-layer trunk"
    for tile_rows in (TM_DENSE, TM_ROUTE, TM_ROWS):
        assert n_p % min(tile_rows, n_p, n_s) == 0 and n_s % min(tile_rows, n_p, n_s) == 0
    assert (n_tot * TOP_K) % TM_GROUP == 0 and n_tot % TM_ROWS == 0
    assert lp % min(TQ_SB, lp) == 0 and lp % min(CHUNK, lp) == 0 and ls % min(CHUNK, ls) == 0 and ls >= 8

    w_in_bf = jnp.pad(w_in[0], ((0, 0), (0, IN_WIDTH_PAD - IN_WIDTH))).astype(BF16)
    nmix_row = norm_mix[0].reshape(1, D_MODEL)
    nsb_row = jnp.tile(norm_sb[0], 2).reshape(1, LANES)
    ng_row = norm_gdn[0].reshape(1, LANES)
    cw8 = _pad_rows(conv_w[0], 8)
    alog_row = jnp.pad(a_log[0], (0, LANES - GDN_HEADS)).reshape(1, LANES)
    dtb_row = jnp.pad(dt_bias[0], (0, LANES - GDN_HEADS)).reshape(1, LANES)
    wo_bf = w_out[0].astype(BF16)
    nf_row = norm_ffn[0].reshape(1, D_MODEL)
    rw_pad = jnp.pad(router_w[0], ((0, 0), (0, LANES - N_EXPERTS)))
    rb_row = jnp.pad(router_b[0], (0, LANES - N_EXPERTS)).reshape(1, LANES)
    nfin_row = norm_final.reshape(1, D_MODEL)

    xp2 = x_prompt.reshape(n_p, D_MODEL)
    xs2 = x_sample.reshape(n_s, D_MODEL)

    qsb, k_prompt, v_prompt, kbf, vbf, gdn_in, gate, ab = _in_proj(xp2, nmix_row, w_in_bf)
    r3 = lambda a, b, l: a.reshape(b, l, a.shape[-1])
    osb_p = _sb_prompt(r3(qsb, bp, lp), r3(kbf, bp, lp), r3(vbf, bp, lp), nsb_row)
    ogdn_p, conv_p, state_p = _gdn(
        r3(gdn_in, bp, lp), r3(gate, bp, lp), r3(ab, bp, lp),
        jnp.zeros((bp, 8, CONV_CH), F32), jnp.zeros((bp, GDN_HEADS, GDN_HEAD_DIM, GDN_HEAD_DIM), F32),
        cw8, alog_row, dtb_row, ng_row)

    qsb, k_sample, v_sample, kbf, vbf, gdn_in, gate, ab = _in_proj(xs2, nmix_row, w_in_bf)
    keys_minor = lambda cache: jnp.transpose(cache, (0, 1, 3, 4, 2))
    osb_s = _sb_sample(r3(qsb, bs, ls), r3(kbf, bs, ls), r3(vbf, bs, ls), keys_minor(cache_sb_k),
                       keys_minor(cache_sb_v), nsb_row)
    conv_init = jnp.concatenate([jnp.zeros((bs, 8 - (CONV_WIDTH - 1), CONV_CH), F32), cache_gdn_conv[0]], axis=1)
    ogdn_s, conv_s, state_s = _gdn(r3(gdn_in, bs, ls), r3(gate, bs, ls), r3(ab, bs, ls), conv_init, state_gdn[0],
                                   cw8, alog_row, dtb_row, ng_row)

    h_buf, xn_buf, route_buf, cnt = _out_route(
        xp2, osb_p.reshape(n_p, SB_WIDTH), ogdn_p.reshape(n_p, GDN_WIDTH),
        xs2, osb_s.reshape(n_s, SB_WIDTH), ogdn_s.reshape(n_s, GDN_WIDTH), wo_bf, nf_row, rw_pad, rb_row)

    counts = cnt[0, :N_EXPERTS].astype(jnp.int32)
    n4 = n_tot * TOP_K
    tmg = min(TM_GROUP, n4)
    meta, starts = _group_plan(counts, n4, tmg)
    idx_t = route_buf[:, 0:TOP_K].T.astype(jnp.int32)
    rank_t = route_buf[:, TOP_K:2 * TOP_K].T.astype(jnp.int32)
    pos_t = rank_t
    for e in range(N_EXPERTS):
        pos_t = pos_t + jnp.where(idx_t == e, starts[e], 0)

    xs_sorted = _scatter_rows(pos_t, xn_buf)
    e3 = lambda bias: bias[0].reshape(N_EXPERTS, 1, D_MODEL)
    ys_sorted = _experts(meta, xs_sorted, w_gate[0], e3(b_gate), w_up[0], e3(b_up), w_down[0], e3(b_down))
    y_prompt = _combine(pos_t, h_buf, route_buf, nfin_row, ys_sorted, 0, n_p).reshape(bp, lp, D_MODEL)
    y_sample = _combine(pos_t, h_buf, route_buf, nfin_row, ys_sorted, n_p, n_s).reshape(bs, ls, D_MODEL)

    heads = lambda a, b, l: a.reshape(1, b, l, SB_HEADS, SB_HEAD_DIM)
    return (y_prompt, y_sample,
            heads(k_prompt, bp, lp), heads(v_prompt, bp, lp),
            conv_p[:, 8 - (CONV_WIDTH - 1):][None], state_p[None],
            heads(k_sample, bs, ls), heads(v_sample, bs, ls),
            conv_s[:, 8 - (CONV_WIDTH - 1):][None], state_s[None])
```
